```python
import math
import jax, jax.numpy as jnp
from jax import lax
import numpy as np

D_MODEL = 1024
BATCH = 4
SEQ = 8192
DEPTH = 2

HEAD_DIM = 64
D_MIX = D_MODEL
D_CONV = D_MIX // 4
D_SSM = D_MIX // 4
D_SB = D_MIX // 4
D_NSA = D_MIX - D_CONV - D_SSM - D_SB

CONV_WIDTH = 3

SSM_GROUP = 16
SSM_N_GROUPS = D_SSM // SSM_GROUP
SSM_STATE = 64
SSM_DT_MIN = 1e-3
SSM_DT_MAX = 1e-1
SSM_MAX_RE = -1e-4

SB_HEADS = D_SB // HEAD_DIM

NSA_HEADS = D_NSA // HEAD_DIM
NSA_KV_HEADS = 2
NSA_HPG = NSA_HEADS // NSA_KV_HEADS
CMP_LEN = 32
CMP_STRIDE = 16
CMP_HIDDEN = 4 * HEAD_DIM
SLC_BLOCK = 64
SLC_TOPK = 16
WINDOW = 512
FORCE_BONUS = 1e4
NEG_INF = -1e30

Q_BLOCK = 128
ROPE_THETA = 10000.0
RMS_EPS = 1e-6

D_FF = 7 * D_MODEL // 2
N_EXPERTS = 8
TOP_K = 2
N_DENSE = (DEPTH + 1) // 2
N_MOE = DEPTH // 2

N_CONV_COLS = 3 * D_CONV
N_SSM_COLS = D_SSM
N_SB_COLS = 3 * D_SB
N_NSA_KV_COLS = 6 * NSA_KV_HEADS * HEAD_DIM
N_NSA_COLS = D_NSA + N_NSA_KV_COLS + 3 * NSA_HEADS
N_IN = N_CONV_COLS + N_SSM_COLS + N_SB_COLS + N_NSA_COLS

kernel_name = 'hybrid_parallel_mixer_block'


def rms_norm(x, g):
    xf = x.astype(jnp.float32)
    y = xf * lax.rsqrt(jnp.mean(xf * xf, axis=-1, keepdims=True) + RMS_EPS)
    return (y * g.astype(jnp.float32)).astype(x.dtype)


def rope(x, pos):
    half = HEAD_DIM // 2
    inv_freq = jnp.power(jnp.float32(ROPE_THETA), -jnp.arange(half, dtype=jnp.float32) / half)
    ang = pos.astype(jnp.float32)[..., None] * inv_freq
    cos = jnp.cos(ang)[:, :, None, :]
    sin = jnp.sin(ang)[:, :, None, :]
    xf = x.astype(jnp.float32)
    x1, x2 = xf[..., :half], xf[..., half:]
    return jnp.concatenate([x1 * cos - x2 * sin, x2 * cos + x1 * sin], axis=-1).astype(x.dtype)


def masked_softmax(scores, mask):
    p = jax.nn.softmax(jnp.where(mask, scores, NEG_INF), axis=-1)
    return jnp.where(mask, p, 0.0)


def to_blocks(a):
    return a.reshape(a.shape[0], a.shape[1] // Q_BLOCK, Q_BLOCK, *a.shape[2:]).swapaxes(0, 1)


def from_blocks(a):
    a = a.swapaxes(0, 1)
    return a.reshape(a.shape[0], a.shape[1] * a.shape[2], -1)


def short_conv_mixer(z, conv_w):
    gate_b, gate_c, u = jnp.split(z, 3, axis=-1)
    v = gate_c * u
    y = lax.conv_general_dilated(
        v, conv_w.astype(v.dtype)[:, None, :], window_strides=(1,),
        padding=[(CONV_WIDTH - 1, 0)], dimension_numbers=('NWC', 'WIO', 'NWC'),
        feature_group_count=D_CONV)
    return gate_b * y


def s5_mixer(u, lam_re, lam_im, b_re, b_im, c_re, c_im, d_skip, log_dt, w_glu):
    f32 = jnp.float32
    Bsz, T, _ = u.shape
    uf = u.astype(f32).reshape(Bsz, T, SSM_N_GROUPS, SSM_GROUP)
    lam = lax.complex(jnp.minimum(lam_re.astype(f32), SSM_MAX_RE), lam_im.astype(f32))
    dt = jnp.exp(log_dt.astype(f32))[:, None]
    lam_bar = jnp.exp(lam * dt)
    b_bar = ((lam_bar - 1.0) / lam)[..., None] * lax.complex(b_re.astype(f32), b_im.astype(f32))
    bu = jnp.einsum('gpc,btgc->btgp', b_bar, uf)
    a = jnp.broadcast_to(lam_bar, bu.shape)

    def combine(left, right):
        a_l, b_l = left
        a_r, b_r = right
        return a_r * a_l, a_r * b_l + b_r

    _, hs = lax.associative_scan(combine, (a, bu), axis=1)
    cmat = lax.complex(c_re.astype(f32), c_im.astype(f32))
    y = jnp.einsum('gcp,btgp->btgc', cmat, hs).real + d_skip.astype(f32).reshape(SSM_N_GROUPS, SSM_GROUP) * uf
    g = jax.nn.gelu(y.reshape(Bsz, T, D_SSM))
    out = g * jax.nn.sigmoid(g @ w_glu.astype(f32))
    return out.astype(u.dtype)


def stick_breaking_attention(q, k, v):
    f32 = jnp.float32
    Bsz, T, H, dh = q.shape
    scale = dh ** -0.5
    kf = k.astype(f32)
    vf = v.astype(f32)
    key_pos = jnp.arange(T)

    def block(args):
        q_blk, start = args
        z = jnp.einsum('bqhd,bshd->bhqs', q_blk.astype(f32), kf) * scale
        t = start + jnp.arange(Q_BLOCK)
        past = key_pos[None, :] < t[:, None]
        log_beta = jax.nn.log_sigmoid(z)
        log_keep = jnp.where(past, log_beta - z, 0.0)
        log_w = log_beta + lax.cumsum(log_keep, axis=3, reverse=True) - log_keep
        w = jnp.where(past, jnp.exp(log_w), 0.0)
        return jnp.einsum('bhqs,bshd->bqhd', w, vf)

    starts = jnp.arange(T // Q_BLOCK) * Q_BLOCK
    out = lax.map(block, (to_blocks(q), starts))
    return from_blocks(out).astype(q.dtype)


def compress_tokens(x, pos_emb, w1, w2):
    Bsz, T, G, dh = x.shape
    r = CMP_LEN // CMP_STRIDE
    nc = T // CMP_STRIDE - r + 1
    chunks = x.reshape(Bsz, T // CMP_STRIDE, CMP_STRIDE, G, dh)
    win = jnp.concatenate([chunks[:, j:j + nc] for j in range(r)], axis=2)
    win = win + pos_emb[:, None, :]
    win = win.transpose(0, 1, 3, 2, 4).reshape(Bsz, nc, G, CMP_LEN * dh)
    return jax.nn.silu(win @ w1) @ w2


def nsa_mixer(q, kv, gate_logits, positions, q_norm_g, k_norm_g, pos_k, pos_v, k_w1, k_w2, v_w1, v_w2):
    f32 = jnp.float32
    Bsz, T, _ = q.shape
    G, HPG, dh = NSA_KV_HEADS, NSA_HPG, HEAD_DIM
    scale = dh ** -0.5
    qh = rope(rms_norm(q.reshape(Bsz, T, NSA_HEADS, dh), q_norm_g), positions)
    qh = qh.astype(f32).reshape(Bsz, T, G, HPG, dh)
    k_cmp, v_cmp, k_slc, v_slc, k_win, v_win = jnp.moveaxis(kv.reshape(Bsz, T, 6, G, dh), 2, 0)

    nc = T // CMP_STRIDE - CMP_LEN // CMP_STRIDE + 1
    cmp_end = jnp.arange(nc) * CMP_STRIDE + CMP_LEN - 1
    kc = rope(rms_norm(compress_tokens(k_cmp, pos_k, k_w1, k_w2), k_norm_g), positions[:, cmp_end]).astype(f32)
    vc = compress_tokens(v_cmp, pos_v, v_w1, v_w2).astype(f32)

    ns = T // SLC_BLOCK
    top_n = min(SLC_TOPK, ns)
    ks = rope(rms_norm(k_slc, k_norm_g), positions).astype(f32)
    ks = ks.reshape(Bsz, ns, SLC_BLOCK, G, dh).transpose(0, 3, 1, 2, 4)
    vs = v_slc.astype(f32).reshape(Bsz, ns, SLC_BLOCK, G, dh).transpose(0, 3, 1, 2, 4)
    cmp_start = jnp.arange(nc) * CMP_STRIDE
    slc_start = jnp.arange(ns) * SLC_BLOCK
    overlap = ((cmp_start[:, None] < slc_start[None, :] + SLC_BLOCK)
               & (cmp_start[:, None] + CMP_LEN > slc_start[None, :])).astype(f32)
    blk_ids = jnp.arange(ns)
    b_ix = jnp.arange(Bsz)[:, None, None, None]
    g_ix = jnp.arange(G)[None, None, :, None]

    pad = ((0, 0), (WINDOW, 0), (0, 0), (0, 0))
    kw = jnp.pad(rope(rms_norm(k_win, k_norm_g), positions).astype(f32), pad)
    vw = jnp.pad(v_win.astype(f32), pad)

    gates = jax.nn.sigmoid(gate_logits.astype(f32)).reshape(Bsz, T, G, HPG, 3)

    def block(args):
        q_blk, g_blk, start = args
        t = start + jnp.arange(Q_BLOCK)
        m_c = (cmp_end[None, :] <= t[:, None])[None, :, None, None, :]
        p_c = masked_softmax(jnp.einsum('bqghd,bngd->bqghn', q_blk, kc) * scale, m_c)
        o_c = jnp.einsum('bqghn,bngd->bqghd', p_c, vc)
        imp = jnp.einsum('bqghn,ns->bqgs', p_c, overlap)
        cur = (t // SLC_BLOCK)[:, None]
        forced = (blk_ids[None, :] == 0) | (blk_ids[None, :] == cur) | (blk_ids[None, :] == cur - 1)
        causal = blk_ids[None, :] <= cur
        imp = jnp.where(causal[None, :, None, :], imp + FORCE_BONUS * forced[None, :, None, :], NEG_INF)
        _, idx = lax.top_k(imp, top_n)
        k_sel = ks[b_ix, g_ix, idx]
        v_sel = vs[b_ix, g_ix, idx].reshape(Bsz, Q_BLOCK, G, top_n * SLC_BLOCK, dh)
        key_pos = (idx[..., None] * SLC_BLOCK + jnp.arange(SLC_BLOCK)).reshape(Bsz, Q_BLOCK, G, 1, top_n * SLC_BLOCK)
        m_s = key_pos <= t[None, :, None, None, None]
        s_s = jnp.einsum('bqghd,bqgksd->bqghks', q_blk, k_sel).reshape(Bsz, Q_BLOCK, G, HPG, top_n * SLC_BLOCK) * scale
        o_s = jnp.einsum('bqghm,bqgmd->bqghd', masked_softmax(s_s, m_s), v_sel)
        k_w = lax.dynamic_slice_in_dim(kw, start, WINDOW + Q_BLOCK, axis=1)
        v_w = lax.dynamic_slice_in_dim(vw, start, WINDOW + Q_BLOCK, axis=1)
        wpos = start - WINDOW + jnp.arange(WINDOW + Q_BLOCK)
        m_w = ((wpos[None, :] <= t[:, None]) & (wpos[None, :] > t[:, None] - WINDOW)
               & (wpos[None, :] >= 0))[None, :, None, None, :]
        p_w = masked_softmax(jnp.einsum('bqghd,bsgd->bqghs', q_blk, k_w) * scale, m_w)
        o_w = jnp.einsum('bqghs,bsgd->bqghd', p_w, v_w)
        return g_blk[..., 0:1] * o_c + g_blk[..., 1:2] * o_s + g_blk[..., 2:3] * o_w

    starts = jnp.arange(T // Q_BLOCK) * Q_BLOCK
    out = lax.map(block, (to_blocks(qh), to_blocks(gates), starts))
    return from_blocks(out).astype(q.dtype)


def hybrid_token_mixing(a, positions, w_in, w_out, conv_w, lam_re, lam_im, b_re, b_im, c_re, c_im, d_skip,
                        log_dt, w_glu, q_norm_g, k_norm_g, pos_k, pos_v, k_w1, k_w2, v_w1, v_w2):
    Bsz, T, _ = a.shape
    z = a @ w_in
    o1 = N_CONV_COLS
    o2 = o1 + N_SSM_COLS
    o3 = o2 + N_SB_COLS
    o4 = o3 + D_NSA
    o5 = o4 + N_NSA_KV_COLS
    y_conv = short_conv_mixer(z[..., :o1], conv_w)
    y_ssm = s5_mixer(z[..., o1:o2], lam_re, lam_im, b_re, b_im, c_re, c_im, d_skip, log_dt, w_glu)
    sb = z[..., o2:o3].reshape(Bsz, T, 3, SB_HEADS, HEAD_DIM)
    y_sb = stick_breaking_attention(sb[:, :, 0], sb[:, :, 1], sb[:, :, 2])
    y_nsa = nsa_mixer(z[..., o3:o4], z[..., o4:o5], z[..., o5:], positions, q_norm_g, k_norm_g,
                      pos_k, pos_v, k_w1, k_w2, v_w1, v_w2)
    y = jnp.concatenate([y_conv, y_ssm, y_sb, y_nsa], axis=-1).astype(a.dtype)
    return y @ w_out


def swiglu(h, w_gate, w_up, w_down):
    return (jax.nn.silu(h @ w_gate) * (h @ w_up)) @ w_down


def moe_swiglu(h, router_w, w_gate, w_up, w_down):
    logits = (h @ router_w).astype(jnp.float32)
    top_val, top_idx = lax.top_k(logits, TOP_K)
    top_w = jax.nn.softmax(top_val, axis=-1)
    gate = jnp.sum(jax.nn.one_hot(top_idx, N_EXPERTS, dtype=jnp.float32) * top_w[..., None], axis=-2)
    gate = gate.astype(h.dtype)
    out = jnp.zeros_like(h)
    for e in range(N_EXPERTS):
        out = out + gate[..., e:e + 1] * swiglu(h, w_gate[e], w_up[e], w_down[e])
    return out


def setup_inputs(seed: int = 0) -> dict:
    key = jax.random.key(seed)
    keys = iter(jax.random.split(key, 64))

    def nrm(shape, scale):
        return jax.random.normal(next(keys), shape, jnp.float32) * scale

    D, F, E = D_MODEL, D_FF, N_EXPERTS
    G, P, HC = SSM_N_GROUPS, SSM_STATE, SSM_GROUP
    x = nrm((BATCH, SEQ, D), 1.0)
    c = nrm((BATCH, D), 1.0)
    positions = (jnp.arange(SEQ, dtype=jnp.int32)[None, :]
                 + jax.random.randint(next(keys), (BATCH, 1), 0, 1024, dtype=jnp.int32))
    return {
        'x': x,
        'c': c,
        'positions': positions,
        'ada_w': nrm((DEPTH, D, 6 * D), 0.5 * D ** -0.5),
        'ada_b': nrm((DEPTH, 6 * D), 0.01),
        'norm_mix_g': 1.0 + nrm((DEPTH, D), 0.02),
        'norm_ffn_g': 1.0 + nrm((DEPTH, D), 0.02),
        'w_in': nrm((DEPTH, D, N_IN), D ** -0.5),
        'w_out': nrm((DEPTH, D_MIX, D), D_MIX ** -0.5),
        'conv_w': nrm((DEPTH, CONV_WIDTH, D_CONV), CONV_WIDTH ** -0.5),
        'ssm_lam_re': -0.5 + nrm((DEPTH, G, P), 0.01),
        'ssm_lam_im': math.pi * jnp.arange(P, dtype=jnp.float32) + nrm((DEPTH, G, P), 0.01),
        'ssm_b_re': nrm((DEPTH, G, P, HC), (2 * HC) ** -0.5),
        'ssm_b_im': nrm((DEPTH, G, P, HC), (2 * HC) ** -0.5),
        'ssm_c_re': nrm((DEPTH, G, HC, P), P ** -0.5),
        'ssm_c_im': nrm((DEPTH, G, HC, P), P ** -0.5),
        'ssm_d': nrm((DEPTH, D_SSM), 1.0),
        'ssm_log_dt': jax.random.uniform(next(keys), (DEPTH, G), dtype=jnp.float32,
                                         minval=math.log(SSM_DT_MIN), maxval=math.log(SSM_DT_MAX)),
        'ssm_w_glu': nrm((DEPTH, D_SSM, D_SSM), D_SSM ** -0.5),
        'nsa_q_norm_g': 1.0 + nrm((DEPTH, HEAD_DIM), 0.02),
        'nsa_k_norm_g': 1.0 + nrm((DEPTH, HEAD_DIM), 0.02),
        'cmp_pos_k': nrm((DEPTH, CMP_LEN, HEAD_DIM), 0.1),
        'cmp_pos_v': nrm((DEPTH, CMP_LEN, HEAD_DIM), 0.1),
        'cmp_k_w1': nrm((DEPTH, CMP_LEN * HEAD_DIM, CMP_HIDDEN), (CMP_LEN * HEAD_DIM) ** -0.5),
        'cmp_k_w2': nrm((DEPTH, CMP_HIDDEN, HEAD_DIM), CMP_HIDDEN ** -0.5),
        'cmp_v_w1': nrm((DEPTH, CMP_LEN * HEAD_DIM, CMP_HIDDEN), (CMP_LEN * HEAD_DIM) ** -0.5),
        'cmp_v_w2': nrm((DEPTH, CMP_HIDDEN, HEAD_DIM), CMP_HIDDEN ** -0.5),
        'ffn_w_gate': nrm((N_DENSE, D, F), D ** -0.5),
        'ffn_w_up': nrm((N_DENSE, D, F), D ** -0.5),
        'ffn_w_down': nrm((N_DENSE, F, D), F ** -0.5),
        'moe_router': nrm((N_MOE, D, E), D ** -0.5),
        'moe_w_gate': nrm((N_MOE, E, D, F), D ** -0.5),
        'moe_w_up': nrm((N_MOE, E, D, F), D ** -0.5),
        'moe_w_down': nrm((N_MOE, E, F, D), F ** -0.5),
    }


def reference(x, c, positions, ada_w, ada_b, norm_mix_g, norm_ffn_g, w_in, w_out, conv_w,
              ssm_lam_re, ssm_lam_im, ssm_b_re, ssm_b_im, ssm_c_re, ssm_c_im, ssm_d, ssm_log_dt, ssm_w_glu,
              nsa_q_norm_g, nsa_k_norm_g, cmp_pos_k, cmp_pos_v, cmp_k_w1, cmp_k_w2, cmp_v_w1, cmp_v_w2,
              ffn_w_gate, ffn_w_up, ffn_w_down, moe_router, moe_w_gate, moe_w_up, moe_w_down):
    h = x
    cond = jax.nn.silu(c)
    for layer in range(DEPTH):
        mod = (cond @ ada_w[layer] + ada_b[layer])[:, None, :]
        shift_mix, scale_mix, gate_mix, shift_ffn, scale_ffn, gate_ffn = jnp.split(mod, 6, axis=-1)
        a = rms_norm(h, norm_mix_g[layer]) * (1.0 + scale_mix) + shift_mix
        mix = hybrid_token_mixing(
            a, positions, w_in[layer], w_out[layer], conv_w[layer],
            ssm_lam_re[layer], ssm_lam_im[layer], ssm_b_re[layer], ssm_b_im[layer],
            ssm_c_re[layer], ssm_c_im[layer], ssm_d[layer], ssm_log_dt[layer], ssm_w_glu[layer],
            nsa_q_norm_g[layer], nsa_k_norm_g[layer], cmp_pos_k[layer], cmp_pos_v[layer],
            cmp_k_w1[layer], cmp_k_w2[layer], cmp_v_w1[layer], cmp_v_w2[layer])
        h = h + gate_mix * mix
        a = rms_norm(h, norm_ffn_g[layer]) * (1.0 + scale_ffn) + shift_ffn
        i = layer // 2
        if layer % 2 == 0:
            f = swiglu(a, ffn_w_gate[i], ffn_w_up[i], ffn_w_down[i])
        else:
            f = moe_swiglu(a, moe_router[i], moe_w_gate[i], moe_w_up[i], moe_w_down[i])
        h = h + gate_ffn * f
    return h
```

```python
import functools
import math

import jax
import jax.numpy as jnp
from jax import lax
from jax.experimental import pallas as pl
from jax.experimental.pallas import tpu as pltpu

F32 = jnp.float32
BF16 = jnp.bfloat16
I32 = jnp.int32

HEAD_DIM = 64
CONV_WIDTH = 3
SSM_GROUP = 16
SSM_STATE = 64
SSM_MAX_RE = -1e-4
NSA_KV_HEADS = 2
CMP_LEN = 32
CMP_STRIDE = 16
SLC_BLOCK = 64
SLC_TOPK = 16
WINDOW = 512
FORCE_BONUS = 1e4
NEG_INF = -1e30
ROPE_THETA = 10000.0
RMS_EPS = 1e-6
N_EXPERTS = 8

LANES = 128
SUBLANES = 8
VMEM_LIMIT = 56 * 1024 * 1024
SEL_MASK_BIAS = -30000.0
SB_SKIP_LOG = -110.0
GATE_PAD = LANES


def _cparams(sem):
    return pltpu.CompilerParams(dimension_semantics=sem, vmem_limit_bytes=VMEM_LIMIT)


def _dot(a, b):
    return jnp.dot(a, b, preferred_element_type=F32)


def _dot_nt(a, b):
    return lax.dot_general(a, b, (((1,), (1,)), ((), ())), preferred_element_type=F32)


def _split(x):
    hi = x.astype(BF16)
    lo = (x - hi.astype(F32)).astype(BF16)
    return hi, lo


def _dot3(a, b):
    ah, al = _split(a)
    bh, bl = _split(b)
    return _dot(ah, bh) + _dot(ah, bl) + _dot(al, bh)


def _dot3_nt(a, b):
    ah, al = _split(a)
    bh, bl = _split(b)
    return _dot_nt(ah, bh) + _dot_nt(ah, bl) + _dot_nt(al, bh)


def _dot2_exact_rhs(a, b_bf16):
    ah, al = _split(a)
    return _dot(ah, b_bf16) + _dot(al, b_bf16)


def _silu(x):
    return x * jax.nn.sigmoid(x)


def _div_pow2(x, n):
    return lax.shift_right_logical(x, jnp.int32(n.bit_length() - 1))


def _mod_pow2(x, n):
    return x & (n - 1)


def _rms_scale(x):
    return lax.rsqrt(jnp.mean(x * x, axis=-1, keepdims=True) + RMS_EPS)


def _mod_kernel(c_ref, w_ref, b_ref, o_ref):
    o_ref[0] = _dot3(_silu(c_ref[...]), w_ref[0]) + b_ref[0]


def _modulation(c, ada_w, ada_b):
    depth, d, n6 = ada_w.shape
    bsz = c.shape[0]
    rows = -(-bsz // SUBLANES) * SUBLANES
    cpad = jnp.zeros((rows, d), F32).at[:bsz].set(c)
    tn = n6 // 4
    out = pl.pallas_call(
        _mod_kernel,
        out_shape=jax.ShapeDtypeStruct((depth, rows, n6), F32),
        grid=(depth, n6 // tn),
        in_specs=[pl.BlockSpec((rows, d), lambda l, j: (0, 0)),
                  pl.BlockSpec((1, d, tn), lambda l, j: (l, 0, j)),
                  pl.BlockSpec((1, 1, tn), lambda l, j: (l, 0, j))],
        out_specs=pl.BlockSpec((1, rows, tn), lambda l, j: (l, 0, j)),
        compiler_params=_cparams(("arbitrary", "arbitrary")),
        name="adaln_mod",
    )(cpad, ada_w, ada_b[:, None, :])
    return out[:, :bsz].reshape(depth, bsz, 6, d)


def _in_kernel(h_ref, g_ref, mod_ref, w_ref, z_ref):
    x = h_ref[0]
    y = x * _rms_scale(x) * g_ref[...]
    a = y * (1.0 + mod_ref[0, 1:2, :]) + mod_ref[0, 0:1, :]
    z_ref[0] = _dot(a.astype(BF16), w_ref[...])


def _in_proj(h, g, mod, w_bf16, tm):
    bsz, t, d = h.shape
    nz = w_bf16.shape[1]
    return pl.pallas_call(
        _in_kernel,
        out_shape=jax.ShapeDtypeStruct((bsz, t, nz), F32),
        grid=(bsz, t // tm),
        in_specs=[pl.BlockSpec((1, tm, d), lambda b, i: (b, i, 0)),
                  pl.BlockSpec((1, d), lambda b, i: (0, 0)),
                  pl.BlockSpec((1, 6, d), lambda b, i: (b, 0, 0)),
                  pl.BlockSpec((d, nz), lambda b, i: (0, 0))],
        out_specs=pl.BlockSpec((1, tm, nz), lambda b, i: (b, i, 0)),
        compiler_params=_cparams(("arbitrary", "arbitrary")),
        name="in_proj",
    )(h, g[None, :], mod, w_bf16)


def _conv_kernel(z_ref, w_ref, o_ref, buf, *, tt, dc):
    @pl.when(pl.program_id(1) == 0)
    def _():
        buf[0:SUBLANES, :] = jnp.zeros((SUBLANES, dc), F32)

    z = z_ref[0]
    gate_b = z[:, :dc]
    v = z[:, dc:2 * dc] * z[:, 2 * dc:]
    buf[SUBLANES:, :] = v
    y = (w_ref[2:3, :] * v
         + w_ref[1:2, :] * buf[SUBLANES - 1:SUBLANES - 1 + tt, :]
         + w_ref[0:1, :] * buf[SUBLANES - 2:SUBLANES - 2 + tt, :])
    o_ref[0] = gate_b * y
    buf[0:SUBLANES, :] = v[tt - SUBLANES:, :]


def _short_conv(z_conv, conv_w, tt):
    bsz, t, c3 = z_conv.shape
    dc = c3 // 3
    return pl.pallas_call(
        functools.partial(_conv_kernel, tt=tt, dc=dc),
        out_shape=jax.ShapeDtypeStruct((bsz, t, dc), F32),
        grid=(bsz, t // tt),
        in_specs=[pl.BlockSpec((1, tt, c3), lambda b, i: (b, i, 0)),
                  pl.BlockSpec((CONV_WIDTH, dc), lambda b, i: (0, 0))],
        out_specs=pl.BlockSpec((1, tt, dc), lambda b, i: (b, i, 0)),
        scratch_shapes=[pltpu.VMEM((tt + SUBLANES, dc), F32)],
        compiler_params=_cparams(("arbitrary", "arbitrary")),
        name="short_conv",
    )(z_conv, conv_w)


SSM_LANE_CHUNK = 2 * LANES


def _gelu_tanh(x):
    return 0.5 * x * (1.0 + jnp.tanh(math.sqrt(2.0 / math.pi) * (x + 0.044715 * (x * x * x))))


def _ssm_kernel(u_ref, wbu_ref, pre_ref, pim_ref, wc_ref, d_ref, wglu_ref, o_ref,
                bre, bim, cre, cim, *, tt, ns):
    @pl.when(pl.program_id(1) == 0)
    def _():
        cre[...] = jnp.zeros_like(cre)
        cim[...] = jnp.zeros_like(cim)

    u = u_ref[0]
    bu = _dot(u.astype(BF16), wbu_ref[...])
    bre[...] = bu[:, :ns]
    bim[...] = bu[:, ns:]
    lw = SSM_LANE_CHUNK
    row = lax.broadcasted_iota(I32, (SUBLANES, lw), 0)
    for lg in range(ns // lw):
        sl = slice(lg * lw, (lg + 1) * lw)
        pr = pre_ref[:, sl]
        pi_ = pim_ref[:, sl]

        def body(i, carry, sl=sl, pr=pr, pi_=pi_):
            c_re, c_im = carry
            r0 = pl.multiple_of(i * SUBLANES, SUBLANES)
            xr = bre[pl.ds(r0, SUBLANES), sl]
            xi = bim[pl.ds(r0, SUBLANES), sl]
            for k in (1, 2, 4):
                ar = pr[k - 1:k]
                ai = pi_[k - 1:k]
                sr = jnp.where(row >= k, pltpu.roll(xr, k, 0), 0.0)
                si = jnp.where(row >= k, pltpu.roll(xi, k, 0), 0.0)
                xr, xi = xr + ar * sr - ai * si, xi + ar * si + ai * sr
            xr, xi = xr + pr * c_re - pi_ * c_im, xi + pr * c_im + pi_ * c_re
            bre[pl.ds(r0, SUBLANES), sl] = xr
            bim[pl.ds(r0, SUBLANES), sl] = xi
            return xr[SUBLANES - 1:], xi[SUBLANES - 1:]

        c_re, c_im = lax.fori_loop(0, tt // SUBLANES, body, (cre[:, sl], cim[:, sl]))
        cre[:, sl] = c_re
        cim[:, sl] = c_im
    hcat = jnp.concatenate([bre[...], bim[...]], axis=1).astype(BF16)
    y = _dot(hcat, wc_ref[...]) + d_ref[...] * u
    g = _gelu_tanh(y)
    o_ref[0] = g * jax.nn.sigmoid(_dot(g.astype(BF16), wglu_ref[...]))


def _s5(u, lam_re, lam_im, b_re, b_im, c_re, c_im, d_skip, log_dt, w_glu, tt):
    bsz, t, ds = u.shape
    g, p = lam_re.shape
    hc = SSM_GROUP
    ns = g * p
    lam = lax.complex(jnp.minimum(lam_re, SSM_MAX_RE), lam_im)
    dt = jnp.exp(log_dt)[:, None]
    lam_bar = jnp.exp(lam * dt)
    b_bar = ((lam_bar - 1.0) / lam)[..., None] * lax.complex(b_re, b_im)
    eye = jnp.eye(g, dtype=F32)
    wbu = jnp.concatenate(
        [jnp.einsum('gpc,gh->gchp', part, eye).reshape(g * hc, ns) for part in (b_bar.real, b_bar.imag)],
        axis=1).astype(BF16)
    wc = jnp.concatenate(
        [jnp.einsum('gcp,gh->gphc', part, eye).reshape(ns, g * hc) for part in (c_re, -c_im)],
        axis=0).astype(BF16)
    pows = jnp.exp(jnp.arange(1, SUBLANES + 1, dtype=F32)[:, None, None] * (lam * dt)[None])
    pows = pows.reshape(SUBLANES, ns)
    return pl.pallas_call(
        functools.partial(_ssm_kernel, tt=tt, ns=ns),
        out_shape=jax.ShapeDtypeStruct((bsz, t, ds), F32),
        grid=(bsz, t // tt),
        in_specs=[pl.BlockSpec((1, tt, ds), lambda b, i: (b, i, 0)),
                  pl.BlockSpec((ds, 2 * ns), lambda b, i: (0, 0)),
                  pl.BlockSpec((SUBLANES, ns), lambda b, i: (0, 0)),
                  pl.BlockSpec((SUBLANES, ns), lambda b, i: (0, 0)),
                  pl.BlockSpec((2 * ns, ds), lambda b, i: (0, 0)),
                  pl.BlockSpec((1, ds), lambda b, i: (0, 0)),
                  pl.BlockSpec((ds, ds), lambda b, i: (0, 0))],
        out_specs=pl.BlockSpec((1, tt, ds), lambda b, i: (b, i, 0)),
        scratch_shapes=[pltpu.VMEM((tt, ns), F32), pltpu.VMEM((tt, ns), F32),
                        pltpu.VMEM((1, ns), F32), pltpu.VMEM((1, ns), F32)],
        compiler_params=_cparams(("arbitrary", "arbitrary")),
        name="s5_scan",
    )(u, wbu, pows.real, pows.imag, wc, d_skip[None, :], w_glu.astype(BF16))


def _sb_kernel(q_ref, k_ref, v_ref, o_ref, acc, csum, *, tq):
    qi = pl.program_id(2)
    q = q_ref[0, 0]
    acc[...] = jnp.zeros_like(acc)
    csum[...] = jnp.zeros_like(csum)
    ti = lax.broadcasted_iota(I32, (tq, tq), 0)
    ji = lax.broadcasted_iota(I32, (tq, tq), 1)
    later = (ti > ji).astype(BF16)

    def cond(state):
        kb, cmax = state
        return jnp.logical_and(kb >= 0, cmax > SB_SKIP_LOG)

    def body(state):
        kb, _ = state
        k0 = pl.multiple_of(kb * tq, tq)
        kblk = k_ref[0, 0, pl.ds(k0, tq), :]
        vblk = v_ref[0, 0, pl.ds(k0, tq), :]
        z = _dot_nt(q, kblk)
        past = (k0 + ji) < (qi * tq + ti)
        log_beta = jnp.minimum(z, 0.0) - jnp.log(1.0 + jnp.exp(-jnp.abs(z)))
        log_keep = jnp.where(past, log_beta - z, 0.0)
        hi, lo = _split(log_keep)
        inner = _dot(hi, later) + _dot(lo, later)
        c = csum[...]
        w = jnp.where(past, jnp.exp(log_beta + inner + c), 0.0)
        acc[...] += _dot(w.astype(BF16), vblk)
        c = c + jnp.sum(log_keep, axis=1, keepdims=True)
        csum[...] = c
        return kb - 1, jnp.max(c)

    lax.while_loop(cond, body, (qi, jnp.float32(0.0)))
    o_ref[0, 0] = acc[...]


def _stick_breaking(q, k, v, tq):
    bsz, nh, t, dh = q.shape
    return pl.pallas_call(
        functools.partial(_sb_kernel, tq=tq),
        out_shape=jax.ShapeDtypeStruct((bsz, nh, t, dh), F32),
        grid=(bsz, nh, t // tq),
        in_specs=[pl.BlockSpec((1, 1, tq, dh), lambda b, h, i: (b, h, i, 0)),
                  pl.BlockSpec((1, 1, t, dh), lambda b, h, i: (b, h, 0, 0)),
                  pl.BlockSpec((1, 1, t, dh), lambda b, h, i: (b, h, 0, 0))],
        out_specs=pl.BlockSpec((1, 1, tq, dh), lambda b, h, i: (b, h, i, 0)),
        scratch_shapes=[pltpu.VMEM((tq, dh), F32), pltpu.VMEM((tq, 1), F32)],
        compiler_params=_cparams(("arbitrary", "arbitrary", "arbitrary")),
        name="stick_breaking",
    )(q, k, v)


def _head_norm(x, gsum_ref, gain):
    ss = _dot2_exact_rhs(x * x, gsum_ref[...]) * (1.0 / HEAD_DIM)
    return x * lax.rsqrt(ss + RMS_EPS) * gain


def _rope_lanes(x, cos, sin):
    w = x.shape[-1]
    lane = lax.broadcasted_iota(I32, x.shape, 1)
    first = _mod_pow2(lane, HEAD_DIM) < (HEAD_DIM // 2)
    partner = jnp.where(first, -pltpu.roll(x, w - HEAD_DIM // 2, 1), pltpu.roll(x, HEAD_DIM // 2, 1))
    return x * cos + partner * sin


def _nsa_prep_kernel(zq_ref, zks_ref, zkw_ref, ang_ref, gq_ref, gk_ref, gsq_ref, gsk_ref,
                     q_ref, kaug_ref, kw_ref, *, tm, ngrp, nblk):
    ang = ang_ref[0]
    cos1 = jnp.cos(ang)
    sin1 = jnp.sin(ang)
    wq = zq_ref.shape[-1]
    cosq = jnp.concatenate([cos1] * (wq // LANES), axis=1)
    sinq = jnp.concatenate([sin1] * (wq // LANES), axis=1)
    q = _rope_lanes(_head_norm(zq_ref[0], gsq_ref, gq_ref[...]), cosq, sinq)
    q_ref[0] = q * (HEAD_DIM ** -0.5)
    ks = _rope_lanes(_head_norm(zks_ref[0], gsk_ref, gk_ref[...]), cos1, sin1)
    kw = _rope_lanes(_head_norm(zkw_ref[0], gsk_ref, gk_ref[...]), cos1, sin1)
    t0 = pl.program_id(1) * tm
    tok = t0 + lax.broadcasted_iota(I32, (tm, nblk), 0)
    blk = lax.broadcasted_iota(I32, (tm, nblk), 1)
    onehot = jnp.where(_div_pow2(tok, SLC_BLOCK) == blk, 1.0, 0.0).astype(BF16)
    for g in range(ngrp):
        sl = slice(g * HEAD_DIM, (g + 1) * HEAD_DIM)
        kaug_ref[0, g] = jnp.concatenate([onehot, ks[:, sl].astype(BF16)], axis=1)
        kw_ref[0, g] = kw[:, sl].astype(BF16)


def _nsa_prep(zq, zks, zkw, ang, gq, gk, tm):
    bsz, t, wq = zq.shape
    ngrp = NSA_KV_HEADS
    nblk = t // SLC_BLOCK
    wk = ngrp * HEAD_DIM
    lane_q = jnp.arange(wq) // HEAD_DIM
    gsq = (lane_q[:, None] == lane_q[None, :]).astype(BF16)
    lane_k = jnp.arange(wk) // HEAD_DIM
    gsk = (lane_k[:, None] == lane_k[None, :]).astype(BF16)
    return pl.pallas_call(
        functools.partial(_nsa_prep_kernel, tm=tm, ngrp=ngrp, nblk=nblk),
        out_shape=(jax.ShapeDtypeStruct((bsz, t, wq), F32),
                   jax.ShapeDtypeStruct((bsz, ngrp, t, HEAD_DIM + nblk), BF16),
                   jax.ShapeDtypeStruct((bsz, ngrp, t, HEAD_DIM), BF16)),
        grid=(bsz, t // tm),
        in_specs=[pl.BlockSpec((1, tm, wq), lambda b, i: (b, i, 0)),
                  pl.BlockSpec((1, tm, wk), lambda b, i: (b, i, 0)),
                  pl.BlockSpec((1, tm, wk), lambda b, i: (b, i, 0)),
                  pl.BlockSpec((1, tm, LANES), lambda b, i: (b, i, 0)),
                  pl.BlockSpec((1, wq), lambda b, i: (0, 0)),
                  pl.BlockSpec((1, wk), lambda b, i: (0, 0)),
                  pl.BlockSpec((wq, wq), lambda b, i: (0, 0)),
                  pl.BlockSpec((wk, wk), lambda b, i: (0, 0))],
        out_specs=(pl.BlockSpec((1, tm, wq), lambda b, i: (b, i, 0)),
                   pl.BlockSpec((1, ngrp, tm, HEAD_DIM + nblk), lambda b, i: (b, 0, i, 0)),
                   pl.BlockSpec((1, ngrp, tm, HEAD_DIM), lambda b, i: (b, 0, i, 0))),
        compiler_params=_cparams(("arbitrary", "arbitrary")),
        name="nsa_prep",
    )(zq, zks, zkw, ang, jnp.tile(gq, wq // HEAD_DIM)[None, :], jnp.tile(gk, ngrp)[None, :], gsq, gsk)


def _compress_kernel(ck_ref, cv_ref, w1k_ref, w2k_ref, w1v_ref, w2v_ref, pek_ref, pev_ref,
                     gk_ref, angc_ref, kc_ref, vc_ref, *, nch, half):
    def mlp(c_ref, w1_ref, w2_ref, pe_ref):
        x = c_ref[0, 0]
        w1 = w1_ref[...]
        first = _dot3(x, w1[:half])
        second = _dot3(x, w1[half:])
        bias = _dot3(pe_ref[...], w1)[0:1]
        hid = _silu(first + pltpu.roll(second, nch - 1, 0) + bias)
        return _dot3(hid, w2_ref[...])

    kc = mlp(ck_ref, w1k_ref, w2k_ref, pek_ref)
    kc = kc * _rms_scale(kc) * gk_ref[...]
    ang = angc_ref[0]
    half_d = HEAD_DIM // 2
    k1 = kc[:, :half_d]
    k2 = kc[:, half_d:]
    cos = jnp.cos(ang[:, :half_d])
    sin = jnp.sin(ang[:, :half_d])
    kc_ref[0, 0] = jnp.concatenate([k1 * cos - k2 * sin, k2 * cos + k1 * sin], axis=1)
    vc_ref[0, 0] = mlp(cv_ref, w1v_ref, w2v_ref, pev_ref)


def _compress(ck, cv, w1k, w2k, w1v, w2v, pos_k, pos_v, gk, angc):
    bsz, ngrp, nch, half = ck.shape
    hid = w1k.shape[1]

    def pe_rows(pe):
        return jnp.zeros((SUBLANES, 2 * half), F32).at[0].set(pe.reshape(-1))

    blk4 = lambda b, g: (b, g, 0, 0)
    full2 = lambda b, g: (0, 0)
    return pl.pallas_call(
        functools.partial(_compress_kernel, nch=nch, half=half),
        out_shape=(jax.ShapeDtypeStruct((bsz, ngrp, nch, HEAD_DIM), F32),
                   jax.ShapeDtypeStruct((bsz, ngrp, nch, HEAD_DIM), F32)),
        grid=(bsz, ngrp),
        in_specs=[pl.BlockSpec((1, 1, nch, half), blk4),
                  pl.BlockSpec((1, 1, nch, half), blk4),
                  pl.BlockSpec((2 * half, hid), full2),
                  pl.BlockSpec((hid, HEAD_DIM), full2),
                  pl.BlockSpec((2 * half, hid), full2),
                  pl.BlockSpec((hid, HEAD_DIM), full2),
                  pl.BlockSpec((SUBLANES, 2 * half), full2),
                  pl.BlockSpec((SUBLANES, 2 * half), full2),
                  pl.BlockSpec((1, HEAD_DIM), full2),
                  pl.BlockSpec((1, nch, HEAD_DIM), lambda b, g: (b, 0, 0))],
        out_specs=(pl.BlockSpec((1, 1, nch, HEAD_DIM), blk4),
                   pl.BlockSpec((1, 1, nch, HEAD_DIM), blk4)),
        compiler_params=_cparams(("arbitrary", "arbitrary")),
        name="nsa_compress",
    )(ck, cv, w1k, w2k, w1v, w2v, pe_rows(pos_k), pe_rows(pos_v), gk[None, :], angc)


def _gate_col(gates, head, branch):
    c = head * 3 + branch
    if isinstance(c, int):
        return gates[:, c:c + 1]
    lane = lax.broadcasted_iota(I32, gates.shape, 1)
    return jnp.sum(jnp.where(lane == c, gates, 0.0), axis=1, keepdims=True)


def _cmp_attn_kernel(q_ref, kc_ref, vc_ref, gl_ref, ov_ref, oc_ref, sel_ref, *, tq, ngrp, hpg, nch, nblk):
    t = pl.program_id(1) * tq + lax.broadcasted_iota(I32, (tq, 1), 0)
    cmp_end = lax.broadcasted_iota(I32, (1, nch), 1) * CMP_STRIDE + (CMP_LEN - 1)
    visible = cmp_end <= t
    q = q_ref[0]
    gates = jax.nn.sigmoid(gl_ref[0])
    blk = lax.broadcasted_iota(I32, (tq, nblk), 1)
    cur = _div_pow2(t, SLC_BLOCK)
    forced = (blk == 0) | (blk == cur) | (blk == cur - 1)
    causal = blk <= cur
    outs = []
    for g in range(ngrp):
        kc = kc_ref[0, g]
        vc = vc_ref[0, g].astype(BF16)
        psum = jnp.zeros((tq, nch), F32)
        for hh in range(hpg):
            head = g * hpg + hh
            qh = q[:, head * HEAD_DIM:(head + 1) * HEAD_DIM]
            s = jnp.where(visible, _dot3_nt(qh, kc), NEG_INF)
            p = jnp.where(visible, jnp.exp(s - jnp.max(s, axis=1, keepdims=True)), 0.0)
            denom = jnp.sum(p, axis=1, keepdims=True)
            p = p / jnp.where(denom == 0.0, 1.0, denom)
            psum = psum + p
            outs.append(_dot(p.astype(BF16), vc) * _gate_col(gates, head, 0))
        imp = _dot2_exact_rhs(psum, ov_ref[...])
        val = jnp.where(causal, imp + jnp.where(forced, FORCE_BONUS, 0.0), NEG_INF)
        for _ in range(min(SLC_TOPK, nblk)):
            m = jnp.max(val, axis=1, keepdims=True)
            first = jnp.min(jnp.where(val == m, blk, nblk), axis=1, keepdims=True)
            val = jnp.where(blk == first, -jnp.inf, val)
        sel_ref[0, g] = jnp.where(val == -jnp.inf, 0.0, SEL_MASK_BIAS).astype(BF16)
    oc_ref[0] = jnp.concatenate(outs, axis=1)


def _cmp_attn(q, kc, vc, glog, tq):
    bsz, t, wq = q.shape
    ngrp, nch = kc.shape[1], kc.shape[2]
    hpg = wq // HEAD_DIM // ngrp
    nblk = t // SLC_BLOCK
    cs = jnp.arange(nch) * CMP_STRIDE
    ss = jnp.arange(nblk) * SLC_BLOCK
    ov = ((cs[:, None] < ss[None, :] + SLC_BLOCK) & (cs[:, None] + CMP_LEN > ss[None, :])).astype(BF16)
    return pl.pallas_call(
        functools.partial(_cmp_attn_kernel, tq=tq, ngrp=ngrp, hpg=hpg, nch=nch, nblk=nblk),
        out_shape=(jax.ShapeDtypeStruct((bsz, t, wq), F32),
                   jax.ShapeDtypeStruct((bsz, ngrp, t, nblk), BF16)),
        grid=(bsz, t // tq),
        in_specs=[pl.BlockSpec((1, tq, wq), lambda b, i: (b, i, 0)),
                  pl.BlockSpec((1, ngrp, nch, HEAD_DIM), lambda b, i: (b, 0, 0, 0)),
                  pl.BlockSpec((1, ngrp, nch, HEAD_DIM), lambda b, i: (b, 0, 0, 0)),
                  pl.BlockSpec((1, tq, GATE_PAD), lambda b, i: (b, i, 0)),
                  pl.BlockSpec((nch, nblk), lambda b, i: (0, 0))],
        out_specs=(pl.BlockSpec((1, tq, wq), lambda b, i: (b, i, 0)),
                   pl.BlockSpec((1, ngrp, tq, nblk), lambda b, i: (b, 0, i, 0))),
        compiler_params=_cparams(("arbitrary", "arbitrary")),
        name="nsa_cmp_select",
    )(q, kc, vc, glog, ov)


def _sel_attn_kernel(q_ref, sel_ref, kaug_ref, v_ref, gl_ref, o_ref, m_s, l_s, acc, *, tq, hpg):
    g = pl.program_id(1)
    qi = pl.program_id(2)
    q = q_ref[0]
    sel = sel_ref[0, 0]
    qa = jnp.concatenate(
        [jnp.concatenate([sel, q[:, hh * HEAD_DIM:(hh + 1) * HEAD_DIM].astype(BF16)], axis=1)
         for hh in range(hpg)], axis=0)
    m_s[...] = jnp.full_like(m_s, NEG_INF)
    l_s[...] = jnp.zeros_like(l_s)
    acc[...] = jnp.zeros_like(acc)
    rows = hpg * tq
    ti = _mod_pow2(lax.broadcasted_iota(I32, (rows, tq), 0), tq)
    ji = lax.broadcasted_iota(I32, (rows, tq), 1)

    def step(kt, diag):
        k0 = pl.multiple_of(kt * tq, tq)
        s = _dot_nt(qa, kaug_ref[0, 0, pl.ds(k0, tq), :])
        if diag:
            s = jnp.where(ji <= ti, s, NEG_INF)
        m_old = m_s[...]
        m_new = jnp.maximum(m_old, jnp.max(s, axis=1, keepdims=True))
        alpha = jnp.exp(m_old - m_new)
        p = jnp.exp(s - m_new)
        l_s[...] = alpha * l_s[...] + jnp.sum(p, axis=1, keepdims=True)
        acc[...] = alpha * acc[...] + _dot(p.astype(BF16), v_ref[0, 0, pl.ds(k0, tq), :])
        m_s[...] = m_new

    def body(kt, carry):
        step(kt, False)
        return carry

    lax.fori_loop(0, qi, body, 0)
    step(qi, True)
    gates = jax.nn.sigmoid(gl_ref[0])
    out = acc[...] / l_s[...]
    o_ref[0] = jnp.concatenate(
        [out[hh * tq:(hh + 1) * tq] * _gate_col(gates, g * hpg + hh, 1) for hh in range(hpg)], axis=1)


def _sel_attn(q, sel, kaug, v, glog, tq):
    bsz, t, wq = q.shape
    ngrp = kaug.shape[1]
    hpg = wq // HEAD_DIM // ngrp
    gw = hpg * HEAD_DIM
    nblk = sel.shape[-1]
    ka = kaug.shape[-1]
    rows = hpg * tq
    return pl.pallas_call(
        functools.partial(_sel_attn_kernel, tq=tq, hpg=hpg),
        out_shape=jax.ShapeDtypeStruct((bsz, t, wq), F32),
        grid=(bsz, ngrp, t // tq),
        in_specs=[pl.BlockSpec((1, tq, gw), lambda b, g, i: (b, i, g)),
                  pl.BlockSpec((1, 1, tq, nblk), lambda b, g, i: (b, g, i, 0)),
                  pl.BlockSpec((1, 1, t, ka), lambda b, g, i: (b, g, 0, 0)),
                  pl.BlockSpec((1, 1, t, HEAD_DIM), lambda b, g, i: (b, g, 0, 0)),
                  pl.BlockSpec((1, tq, GATE_PAD), lambda b, g, i: (b, i, 0))],
        out_specs=pl.BlockSpec((1, tq, gw), lambda b, g, i: (b, i, g)),
        scratch_shapes=[pltpu.VMEM((rows, 1), F32), pltpu.VMEM((rows, 1), F32),
                        pltpu.VMEM((rows, HEAD_DIM), F32)],
        compiler_params=_cparams(("arbitrary", "arbitrary", "arbitrary")),
        name="nsa_selected",
    )(q, sel, kaug, v, glog)


def _win_attn_kernel(q_ref, k_ref, v_ref, gl_ref, o_ref, *, tq, hpg, span):
    g = pl.program_id(1)
    qi = pl.program_id(2)
    q = q_ref[0]
    q2 = jnp.concatenate([q[:, hh * HEAD_DIM:(hh + 1) * HEAD_DIM] for hh in range(hpg)], axis=0).astype(BF16)
    base = pl.multiple_of(jnp.maximum(qi * tq - WINDOW, 0), tq)
    kblk = k_ref[0, 0, pl.ds(base, span), :]
    vblk = v_ref[0, 0, pl.ds(base, span), :]
    rows = hpg * tq
    t = qi * tq + _mod_pow2(lax.broadcasted_iota(I32, (rows, span), 0), tq)
    wpos = base + lax.broadcasted_iota(I32, (rows, span), 1)
    mask = (wpos <= t) & (wpos > t - WINDOW)
    s = jnp.where(mask, _dot_nt(q2, kblk), NEG_INF)
    p = jnp.where(mask, jnp.exp(s - jnp.max(s, axis=1, keepdims=True)), 0.0)
    p = p / jnp.sum(p, axis=1, keepdims=True)
    out = _dot(p.astype(BF16), vblk)
    gates = jax.nn.sigmoid(gl_ref[0])
    o_ref[0] = jnp.concatenate(
        [out[hh * tq:(hh + 1) * tq] * _gate_col(gates, g * hpg + hh, 2) for hh in range(hpg)], axis=1)


def _win_attn(q, kw, vw, glog, tq):
    bsz, t, wq = q.shape
    ngrp = kw.shape[1]
    hpg = wq // HEAD_DIM // ngrp
    gw = hpg * HEAD_DIM
    span = WINDOW + tq
    assert WINDOW % tq == 0 and t >= span
    return pl.pallas_call(
        functools.partial(_win_attn_kernel, tq=tq, hpg=hpg, span=span),
        out_shape=jax.ShapeDtypeStruct((bsz, t, wq), F32),
        grid=(bsz, ngrp, t // tq),
        in_specs=[pl.BlockSpec((1, tq, gw), lambda b, g, i: (b, i, g)),
                  pl.BlockSpec((1, 1, t, HEAD_DIM), lambda b, g, i: (b, g, 0, 0)),
                  pl.BlockSpec((1, 1, t, HEAD_DIM), lambda b, g, i: (b, g, 0, 0)),
                  pl.BlockSpec((1, tq, GATE_PAD), lambda b, g, i: (b, i, 0))],
        out_specs=pl.BlockSpec((1, tq, gw), lambda b, g, i: (b, i, g)),
        compiler_params=_cparams(("arbitrary", "arbitrary", "arbitrary")),
        name="nsa_window",
    )(q, kw, vw, glog)


def _out_kernel(h_ref, yc_ref, ys_ref, yb_ref, oc_ref, os_ref, ow_ref, mod_ref, w_ref, o_ref):
    y = jnp.concatenate([yc_ref[0], ys_ref[0], yb_ref[0], oc_ref[0] + os_ref[0] + ow_ref[0]], axis=1)
    o_ref[0] = h_ref[0] + mod_ref[0, 2:3, :] * _dot(y.astype(BF16), w_ref[...])


def _out_proj(h, parts, mod, w_bf16, tm):
    bsz, t, d = h.shape
    wp = parts[0].shape[-1]
    tok = lambda b, i: (b, i, 0)
    return pl.pallas_call(
        _out_kernel,
        out_shape=jax.ShapeDtypeStruct((bsz, t, d), F32),
        grid=(bsz, t // tm),
        in_specs=[pl.BlockSpec((1, tm, d), tok)] + [pl.BlockSpec((1, tm, wp), tok)] * 6
        + [pl.BlockSpec((1, 6, d), lambda b, i: (b, 0, 0)),
           pl.BlockSpec((d, d), lambda b, i: (0, 0))],
        out_specs=pl.BlockSpec((1, tm, d), tok),
        compiler_params=_cparams(("arbitrary", "arbitrary")),
        name="out_proj",
    )(h, *parts, mod, w_bf16)


def _ffn_kernel(h_ref, g_ref, mod_ref, wg_ref, wu_ref, wd_ref, o_ref, a_s, acc):
    f = pl.program_id(2)

    @pl.when(f == 0)
    def _():
        x = h_ref[0]
        y = x * _rms_scale(x) * g_ref[...]
        a_s[...] = (y * (1.0 + mod_ref[0, 4:5, :]) + mod_ref[0, 3:4, :]).astype(BF16)
        acc[...] = jnp.zeros_like(acc)

    a = a_s[...]
    hid = _silu(_dot(a, wg_ref[...])) * _dot(a, wu_ref[...])
    acc[...] += _dot(hid.astype(BF16), wd_ref[...])

    @pl.when(f == pl.num_programs(2) - 1)
    def _():
        o_ref[0] = h_ref[0] + mod_ref[0, 5:6, :] * acc[...]


def _dense_ffn(h, g, mod, wg, wu, wd, tm, tf):
    bsz, t, d = h.shape
    f = wg.shape[1]
    tok = lambda b, i, j: (b, i, 0)
    return pl.pallas_call(
        _ffn_kernel,
        out_shape=jax.ShapeDtypeStruct((bsz, t, d), F32),
        grid=(bsz, t // tm, f // tf),
        in_specs=[pl.BlockSpec((1, tm, d), tok),
                  pl.BlockSpec((1, d), lambda b, i, j: (0, 0)),
                  pl.BlockSpec((1, 6, d), lambda b, i, j: (b, 0, 0)),
                  pl.BlockSpec((d, tf), lambda b, i, j: (0, j)),
                  pl.BlockSpec((d, tf), lambda b, i, j: (0, j)),
                  pl.BlockSpec((tf, d), lambda b, i, j: (j, 0))],
        out_specs=pl.BlockSpec((1, tm, d), tok),
        scratch_shapes=[pltpu.VMEM((tm, d), BF16), pltpu.VMEM((tm, d), F32)],
        compiler_params=_cparams(("arbitrary", "arbitrary", "arbitrary")),
        name="dense_swiglu",
    )(h, g[None, :], mod, wg, wu, wd)


def _moe_kernel(h_ref, g_ref, mod_ref, wr_ref, wg_ref, wu_ref, wd_ref, o_ref, a_s, gate_s, ge_s, acc):
    e = pl.program_id(2)
    f = pl.program_id(3)
    first = jnp.logical_and(e == 0, f == 0)

    @pl.when(first)
    def _():
        x = h_ref[0]
        y = x * _rms_scale(x) * g_ref[...]
        a = y * (1.0 + mod_ref[0, 4:5, :]) + mod_ref[0, 3:4, :]
        a_s[...] = a.astype(BF16)
        acc[...] = jnp.zeros_like(acc)
        logits = _dot3(a, wr_ref[...])
        lane = lax.broadcasted_iota(I32, logits.shape, 1)
        logits = jnp.where(lane < N_EXPERTS, logits, -jnp.inf)
        v1 = jnp.max(logits, axis=1, keepdims=True)
        i1 = jnp.min(jnp.where(logits == v1, lane, LANES), axis=1, keepdims=True)
        rest = jnp.where(lane == i1, -jnp.inf, logits)
        v2 = jnp.max(rest, axis=1, keepdims=True)
        i2 = jnp.min(jnp.where(rest == v2, lane, LANES), axis=1, keepdims=True)
        e2 = jnp.exp(v2 - v1)
        w1 = 1.0 / (1.0 + e2)
        w2 = e2 / (1.0 + e2)
        gate_s[...] = jnp.where(lane == i1, w1, 0.0) + jnp.where(lane == i2, w2, 0.0)

    @pl.when(f == 0)
    def _():
        gates = gate_s[...]
        lane = lax.broadcasted_iota(I32, gates.shape, 1)
        ge_s[...] = jnp.sum(jnp.where(lane == e, gates, 0.0), axis=1, keepdims=True)

    a = a_s[...]
    hid = _silu(_dot(a, wg_ref[0])) * _dot(a, wu_ref[0]) * ge_s[...]
    acc[...] += _dot(hid.astype(BF16), wd_ref[0])

    last = jnp.logical_and(e == pl.num_programs(2) - 1, f == pl.num_programs(3) - 1)

    @pl.when(last)
    def _():
        o_ref[0] = h_ref[0] + mod_ref[0, 5:6, :] * acc[...]


def _moe_ffn(h, g, mod, router, wg, wu, wd, tm, tf):
    bsz, t, d = h.shape
    ne, _, f = wg.shape
    wr = jnp.zeros((d, LANES), F32).at[:, :ne].set(router)
    tok = lambda b, i, e, j: (b, i, 0)
    return pl.pallas_call(
        _moe_kernel,
        out_shape=jax.ShapeDtypeStruct((bsz, t, d), F32),
        grid=(bsz, t // tm, ne, f // tf),
        in_specs=[pl.BlockSpec((1, tm, d), tok),
                  pl.BlockSpec((1, d), lambda b, i, e, j: (0, 0)),
                  pl.BlockSpec((1, 6, d), lambda b, i, e, j: (b, 0, 0)),
                  pl.BlockSpec((d, LANES), lambda b, i, e, j: (0, 0)),
                  pl.BlockSpec((1, d, tf), lambda b, i, e, j: (e, 0, j)),
                  pl.BlockSpec((1, d, tf), lambda b, i, e, j: (e, 0, j)),
                  pl.BlockSpec((1, tf, d), lambda b, i, e, j: (e, j, 0))],
        out_specs=pl.BlockSpec((1, tm, d), tok),
        scratch_shapes=[pltpu.VMEM((tm, d), BF16), pltpu.VMEM((tm, LANES), F32),
                        pltpu.VMEM((tm, 1), F32), pltpu.VMEM((tm, d), F32)],
        compiler_params=_cparams(("arbitrary", "arbitrary", "arbitrary", "arbitrary")),
        name="moe_swiglu",
    )(h, g[None, :], mod, wr, wg, wu, wd)


def _pack_w_in(w_in, d_mix):
    n_gate = 3 * (d_mix // 4 // HEAD_DIM)
    pad = jnp.zeros((w_in.shape[0], GATE_PAD - n_gate), w_in.dtype)
    return jnp.concatenate([w_in, pad], axis=1).astype(BF16)


def _token_mixing(h, mod, positions, g_mix, w_in, w_out, conv_w, ssm, nsa):
    bsz, t, d = h.shape
    dq = d // 4
    ngrp = NSA_KV_HEADS
    gw = ngrp * HEAD_DIM
    z = _in_proj(h, g_mix, mod, _pack_w_in(w_in, d), tm=512)
    o1 = 3 * dq
    o2 = o1 + dq
    o3 = o2 + 3 * dq
    o4 = o3 + dq
    o5 = o4 + 6 * gw
    y_conv = _short_conv(z[..., :o1], conv_w, tt=512)
    y_ssm = _s5(z[..., o1:o2], *ssm, tt=256)

    nh_sb = dq // HEAD_DIM
    sb = z[..., o2:o3].reshape(bsz, t, 3, nh_sb, HEAD_DIM).transpose(2, 0, 3, 1, 4)
    y_sb = _stick_breaking((sb[0] * HEAD_DIM ** -0.5).astype(BF16), sb[1].astype(BF16), sb[2].astype(BF16),
                           tq=256)
    y_sb = y_sb.transpose(0, 2, 1, 3).reshape(bsz, t, dq)

    o_c, o_s, o_w = _nsa_mixer(z[..., o3:o4], z[..., o4:o5], z[..., o5:], positions, nsa)
    return _out_proj(h, [y_conv, y_ssm, y_sb, o_c, o_s, o_w], mod, w_out.astype(BF16), tm=512)


def _nsa_mixer(zq, kv, glog, positions, nsa):
    bsz, t, _ = zq.shape
    ngrp = NSA_KV_HEADS
    gw = ngrp * HEAD_DIM
    q_norm_g, k_norm_g, pos_k, pos_v, k_w1, k_w2, v_w1, v_w2 = nsa
    half = HEAD_DIM // 2
    inv_freq = jnp.power(jnp.float32(ROPE_THETA), -jnp.arange(half, dtype=F32) / half)
    ang = positions.astype(F32)[..., None] * inv_freq
    ang_tok = jnp.tile(ang, (1, 1, LANES // half))
    q_rot, kaug, kwin = _nsa_prep(zq, kv[..., 2 * gw:3 * gw], kv[..., 4 * gw:5 * gw], ang_tok,
                                  q_norm_g, k_norm_g, tm=512)

    nch = t // CMP_STRIDE

    def chunks(x):
        return x.reshape(bsz, nch, CMP_STRIDE, ngrp, HEAD_DIM).transpose(0, 3, 1, 2, 4).reshape(
            bsz, ngrp, nch, CMP_STRIDE * HEAD_DIM)

    end_idx = jnp.minimum(jnp.arange(nch) * CMP_STRIDE + CMP_LEN - 1, t - 1)
    angc = jnp.tile(positions[:, end_idx].astype(F32)[..., None] * inv_freq, (1, 1, 2))
    kc, vc = _compress(chunks(kv[..., :gw]), chunks(kv[..., gw:2 * gw]), k_w1, k_w2, v_w1, v_w2,
                       pos_k, pos_v, k_norm_g, angc)
    o_c, sel = _cmp_attn(q_rot, kc, vc, glog, tq=256)

    def heads(x):
        return x.reshape(bsz, t, ngrp, HEAD_DIM).transpose(0, 2, 1, 3).astype(BF16)

    o_s = _sel_attn(q_rot, sel, kaug, heads(kv[..., 3 * gw:4 * gw]), glog, tq=256)
    o_w = _win_attn(q_rot, kwin, heads(kv[..., 5 * gw:6 * gw]), glog, tq=256)
    return o_c, o_s, o_w


def kernel(x, c, positions, ada_w, ada_b, norm_mix_g, norm_ffn_g, w_in, w_out, conv_w, ssm_lam_re, ssm_lam_im, ssm_b_re, ssm_b_im, ssm_c_re, ssm_c_im, ssm_d, ssm_log_dt, ssm_w_glu, nsa_q_norm_g, nsa_k_norm_g, cmp_pos_k, cmp_pos_v, cmp_k_w1, cmp_k_w2, cmp_v_w1, cmp_v_w2, ffn_w_gate, ffn_w_up, ffn_w_down, moe_router, moe_w_gate, moe_w_up, moe_w_down):
    depth = ada_w.shape[0]
    mods = _modulation(c, ada_w, ada_b)
    h = x
    for layer in range(depth):
        mod = mods[layer]
        ssm = (ssm_lam_re[layer], ssm_lam_im[layer], ssm_b_re[layer], ssm_b_im[layer], ssm_c_re[layer],
               ssm_c_im[layer], ssm_d[layer], ssm_log_dt[layer], ssm_w_glu[layer])
        nsa = (nsa_q_norm_g[layer], nsa_k_norm_g[layer], cmp_pos_k[layer], cmp_pos_v[layer],
               cmp_k_w1[layer], cmp_k_w2[layer], cmp_v_w1[layer], cmp_v_w2[layer])
        h = _token_mixing(h, mod, positions, norm_mix_g[layer], w_in[layer], w_out[layer], conv_w[layer],
                          ssm, nsa)
        i = layer // 2
        if layer % 2 == 0:
            h = _dense_ffn(h, norm_ffn_g[layer], mod, ffn_w_gate[i].astype(BF16), ffn_w_up[i].astype(BF16),
                           ffn_w_down[i].astype(BF16), tm=512, tf=512)
        else:
            h = _moe_ffn(h, norm_ffn_g[layer], mod, moe_router[i], moe_w_gate[i].astype(BF16),
                         moe_w_up[i].astype(BF16), moe_w_down[i].astype(BF16), tm=512, tf=512)
    return h
```

```python
import functools
import math

import jax
import jax.numpy as jnp
from jax import lax
from jax.experimental import pallas as pl
from jax.experimental.pallas import tpu as pltpu

F32 = jnp.float32
BF16 = jnp.bfloat16
I32 = jnp.int32

HEAD_DIM = 64
CONV_WIDTH = 3
SSM_GROUP = 16
SSM_STATE = 64
SSM_MAX_RE = -1e-4
NSA_KV_HEADS = 2
CMP_LEN = 32
CMP_STRIDE = 16
SLC_BLOCK = 64
SLC_TOPK = 16
WINDOW = 512
FORCE_BONUS = 1e4
NEG_INF = -1e30
ROPE_THETA = 10000.0
RMS_EPS = 1e-6
N_EXPERTS = 8

LANES = 128
SUBLANES = 8
VMEM_LIMIT = 56 * 1024 * 1024
SEL_MASK_BIAS = -30000.0
SB_SKIP_LOG = -110.0
GATE_PAD = LANES


def _cparams(sem):
    return pltpu.CompilerParams(dimension_semantics=sem, vmem_limit_bytes=VMEM_LIMIT)


def _dot(a, b):
    return jnp.dot(a, b, preferred_element_type=F32)


def _dot_nt(a, b):
    return lax.dot_general(a, b, (((1,), (1,)), ((), ())), preferred_element_type=F32)


def _split(x):
    hi = x.astype(BF16)
    lo = (x - hi.astype(F32)).astype(BF16)
    return hi, lo


def _dot3(a, b):
    ah, al = _split(a)
    bh, bl = _split(b)
    return _dot(ah, bh) + _dot(ah, bl) + _dot(al, bh)


def _dot3_nt(a, b):
    ah, al = _split(a)
    bh, bl = _split(b)
    return _dot_nt(ah, bh) + _dot_nt(ah, bl) + _dot_nt(al, bh)


def _dot2_exact_rhs(a, b_bf16):
    ah, al = _split(a)
    return _dot(ah, b_bf16) + _dot(al, b_bf16)


def _silu(x):
    return x * jax.nn.sigmoid(x)


def _div_pow2(x, n):
    return lax.shift_right_logical(x, jnp.int32(n.bit_length() - 1))


def _mod_pow2(x, n):
    return x & (n - 1)


def _rms_scale(x):
    return lax.rsqrt(jnp.mean(x * x, axis=-1, keepdims=True) + RMS_EPS)


def _mod_kernel(c_ref, w_ref, b_ref, o_ref):
    o_ref[0] = _dot3(_silu(c_ref[...]), w_ref[0]) + b_ref[0]


def _modulation(c, ada_w, ada_b):
    depth, d, n6 = ada_w.shape
    bsz = c.shape[0]
    rows = -(-bsz // SUBLANES) * SUBLANES
    cpad = jnp.zeros((rows, d), F32).at[:bsz].set(c)
    tn = n6 // 4
    out = pl.pallas_call(
        _mod_kernel,
        out_shape=jax.ShapeDtypeStruct((depth, rows, n6), F32),
        grid=(depth, n6 // tn),
        in_specs=[pl.BlockSpec((rows, d), lambda l, j: (0, 0)),
                  pl.BlockSpec((1, d, tn), lambda l, j: (l, 0, j)),
                  pl.BlockSpec((1, 1, tn), lambda l, j: (l, 0, j))],
        out_specs=pl.BlockSpec((1, rows, tn), lambda l, j: (l, 0, j)),
        compiler_params=_cparams(("arbitrary", "arbitrary")),
        name="adaln_mod",
    )(cpad, ada_w, ada_b[:, None, :])
    return out[:, :bsz].reshape(depth, bsz, 6, d)


def _in_kernel(h_ref, g_ref, mod_ref, w_ref, z_ref):
    x = h_ref[0]
    y = x * _rms_scale(x) * g_ref[...]
    a = y * (1.0 + mod_ref[0, 1:2, :]) + mod_ref[0, 0:1, :]
    z_ref[0] = _dot(a.astype(BF16), w_ref[...])


def _in_proj(h, g, mod, w_bf16, tm):
    bsz, t, d = h.shape
    nz = w_bf16.shape[1]
    return pl.pallas_call(
        _in_kernel,
        out_shape=jax.ShapeDtypeStruct((bsz, t, nz), F32),
        grid=(bsz, t // tm),
        in_specs=[pl.BlockSpec((1, tm, d), lambda b, i: (b, i, 0)),
                  pl.BlockSpec((1, d), lambda b, i: (0, 0)),
                  pl.BlockSpec((1, 6, d), lambda b, i: (b, 0, 0)),
                  pl.BlockSpec((d, nz), lambda b, i: (0, 0))],
        out_specs=pl.BlockSpec((1, tm, nz), lambda b, i: (b, i, 0)),
        compiler_params=_cparams(("arbitrary", "arbitrary")),
        name="in_proj",
    )(h, g[None, :], mod, w_bf16)


def _conv_kernel(z_ref, w_ref, o_ref, buf, *, tt, dc):
    @pl.when(pl.program_id(1) == 0)
    def _():
        buf[0:SUBLANES, :] = jnp.zeros((SUBLANES, dc), F32)

    z = z_ref[0]
    gate_b = z[:, :dc]
    v = z[:, dc:2 * dc] * z[:, 2 * dc:]
    buf[SUBLANES:, :] = v
    y = (w_ref[2:3, :] * v
         + w_ref[1:2, :] * buf[SUBLANES - 1:SUBLANES - 1 + tt, :]
         + w_ref[0:1, :] * buf[SUBLANES - 2:SUBLANES - 2 + tt, :])
    o_ref[0] = gate_b * y
    buf[0:SUBLANES, :] = v[tt - SUBLANES:, :]


def _short_conv(z_conv, conv_w, tt):
    bsz, t, c3 = z_conv.shape
    dc = c3 // 3
    return pl.pallas_call(
        functools.partial(_conv_kernel, tt=tt, dc=dc),
        out_shape=jax.ShapeDtypeStruct((bsz, t, dc), F32),
        grid=(bsz, t // tt),
        in_specs=[pl.BlockSpec((1, tt, c3), lambda b, i: (b, i, 0)),
                  pl.BlockSpec((CONV_WIDTH, dc), lambda b, i: (0, 0))],
        out_specs=pl.BlockSpec((1, tt, dc), lambda b, i: (b, i, 0)),
        scratch_shapes=[pltpu.VMEM((tt + SUBLANES, dc), F32)],
        compiler_params=_cparams(("arbitrary", "arbitrary")),
        name="short_conv",
    )(z_conv, conv_w)


SSM_LANE_CHUNK = 2 * LANES


def _gelu_tanh(x):
    return 0.5 * x * (1.0 + jnp.tanh(math.sqrt(2.0 / math.pi) * (x + 0.044715 * (x * x * x))))


def _ssm_kernel(u_ref, wbu_ref, pre_ref, pim_ref, wc_ref, d_ref, wglu_ref, o_ref,
                bre, bim, cre, cim, *, tt, ns):
    @pl.when(pl.program_id(1) == 0)
    def _():
        cre[...] = jnp.zeros_like(cre)
        cim[...] = jnp.zeros_like(cim)

    u = u_ref[0]
    bu = _dot(u.astype(BF16), wbu_ref[...])
    bre[...] = bu[:, :ns]
    bim[...] = bu[:, ns:]
    lw = SSM_LANE_CHUNK
    row = lax.broadcasted_iota(I32, (SUBLANES, lw), 0)
    for lg in range(ns // lw):
        sl = slice(lg * lw, (lg + 1) * lw)
        pr = pre_ref[:, sl]
        pi_ = pim_ref[:, sl]

        def body(i, carry, sl=sl, pr=pr, pi_=pi_):
            c_re, c_im = carry
            r0 = pl.multiple_of(i * SUBLANES, SUBLANES)
            xr = bre[pl.ds(r0, SUBLANES), sl]
            xi = bim[pl.ds(r0, SUBLANES), sl]
            for k in (1, 2, 4):
                ar = pr[k - 1:k]
                ai = pi_[k - 1:k]
                sr = jnp.where(row >= k, pltpu.roll(xr, k, 0), 0.0)
                si = jnp.where(row >= k, pltpu.roll(xi, k, 0), 0.0)
                xr, xi = xr + ar * sr - ai * si, xi + ar * si + ai * sr
            xr, xi = xr + pr * c_re - pi_ * c_im, xi + pr * c_im + pi_ * c_re
            bre[pl.ds(r0, SUBLANES), sl] = xr
            bim[pl.ds(r0, SUBLANES), sl] = xi
            return xr[SUBLANES - 1:], xi[SUBLANES - 1:]

        c_re, c_im = lax.fori_loop(0, tt // SUBLANES, body, (cre[:, sl], cim[:, sl]))
        cre[:, sl] = c_re
        cim[:, sl] = c_im
    hcat = jnp.concatenate([bre[...], bim[...]], axis=1).astype(BF16)
    y = _dot(hcat, wc_ref[...]) + d_ref[...] * u
    g = _gelu_tanh(y)
    o_ref[0] = g * jax.nn.sigmoid(_dot(g.astype(BF16), wglu_ref[...]))


def _s5(u, lam_re, lam_im, b_re, b_im, c_re, c_im, d_skip, log_dt, w_glu, tt):
    bsz, t, ds = u.shape
    g, p = lam_re.shape
    hc = SSM_GROUP
    ns = g * p
    lam = lax.complex(jnp.minimum(lam_re, SSM_MAX_RE), lam_im)
    dt = jnp.exp(log_dt)[:, None]
    lam_bar = jnp.exp(lam * dt)
    b_bar = ((lam_bar - 1.0) / lam)[..., None] * lax.complex(b_re, b_im)
    eye = jnp.eye(g, dtype=F32)
    wbu = jnp.concatenate(
        [jnp.einsum('gpc,gh->gchp', part, eye).reshape(g * hc, ns) for part in (b_bar.real, b_bar.imag)],
        axis=1).astype(BF16)
    wc = jnp.concatenate(
        [jnp.einsum('gcp,gh->gphc', part, eye).reshape(ns, g * hc) for part in (c_re, -c_im)],
        axis=0).astype(BF16)
    pows = jnp.exp(jnp.arange(1, SUBLANES + 1, dtype=F32)[:, None, None] * (lam * dt)[None])
    pows = pows.reshape(SUBLANES, ns)
    return pl.pallas_call(
        functools.partial(_ssm_kernel, tt=tt, ns=ns),
        out_shape=jax.ShapeDtypeStruct((bsz, t, ds), F32),
        grid=(bsz, t // tt),
        in_specs=[pl.BlockSpec((1, tt, ds), lambda b, i: (b, i, 0)),
                  pl.BlockSpec((ds, 2 * ns), lambda b, i: (0, 0)),
                  pl.BlockSpec((SUBLANES, ns), lambda b, i: (0, 0)),
                  pl.BlockSpec((SUBLANES, ns), lambda b, i: (0, 0)),
                  pl.BlockSpec((2 * ns, ds), lambda b, i: (0, 0)),
                  pl.BlockSpec((1, ds), lambda b, i: (0, 0)),
                  pl.BlockSpec((ds, ds), lambda b, i: (0, 0))],
        out_specs=pl.BlockSpec((1, tt, ds), lambda b, i: (b, i, 0)),
        scratch_shapes=[pltpu.VMEM((tt, ns), F32), pltpu.VMEM((tt, ns), F32),
                        pltpu.VMEM((1, ns), F32), pltpu.VMEM((1, ns), F32)],
        compiler_params=_cparams(("arbitrary", "arbitrary")),
        name="s5_scan",
    )(u, wbu, pows.real, pows.imag, wc, d_skip[None, :], w_glu.astype(BF16))


def _sb_kernel(q_ref, k_ref, v_ref, o_ref, acc, csum, *, tq):
    qi = pl.program_id(2)
    q = q_ref[0, 0]
    acc[...] = jnp.zeros_like(acc)
    csum[...] = jnp.zeros_like(csum)
    ti = lax.broadcasted_iota(I32, (tq, tq), 0)
    ji = lax.broadcasted_iota(I32, (tq, tq), 1)
    later = (ti > ji).astype(BF16)

    def cond(state):
        kb, cmax = state
        return jnp.logical_and(kb >= 0, cmax > SB_SKIP_LOG)

    def body(state):
        kb, _ = state
        k0 = pl.multiple_of(kb * tq, tq)
        kblk = k_ref[0, 0, pl.ds(k0, tq), :]
        vblk = v_ref[0, 0, pl.ds(k0, tq), :]
        z = _dot_nt(q, kblk)
        past = (k0 + ji) < (qi * tq + ti)
        log_beta = jnp.minimum(z, 0.0) - jnp.log(1.0 + jnp.exp(-jnp.abs(z)))
        log_keep = jnp.where(past, log_beta - z, 0.0)
        hi, lo = _split(log_keep)
        inner = _dot(hi, later) + _dot(lo, later)
        c = csum[...]
        w = jnp.where(past, jnp.exp(log_beta + inner + c), 0.0)
        acc[...] += _dot(w.astype(BF16), vblk)
        c = c + jnp.sum(log_keep, axis=1, keepdims=True)
        csum[...] = c
        return kb - 1, jnp.max(c)

    lax.while_loop(cond, body, (qi, jnp.float32(0.0)))
    o_ref[0, 0] = acc[...]


def _stick_breaking(q, k, v, tq):
    bsz, nh, t, dh = q.shape
    return pl.pallas_call(
        functools.partial(_sb_kernel, tq=tq),
        out_shape=jax.ShapeDtypeStruct((bsz, nh, t, dh), F32),
        grid=(bsz, nh, t // tq),
        in_specs=[pl.BlockSpec((1, 1, tq, dh), lambda b, h, i: (b, h, i, 0)),
                  pl.BlockSpec((1, 1, t, dh), lambda b, h, i: (b, h, 0, 0)),
                  pl.BlockSpec((1, 1, t, dh), lambda b, h, i: (b, h, 0, 0))],
        out_specs=pl.BlockSpec((1, 1, tq, dh), lambda b, h, i: (b, h, i, 0)),
        scratch_shapes=[pltpu.VMEM((tq, dh), F32), pltpu.VMEM((tq, 1), F32)],
        compiler_params=_cparams(("arbitrary", "arbitrary", "arbitrary")),
        name="stick_breaking",
    )(q, k, v)


def _head_norm(x, gsum_ref, gain):
    ss = _dot2_exact_rhs(x * x, gsum_ref[...]) * (1.0 / HEAD_DIM)
    return x * lax.rsqrt(ss + RMS_EPS) * gain


def _rope_lanes(x, cos, sin):
    w = x.shape[-1]
    lane = lax.broadcasted_iota(I32, x.shape, 1)
    first = _mod_pow2(lane, HEAD_DIM) < (HEAD_DIM // 2)
    partner = jnp.where(first, -pltpu.roll(x, w - HEAD_DIM // 2, 1), pltpu.roll(x, HEAD_DIM // 2, 1))
    return x * cos + partner * sin


def _nsa_prep_kernel(zq_ref, zks_ref, zkw_ref, ang_ref, gq_ref, gk_ref, gsq_ref, gsk_ref,
                     q_ref, kaug_ref, kw_ref, *, tm, ngrp, nblk):
    ang = ang_ref[0]
    cos1 = jnp.cos(ang)
    sin1 = jnp.sin(ang)
    wq = zq_ref.shape[-1]
    cosq = jnp.concatenate([cos1] * (wq // LANES), axis=1)
    sinq = jnp.concatenate([sin1] * (wq // LANES), axis=1)
    q = _rope_lanes(_head_norm(zq_ref[0], gsq_ref, gq_ref[...]), cosq, sinq)
    q_ref[0] = q * (HEAD_DIM ** -0.5)
    ks = _rope_lanes(_head_norm(zks_ref[0], gsk_ref, gk_ref[...]), cos1, sin1)
    kw = _rope_lanes(_head_norm(zkw_ref[0], gsk_ref, gk_ref[...]), cos1, sin1)
    t0 = pl.program_id(1) * tm
    tok = t0 + lax.broadcasted_iota(I32, (tm, nblk), 0)
    blk = lax.broadcasted_iota(I32, (tm, nblk), 1)
    onehot = jnp.where(_div_pow2(tok, SLC_BLOCK) == blk, 1.0, 0.0).astype(BF16)
    for g in range(ngrp):
        sl = slice(g * HEAD_DIM, (g + 1) * HEAD_DIM)
        kaug_ref[0, g] = jnp.concatenate([onehot, ks[:, sl].astype(BF16)], axis=1)
        kw_ref[0, g] = kw[:, sl].astype(BF16)


def _nsa_prep(zq, zks, zkw, ang, gq, gk, tm):
    bsz, t, wq = zq.shape
    ngrp = NSA_KV_HEADS
    nblk = t // SLC_BLOCK
    wk = ngrp * HEAD_DIM
    lane_q = jnp.arange(wq) // HEAD_DIM
    gsq = (lane_q[:, None] == lane_q[None, :]).astype(BF16)
    lane_k = jnp.arange(wk) // HEAD_DIM
    gsk = (lane_k[:, None] == lane_k[None, :]).astype(BF16)
    return pl.pallas_call(
        functools.partial(_nsa_prep_kernel, tm=tm, ngrp=ngrp, nblk=nblk),
        out_shape=(jax.ShapeDtypeStruct((bsz, t, wq), F32),
                   jax.ShapeDtypeStruct((bsz, ngrp, t, HEAD_DIM + nblk), BF16),
                   jax.ShapeDtypeStruct((bsz, ngrp, t, HEAD_DIM), BF16)),
        grid=(bsz, t // tm),
        in_specs=[pl.BlockSpec((1, tm, wq), lambda b, i: (b, i, 0)),
                  pl.BlockSpec((1, tm, wk), lambda b, i: (b, i, 0)),
                  pl.BlockSpec((1, tm, wk), lambda b, i: (b, i, 0)),
                  pl.BlockSpec((1, tm, LANES), lambda b, i: (b, i, 0)),
                  pl.BlockSpec((1, wq), lambda b, i: (0, 0)),
                  pl.BlockSpec((1, wk), lambda b, i: (0, 0)),
                  pl.BlockSpec((wq, wq), lambda b, i: (0, 0)),
                  pl.BlockSpec((wk, wk), lambda b, i: (0, 0))],
        out_specs=(pl.BlockSpec((1, tm, wq), lambda b, i: (b, i, 0)),
                   pl.BlockSpec((1, ngrp, tm, HEAD_DIM + nblk), lambda b, i: (b, 0, i, 0)),
                   pl.BlockSpec((1, ngrp, tm, HEAD_DIM), lambda b, i: (b, 0, i, 0))),
        compiler_params=_cparams(("arbitrary", "arbitrary")),
        name="nsa_prep",
    )(zq, zks, zkw, ang, jnp.tile(gq, wq // HEAD_DIM)[None, :], jnp.tile(gk, ngrp)[None, :], gsq, gsk)


def _compress_kernel(ck_ref, cv_ref, w1k_ref, w2k_ref, w1v_ref, w2v_ref, pek_ref, pev_ref,
                     gk_ref, angc_ref, kc_ref, vc_ref, *, nch, half):
    def mlp(c_ref, w1_ref, w2_ref, pe_ref):
        x = c_ref[0, 0]
        w1 = w1_ref[...]
        first = _dot3(x, w1[:half])
        second = _dot3(x, w1[half:])
        bias = _dot3(pe_ref[...], w1)[0:1]
        hid = _silu(first + pltpu.roll(second, nch - 1, 0) + bias)
        return _dot3(hid, w2_ref[...])

    kc = mlp(ck_ref, w1k_ref, w2k_ref, pek_ref)
    kc = kc * _rms_scale(kc) * gk_ref[...]
    ang = angc_ref[0]
    half_d = HEAD_DIM // 2
    k1 = kc[:, :half_d]
    k2 = kc[:, half_d:]
    cos = jnp.cos(ang[:, :half_d])
    sin = jnp.sin(ang[:, :half_d])
    kc_ref[0, 0] = jnp.concatenate([k1 * cos - k2 * sin, k2 * cos + k1 * sin], axis=1)
    vc_ref[0, 0] = mlp(cv_ref, w1v_ref, w2v_ref, pev_ref)


def _compress(ck, cv, w1k, w2k, w1v, w2v, pos_k, pos_v, gk, angc):
    bsz, ngrp, nch, half = ck.shape
    hid = w1k.shape[1]

    def pe_rows(pe):
        return jnp.zeros((SUBLANES, 2 * half), F32).at[0].set(pe.reshape(-1))

    blk4 = lambda b, g: (b, g, 0, 0)
    full2 = lambda b, g: (0, 0)
    return pl.pallas_call(
        functools.partial(_compress_kernel, nch=nch, half=half),
        out_shape=(jax.ShapeDtypeStruct((bsz, ngrp, nch, HEAD_DIM), F32),
                   jax.ShapeDtypeStruct((bsz, ngrp, nch, HEAD_DIM), F32)),
        grid=(bsz, ngrp),
        in_specs=[pl.BlockSpec((1, 1, nch, half), blk4),
                  pl.BlockSpec((1, 1, nch, half), blk4),
                  pl.BlockSpec((2 * half, hid), full2),
                  pl.BlockSpec((hid, HEAD_DIM), full2),
                  pl.BlockSpec((2 * half, hid), full2),
                  pl.BlockSpec((hid, HEAD_DIM), full2),
                  pl.BlockSpec((SUBLANES, 2 * half), full2),
                  pl.BlockSpec((SUBLANES, 2 * half), full2),
                  pl.BlockSpec((1, HEAD_DIM), full2),
                  pl.BlockSpec((1, nch, HEAD_DIM), lambda b, g: (b, 0, 0))],
        out_specs=(pl.BlockSpec((1, 1, nch, HEAD_DIM), blk4),
                   pl.BlockSpec((1, 1, nch, HEAD_DIM), blk4)),
        compiler_params=_cparams(("arbitrary", "arbitrary")),
        name="nsa_compress",
    )(ck, cv, w1k, w2k, w1v, w2v, pe_rows(pos_k), pe_rows(pos_v), gk[None, :], angc)


def _gate_col(gates, head, branch):
    c = head * 3 + branch
    if isinstance(c, int):
        return gates[:, c:c + 1]
    lane = lax.broadcasted_iota(I32, gates.shape, 1)
    return jnp.sum(jnp.where(lane == c, gates, 0.0), axis=1, keepdims=True)


def _cmp_attn_kernel(q_ref, kc_ref, vc_ref, gl_ref, ov_ref, oc_ref, sel_ref, *, tq, ngrp, hpg, nch, nblk):
    t = pl.program_id(1) * tq + lax.broadcasted_iota(I32, (tq, 1), 0)
    cmp_end = lax.broadcasted_iota(I32, (1, nch), 1) * CMP_STRIDE + (CMP_LEN - 1)
    visible = cmp_end <= t
    q = q_ref[0]
    gates = jax.nn.sigmoid(gl_ref[0])
    blk = lax.broadcasted_iota(I32, (tq, nblk), 1)
    cur = _div_pow2(t, SLC_BLOCK)
    forced = (blk == 0) | (blk == cur) | (blk == cur - 1)
    causal = blk <= cur
    outs = []
    for g in range(ngrp):
        kc = kc_ref[0, g]
        vc = vc_ref[0, g].astype(BF16)
        psum = jnp.zeros((tq, nch), F32)
        for hh in range(hpg):
            head = g * hpg + hh
            qh = q[:, head * HEAD_DIM:(head + 1) * HEAD_DIM]
            s = jnp.where(visible, _dot3_nt(qh, kc), NEG_INF)
            p = jnp.where(visible, jnp.exp(s - jnp.max(s, axis=1, keepdims=True)), 0.0)
            denom = jnp.sum(p, axis=1, keepdims=True)
            p = p / jnp.where(denom == 0.0, 1.0, denom)
            psum = psum + p
            outs.append(_dot(p.astype(BF16), vc) * _gate_col(gates, head, 0))
        imp = _dot2_exact_rhs(psum, ov_ref[...])
        val = jnp.where(causal, imp + jnp.where(forced, FORCE_BONUS, 0.0), NEG_INF)
        for _ in range(min(SLC_TOPK, nblk)):
            m = jnp.max(val, axis=1, keepdims=True)
            first = jnp.min(jnp.where(val == m, blk, nblk), axis=1, keepdims=True)
            val = jnp.where(blk == first, -jnp.inf, val)
        sel_ref[0, g] = jnp.where(val == -jnp.inf, 0.0, SEL_MASK_BIAS).astype(BF16)
    oc_ref[0] = jnp.concatenate(outs, axis=1)


def _cmp_attn(q, kc, vc, glog, tq):
    bsz, t, wq = q.shape
    ngrp, nch = kc.shape[1], kc.shape[2]
    hpg = wq // HEAD_DIM // ngrp
    nblk = t // SLC_BLOCK
    cs = jnp.arange(nch) * CMP_STRIDE
    ss = jnp.arange(nblk) * SLC_BLOCK
    ov = ((cs[:, None] < ss[None, :] + SLC_BLOCK) & (cs[:, None] + CMP_LEN > ss[None, :])).astype(BF16)
    return pl.pallas_call(
        functools.partial(_cmp_attn_kernel, tq=tq, ngrp=ngrp, hpg=hpg, nch=nch, nblk=nblk),
        out_shape=(jax.ShapeDtypeStruct((bsz, t, wq), F32),
                   jax.ShapeDtypeStruct((bsz, ngrp, t, nblk), BF16)),
        grid=(bsz, t // tq),
        in_specs=[pl.BlockSpec((1, tq, wq), lambda b, i: (b, i, 0)),
                  pl.BlockSpec((1, ngrp, nch, HEAD_DIM), lambda b, i: (b, 0, 0, 0)),
                  pl.BlockSpec((1, ngrp, nch, HEAD_DIM), lambda b, i: (b, 0, 0, 0)),
                  pl.BlockSpec((1, tq, GATE_PAD), lambda b, i: (b, i, 0)),
                  pl.BlockSpec((nch, nblk), lambda b, i: (0, 0))],
        out_specs=(pl.BlockSpec((1, tq, wq), lambda b, i: (b, i, 0)),
                   pl.BlockSpec((1, ngrp, tq, nblk), lambda b, i: (b, 0, i, 0))),
        compiler_params=_cparams(("arbitrary", "arbitrary")),
        name="nsa_cmp_select",
    )(q, kc, vc, glog, ov)


SEL_ROW_CHUNK = 128


def _sel_attn_kernel(q_ref, sel_ref, kaug_ref, vaug_ref, gl_ref, o_ref, qa_s, m_s, acc, s_a, s_b, *, tq, tk, hpg):
    g = pl.program_id(1)
    qi = pl.program_id(2)
    q = q_ref[0]
    sel = sel_ref[0, 0]
    for hh in range(hpg):
        qa_s[hh * tq:(hh + 1) * tq, :] = jnp.concatenate(
            [sel, q[:, hh * HEAD_DIM:(hh + 1) * HEAD_DIM].astype(BF16)], axis=1)
    m_s[...] = jnp.full_like(m_s, NEG_INF)
    acc[...] = jnp.zeros_like(acc)
    rows = hpg * tq
    rc = SEL_ROW_CHUNK

    def scores(kt, buf):
        k0 = pl.multiple_of(kt * tk, tk)
        kblk = kaug_ref[0, 0, pl.ds(k0, tk), :]
        for c in range(rows // rc):
            rs = slice(c * rc, (c + 1) * rc)
            buf[rs, :] = _dot_nt(qa_s[rs, :], kblk)

    def consume(kt, buf, diag):
        k0 = pl.multiple_of(kt * tk, tk)
        vblk = vaug_ref[0, 0, pl.ds(k0, tk), :]
        for c in range(rows // rc):
            rs = slice(c * rc, (c + 1) * rc)
            s = buf[rs, :]
            if diag:
                t = qi * tq + _mod_pow2(c * rc + lax.broadcasted_iota(I32, (rc, tk), 0), tq)
                s = jnp.where(k0 + lax.broadcasted_iota(I32, (rc, tk), 1) <= t, s, NEG_INF)
            m_old = m_s[rs, :]
            m_new = jnp.maximum(m_old, jnp.max(s, axis=1, keepdims=True))
            alpha = jnp.exp(m_old - m_new)
            p = jnp.exp(s - m_new[:, :1])
            acc[rs, :] = alpha * acc[rs, :] + _dot(p.astype(BF16), vblk)
            m_s[rs, :] = m_new

    last = (qi * tq) // tk
    scores(0, s_a)

    def pair(j, carry):
        scores(2 * j + 1, s_b)
        consume(2 * j, s_a, False)
        scores(2 * j + 2, s_a)
        consume(2 * j + 1, s_b, False)
        return carry

    lax.fori_loop(0, last // 2, pair, 0)
    t0 = (last // 2) * 2

    @pl.when(t0 < last)
    def _():
        scores(last, s_b)
        consume(t0, s_a, False)
        consume(last, s_b, True)

    @pl.when(t0 == last)
    def _():
        consume(last, s_a, True)

    gates = jax.nn.sigmoid(gl_ref[0])
    a = acc[...]
    out = a[:, :HEAD_DIM] / a[:, HEAD_DIM:HEAD_DIM + 1]
    o_ref[0] = jnp.concatenate(
        [out[hh * tq:(hh + 1) * tq] * _gate_col(gates, g * hpg + hh, 1) for hh in range(hpg)], axis=1)


def _sel_attn(q, sel, kaug, v, glog, tq, tk):
    bsz, t, wq = q.shape
    ngrp = kaug.shape[1]
    hpg = wq // HEAD_DIM // ngrp
    gw = hpg * HEAD_DIM
    nblk = sel.shape[-1]
    ka = kaug.shape[-1]
    rows = hpg * tq
    assert tk % tq == 0 and rows % SEL_ROW_CHUNK == 0
    ones = jnp.ones(v.shape[:-1] + (1,), BF16)
    vaug = jnp.concatenate([v, ones, jnp.zeros(v.shape[:-1] + (LANES - HEAD_DIM - 1,), BF16)], axis=-1)
    return pl.pallas_call(
        functools.partial(_sel_attn_kernel, tq=tq, tk=tk, hpg=hpg),
        out_shape=jax.ShapeDtypeStruct((bsz, t, wq), F32),
        grid=(bsz, ngrp, t // tq),
        in_specs=[pl.BlockSpec((1, tq, gw), lambda b, g, i: (b, i, g)),
                  pl.BlockSpec((1, 1, tq, nblk), lambda b, g, i: (b, g, i, 0)),
                  pl.BlockSpec((1, 1, t, ka), lambda b, g, i: (b, g, 0, 0)),
                  pl.BlockSpec((1, 1, t, LANES), lambda b, g, i: (b, g, 0, 0)),
                  pl.BlockSpec((1, tq, GATE_PAD), lambda b, g, i: (b, i, 0))],
        out_specs=pl.BlockSpec((1, tq, gw), lambda b, g, i: (b, i, g)),
        scratch_shapes=[pltpu.VMEM((rows, ka), BF16), pltpu.VMEM((rows, LANES), F32),
                        pltpu.VMEM((rows, LANES), F32),
                        pltpu.VMEM((rows, tk), F32), pltpu.VMEM((rows, tk), F32)],
        compiler_params=_cparams(("arbitrary", "arbitrary", "arbitrary")),
        name="nsa_selected",
    )(q, sel, kaug, vaug, glog)


def _win_attn_kernel(q_ref, k_ref, v_ref, gl_ref, o_ref, *, tq, hpg, span):
    g = pl.program_id(1)
    qi = pl.program_id(2)
    q = q_ref[0]
    q2 = jnp.concatenate([q[:, hh * HEAD_DIM:(hh + 1) * HEAD_DIM] for hh in range(hpg)], axis=0).astype(BF16)
    base = pl.multiple_of(jnp.maximum(qi * tq - WINDOW, 0), tq)
    kblk = k_ref[0, 0, pl.ds(base, span), :]
    vblk = v_ref[0, 0, pl.ds(base, span), :]
    rows = hpg * tq
    t = qi * tq + _mod_pow2(lax.broadcasted_iota(I32, (rows, span), 0), tq)
    wpos = base + lax.broadcasted_iota(I32, (rows, span), 1)
    mask = (wpos <= t) & (wpos > t - WINDOW)
    s = jnp.where(mask, _dot_nt(q2, kblk), NEG_INF)
    p = jnp.where(mask, jnp.exp(s - jnp.max(s, axis=1, keepdims=True)), 0.0)
    p = p / jnp.sum(p, axis=1, keepdims=True)
    out = _dot(p.astype(BF16), vblk)
    gates = jax.nn.sigmoid(gl_ref[0])
    o_ref[0] = jnp.concatenate(
        [out[hh * tq:(hh + 1) * tq] * _gate_col(gates, g * hpg + hh, 2) for hh in range(hpg)], axis=1)


def _win_attn(q, kw, vw, glog, tq):
    bsz, t, wq = q.shape
    ngrp = kw.shape[1]
    hpg = wq // HEAD_DIM // ngrp
    gw = hpg * HEAD_DIM
    span = WINDOW + tq
    assert WINDOW % tq == 0 and t >= span
    return pl.pallas_call(
        functools.partial(_win_attn_kernel, tq=tq, hpg=hpg, span=span),
        out_shape=jax.ShapeDtypeStruct((bsz, t, wq), F32),
        grid=(bsz, ngrp, t // tq),
        in_specs=[pl.BlockSpec((1, tq, gw), lambda b, g, i: (b, i, g)),
                  pl.BlockSpec((1, 1, t, HEAD_DIM), lambda b, g, i: (b, g, 0, 0)),
                  pl.BlockSpec((1, 1, t, HEAD_DIM), lambda b, g, i: (b, g, 0, 0)),
                  pl.BlockSpec((1, tq, GATE_PAD), lambda b, g, i: (b, i, 0))],
        out_specs=pl.BlockSpec((1, tq, gw), lambda b, g, i: (b, i, g)),
        compiler_params=_cparams(("arbitrary", "arbitrary", "arbitrary")),
        name="nsa_window",
    )(q, kw, vw, glog)


def _out_kernel(h_ref, yc_ref, ys_ref, yb_ref, oc_ref, os_ref, ow_ref, mod_ref, w_ref, o_ref):
    y = jnp.concatenate([yc_ref[0], ys_ref[0], yb_ref[0], oc_ref[0] + os_ref[0] + ow_ref[0]], axis=1)
    o_ref[0] = h_ref[0] + mod_ref[0, 2:3, :] * _dot(y.astype(BF16), w_ref[...])


def _out_proj(h, parts, mod, w_bf16, tm):
    bsz, t, d = h.shape
    wp = parts[0].shape[-1]
    tok = lambda b, i: (b, i, 0)
    return pl.pallas_call(
        _out_kernel,
        out_shape=jax.ShapeDtypeStruct((bsz, t, d), F32),
        grid=(bsz, t // tm),
        in_specs=[pl.BlockSpec((1, tm, d), tok)] + [pl.BlockSpec((1, tm, wp), tok)] * 6
        + [pl.BlockSpec((1, 6, d), lambda b, i: (b, 0, 0)),
           pl.BlockSpec((d, d), lambda b, i: (0, 0))],
        out_specs=pl.BlockSpec((1, tm, d), tok),
        compiler_params=_cparams(("arbitrary", "arbitrary")),
        name="out_proj",
    )(h, *parts, mod, w_bf16)


def _ffn_kernel(h_ref, g_ref, mod_ref, wg_ref, wu_ref, wd_ref, o_ref, a_s, acc):
    f = pl.program_id(2)

    @pl.when(f == 0)
    def _():
        x = h_ref[0]
        y = x * _rms_scale(x) * g_ref[...]
        a_s[...] = (y * (1.0 + mod_ref[0, 4:5, :]) + mod_ref[0, 3:4, :]).astype(BF16)
        acc[...] = jnp.zeros_like(acc)

    a = a_s[...]
    hid = _silu(_dot(a, wg_ref[...])) * _dot(a, wu_ref[...])
    acc[...] += _dot(hid.astype(BF16), wd_ref[...])

    @pl.when(f == pl.num_programs(2) - 1)
    def _():
        o_ref[0] = h_ref[0] + mod_ref[0, 5:6, :] * acc[...]


def _dense_ffn(h, g, mod, wg, wu, wd, tm, tf):
    bsz, t, d = h.shape
    f = wg.shape[1]
    tok = lambda b, i, j: (b, i, 0)
    return pl.pallas_call(
        _ffn_kernel,
        out_shape=jax.ShapeDtypeStruct((bsz, t, d), F32),
        grid=(bsz, t // tm, f // tf),
        in_specs=[pl.BlockSpec((1, tm, d), tok),
                  pl.BlockSpec((1, d), lambda b, i, j: (0, 0)),
                  pl.BlockSpec((1, 6, d), lambda b, i, j: (b, 0, 0)),
                  pl.BlockSpec((d, tf), lambda b, i, j: (0, j)),
                  pl.BlockSpec((d, tf), lambda b, i, j: (0, j)),
                  pl.BlockSpec((tf, d), lambda b, i, j: (j, 0))],
        out_specs=pl.BlockSpec((1, tm, d), tok),
        scratch_shapes=[pltpu.VMEM((tm, d), BF16), pltpu.VMEM((tm, d), F32)],
        compiler_params=_cparams(("arbitrary", "arbitrary", "arbitrary")),
        name="dense_swiglu",
    )(h, g[None, :], mod, wg, wu, wd)


MOE_CHUNK = 256


def _moe_route_kernel(h_ref, g_ref, mod_ref, wrt_ref, a_ref, gates_ref, rank_ref, cnt_ref, *, tm):
    x = h_ref[0]
    y = x * _rms_scale(x) * g_ref[...]
    a = y * (1.0 + mod_ref[0, 4:5, :]) + mod_ref[0, 3:4, :]
    a_ref[0] = a.astype(BF16)
    logits = _dot3_nt(wrt_ref[...], a)
    row = lax.broadcasted_iota(I32, logits.shape, 0)
    v1 = jnp.max(logits, axis=0, keepdims=True)
    i1 = jnp.min(jnp.where(logits == v1, row, N_EXPERTS), axis=0, keepdims=True)
    rest = jnp.where(row == i1, -jnp.inf, logits)
    v2 = jnp.max(rest, axis=0, keepdims=True)
    i2 = jnp.min(jnp.where(rest == v2, row, N_EXPERTS), axis=0, keepdims=True)
    e2 = jnp.exp(v2 - v1)
    gates_ref[0] = jnp.where(row == i1, 1.0 / (1.0 + e2), 0.0) + jnp.where(row == i2, e2 / (1.0 + e2), 0.0)
    routed = jnp.where(row == i1, 1.0, jnp.where(row == i2, 1.0, 0.0))
    si = lax.broadcasted_iota(I32, (MOE_CHUNK, MOE_CHUNK), 0)
    sj = lax.broadcasted_iota(I32, (MOE_CHUNK, MOE_CHUNK), 1)
    before = jnp.where(si < sj, 1.0, 0.0).astype(BF16)
    carry = jnp.zeros((N_EXPERTS, 1), F32)
    parts = []
    for c in range(tm // MOE_CHUNK):
        rc = routed[:, c * MOE_CHUNK:(c + 1) * MOE_CHUNK]
        parts.append(_dot(rc.astype(BF16), before) + carry)
        carry = carry + jnp.sum(rc, axis=1, keepdims=True)
    rank = jnp.concatenate(parts, axis=1)
    rank_ref[0] = jnp.where(routed > 0.0, rank, -1.0)
    cnt_ref[0] = jnp.broadcast_to(carry, (N_EXPERTS, LANES)).astype(I32)


def _moe_expert_kernel(cnt_ref, a_ref, rank_t_ref, rank_c_ref, gate_c_ref, wg_ref, wu_ref, wd_ref, o_ref,
                       xc, yacc, *, tm, ne):
    i = pl.program_id(0)
    e = pl.program_id(1)
    f = pl.program_id(2)
    nch = (cnt_ref[i * ne + e] + (MOE_CHUNK - 1)) // MOE_CHUNK
    d = o_ref.shape[-1]

    @pl.when(jnp.logical_and(e == 0, f == 0))
    def _():
        o_ref[...] = jnp.zeros_like(o_ref)

    @pl.when(f == 0)
    def _():
        rrow = rank_t_ref[0, pl.ds(e, 1), :]

        def gather(c, carry):
            r0 = pl.multiple_of(c * MOE_CHUNK, MOE_CHUNK)
            tgt = (r0 + lax.broadcasted_iota(I32, (MOE_CHUNK, tm), 0)).astype(F32)
            pick = jnp.where(rrow == tgt, 1.0, 0.0).astype(BF16)
            xc[pl.ds(r0, MOE_CHUNK), :] = _dot(pick, a_ref[...]).astype(BF16)
            yacc[pl.ds(r0, MOE_CHUNK), :] = jnp.zeros((MOE_CHUNK, d), F32)
            return carry

        lax.fori_loop(0, nch, gather, 0)

    def ffn(c, carry):
        r0 = pl.multiple_of(c * MOE_CHUNK, MOE_CHUNK)
        x = xc[pl.ds(r0, MOE_CHUNK), :]
        hid = _silu(_dot(x, wg_ref[0])) * _dot(x, wu_ref[0])
        yacc[pl.ds(r0, MOE_CHUNK), :] += _dot(hid.astype(BF16), wd_ref[0])
        return carry

    lax.fori_loop(0, nch, ffn, 0)

    @pl.when(f == pl.num_programs(2) - 1)
    def _():
        lane = lax.broadcasted_iota(I32, (tm, ne), 1)
        rcol = jnp.sum(jnp.where(lane == e, rank_c_ref[0], 0.0), axis=1, keepdims=True)
        gcol = jnp.sum(jnp.where(lane == e, gate_c_ref[0], 0.0), axis=1, keepdims=True)

        def scatter(c, carry):
            r0 = pl.multiple_of(c * MOE_CHUNK, MOE_CHUNK)
            tgt = (r0 + lax.broadcasted_iota(I32, (tm, MOE_CHUNK), 1)).astype(F32)
            place = jnp.where(rcol == tgt, 1.0, 0.0).astype(BF16)
            o_ref[...] += gcol * _dot(place, yacc[pl.ds(r0, MOE_CHUNK), :].astype(BF16))
            return carry

        lax.fori_loop(0, nch, scatter, 0)


def _residual_kernel(h_ref, y_ref, mod_ref, o_ref):
    o_ref[0] = h_ref[0] + mod_ref[0, 5:6, :] * y_ref[0]


def _moe_ffn(h, g, mod, router, wg, wu, wd, tm, tf):
    bsz, t, d = h.shape
    ne, _, f = wg.shape
    assert ne == N_EXPERTS and tm % MOE_CHUNK == 0
    nt = t // tm
    tok = lambda b, i: (b, i, 0)
    a, gates_t, rank_t, cnt = pl.pallas_call(
        functools.partial(_moe_route_kernel, tm=tm),
        out_shape=(jax.ShapeDtypeStruct((bsz, t, d), BF16),
                   jax.ShapeDtypeStruct((bsz * nt, ne, tm), F32),
                   jax.ShapeDtypeStruct((bsz * nt, ne, tm), F32),
                   jax.ShapeDtypeStruct((bsz * nt, ne, LANES), I32)),
        grid=(bsz, nt),
        in_specs=[pl.BlockSpec((1, tm, d), tok),
                  pl.BlockSpec((1, d), lambda b, i: (0, 0)),
                  pl.BlockSpec((1, 6, d), lambda b, i: (b, 0, 0)),
                  pl.BlockSpec((ne, d), lambda b, i: (0, 0))],
        out_specs=(pl.BlockSpec((1, tm, d), tok),
                   pl.BlockSpec((1, ne, tm), lambda b, i: (b * nt + i, 0, 0)),
                   pl.BlockSpec((1, ne, tm), lambda b, i: (b * nt + i, 0, 0)),
                   pl.BlockSpec((1, ne, LANES), lambda b, i: (b * nt + i, 0, 0))),
        compiler_params=_cparams(("arbitrary", "arbitrary")),
        name="moe_route",
    )(h, g[None, :], mod, router.T)
    counts = cnt[:, :, 0].reshape(-1)
    y = pl.pallas_call(
        functools.partial(_moe_expert_kernel, tm=tm, ne=ne),
        out_shape=jax.ShapeDtypeStruct((bsz * t, d), F32),
        grid_spec=pltpu.PrefetchScalarGridSpec(
            num_scalar_prefetch=1,
            grid=(bsz * nt, ne, f // tf),
            in_specs=[pl.BlockSpec((tm, d), lambda i, e, j, cnt: (i, 0)),
                      pl.BlockSpec((1, ne, tm), lambda i, e, j, cnt: (i, 0, 0)),
                      pl.BlockSpec((1, tm, ne), lambda i, e, j, cnt: (i, 0, 0)),
                      pl.BlockSpec((1, tm, ne), lambda i, e, j, cnt: (i, 0, 0)),
                      pl.BlockSpec((1, d, tf), lambda i, e, j, cnt: (e, 0, j)),
                      pl.BlockSpec((1, d, tf), lambda i, e, j, cnt: (e, 0, j)),
                      pl.BlockSpec((1, tf, d), lambda i, e, j, cnt: (e, j, 0))],
            out_specs=pl.BlockSpec((tm, d), lambda i, e, j, cnt: (i, 0)),
            scratch_shapes=[pltpu.VMEM((tm, d), BF16), pltpu.VMEM((tm, d), F32)]),
        compiler_params=_cparams(("arbitrary", "arbitrary", "arbitrary")),
        name="moe_experts",
    )(counts, a.reshape(bsz * t, d), rank_t, rank_t.transpose(0, 2, 1), gates_t.transpose(0, 2, 1), wg, wu, wd)
    tr = 512
    return pl.pallas_call(
        _residual_kernel,
        out_shape=jax.ShapeDtypeStruct((bsz, t, d), F32),
        grid=(bsz, t // tr),
        in_specs=[pl.BlockSpec((1, tr, d), tok), pl.BlockSpec((1, tr, d), tok),
                  pl.BlockSpec((1, 6, d), lambda b, i: (b, 0, 0))],
        out_specs=pl.BlockSpec((1, tr, d), tok),
        compiler_params=_cparams(("arbitrary", "arbitrary")),
        name="moe_residual",
    )(h, y.reshape(bsz, t, d), mod)


def _pack_w_in(w_in, d_mix):
    n_gate = 3 * (d_mix // 4 // HEAD_DIM)
    pad = jnp.zeros((w_in.shape[0], GATE_PAD - n_gate), w_in.dtype)
    return jnp.concatenate([w_in, pad], axis=1).astype(BF16)


def _token_mixing(h, mod, positions, g_mix, w_in, w_out, conv_w, ssm, nsa):
    bsz, t, d = h.shape
    dq = d // 4
    ngrp = NSA_KV_HEADS
    gw = ngrp * HEAD_DIM
    z = _in_proj(h, g_mix, mod, _pack_w_in(w_in, d), tm=512)
    o1 = 3 * dq
    o2 = o1 + dq
    o3 = o2 + 3 * dq
    o4 = o3 + dq
    o5 = o4 + 6 * gw
    y_conv = _short_conv(z[..., :o1], conv_w, tt=512)
    y_ssm = _s5(z[..., o1:o2], *ssm, tt=256)

    nh_sb = dq // HEAD_DIM
    sb = z[..., o2:o3].reshape(bsz, t, 3, nh_sb, HEAD_DIM).transpose(2, 0, 3, 1, 4)
    y_sb = _stick_breaking((sb[0] * HEAD_DIM ** -0.5).astype(BF16), sb[1].astype(BF16), sb[2].astype(BF16),
                           tq=256)
    y_sb = y_sb.transpose(0, 2, 1, 3).reshape(bsz, t, dq)

    o_c, o_s, o_w = _nsa_mixer(z[..., o3:o4], z[..., o4:o5], z[..., o5:], positions, nsa)
    return _out_proj(h, [y_conv, y_ssm, y_sb, o_c, o_s, o_w], mod, w_out.astype(BF16), tm=512)


def _nsa_mixer(zq, kv, glog, positions, nsa):
    bsz, t, _ = zq.shape
    ngrp = NSA_KV_HEADS
    gw = ngrp * HEAD_DIM
    q_norm_g, k_norm_g, pos_k, pos_v, k_w1, k_w2, v_w1, v_w2 = nsa
    half = HEAD_DIM // 2
    inv_freq = jnp.power(jnp.float32(ROPE_THETA), -jnp.arange(half, dtype=F32) / half)
    ang = positions.astype(F32)[..., None] * inv_freq
    ang_tok = jnp.tile(ang, (1, 1, LANES // half))
    q_rot, kaug, kwin = _nsa_prep(zq, kv[..., 2 * gw:3 * gw], kv[..., 4 * gw:5 * gw], ang_tok,
                                  q_norm_g, k_norm_g, tm=512)

    nch = t // CMP_STRIDE

    def chunks(x):
        return x.reshape(bsz, nch, CMP_STRIDE, ngrp, HEAD_DIM).transpose(0, 3, 1, 2, 4).reshape(
            bsz, ngrp, nch, CMP_STRIDE * HEAD_DIM)

    end_idx = jnp.minimum(jnp.arange(nch) * CMP_STRIDE + CMP_LEN - 1, t - 1)
    angc = jnp.tile(positions[:, end_idx].astype(F32)[..., None] * inv_freq, (1, 1, 2))
    kc, vc = _compress(chunks(kv[..., :gw]), chunks(kv[..., gw:2 * gw]), k_w1, k_w2, v_w1, v_w2,
                       pos_k, pos_v, k_norm_g, angc)
    o_c, sel = _cmp_attn(q_rot, kc, vc, glog, tq=256)

    def heads(x):
        return x.reshape(bsz, t, ngrp, HEAD_DIM).transpose(0, 2, 1, 3).astype(BF16)

    o_s = _sel_attn(q_rot, sel, kaug, heads(kv[..., 3 * gw:4 * gw]), glog, tq=256, tk=512)
    o_w = _win_attn(q_rot, kwin, heads(kv[..., 5 * gw:6 * gw]), glog, tq=256)
    return o_c, o_s, o_w


def kernel(x, c, positions, ada_w, ada_b, norm_mix_g, norm_ffn_g, w_in, w_out, conv_w, ssm_lam_re, ssm_lam_im, ssm_b_re, ssm_b_im, ssm_c_re, ssm_c_im, ssm_d, ssm_log_dt, ssm_w_glu, nsa_q_norm_g, nsa_k_norm_g, cmp_pos_k, cmp_pos_v, cmp_k_w1, cmp_k_w2, cmp_v_w1, cmp_v_w2, ffn_w_gate, ffn_w_up, ffn_w_down, moe_router, moe_w_gate, moe_w_up, moe_w_down):
    depth = ada_w.shape[0]
    mods = _modulation(c, ada_w, ada_b)
    h = x
    for layer in range(depth):
        mod = mods[layer]
        ssm = (ssm_lam_re[layer], ssm_lam_im[layer], ssm_b_re[layer], ssm_b_im[layer], ssm_c_re[layer],
               ssm_c_im[layer], ssm_d[layer], ssm_log_dt[layer], ssm_w_glu[layer])
        nsa = (nsa_q_norm_g[layer], nsa_k_norm_g[layer], cmp_pos_k[layer], cmp_pos_v[layer],
               cmp_k_w1[layer], cmp_k_w2[layer], cmp_v_w1[layer], cmp_v_w2[layer])
        h = _token_mixing(h, mod, positions, norm_mix_g[layer], w_in[layer], w_out[layer], conv_w[layer],
                          ssm, nsa)
        i = layer // 2
        if layer % 2 == 0:
            h = _dense_ffn(h, norm_ffn_g[layer], mod, ffn_w_gate[i].astype(BF16), ffn_w_up[i].astype(BF16),
                           ffn_w_down[i].astype(BF16), tm=512, tf=512)
        else:
            h = _moe_ffn(h, norm_ffn_g[layer], mod, moe_router[i], moe_w_gate[i].astype(BF16),
                         moe_w_up[i].astype(BF16), moe_w_down[i].astype(BF16), tm=min(2048, h.shape[1]),
                         tf=512)
    return h
```

```python
import functools
import math

import jax
import jax.numpy as jnp
from jax import lax
from jax.experimental import pallas as pl
from jax.experimental.pallas import tpu as pltpu

F32 = jnp.float32
BF16 = jnp.bfloat16
I32 = jnp.int32

HEAD_DIM = 64
CONV_WIDTH = 3
SSM_GROUP = 16
SSM_STATE = 64
SSM_MAX_RE = -1e-4
NSA_KV_HEADS = 2
CMP_LEN = 32
CMP_STRIDE = 16
SLC_BLOCK = 64
SLC_TOPK = 16
WINDOW = 512
FORCE_BONUS = 1e4
NEG_INF = -1e30
ROPE_THETA = 10000.0
RMS_EPS = 1e-6
N_EXPERTS = 8

LANES = 128
SUBLANES = 8
VMEM_LIMIT = 56 * 1024 * 1024
SEL_MASK_BIAS = -30000.0
SB_SKIP_LOG = -110.0
GATE_PAD = LANES


def _cparams(sem):
    return pltpu.CompilerParams(dimension_semantics=sem, vmem_limit_bytes=VMEM_LIMIT)


def _dot(a, b):
    return jnp.dot(a, b, preferred_element_type=F32)


def _dot_nt(a, b):
    return lax.dot_general(a, b, (((1,), (1,)), ((), ())), preferred_element_type=F32)


def _split(x):
    hi = x.astype(BF16)
    lo = (x - hi.astype(F32)).astype(BF16)
    return hi, lo


def _dot3(a, b):
    ah, al = _split(a)
    bh, bl = _split(b)
    return _dot(ah, bh) + _dot(ah, bl) + _dot(al, bh)


def _dot3_nt(a, b):
    ah, al = _split(a)
    bh, bl = _split(b)
    return _dot_nt(ah, bh) + _dot_nt(ah, bl) + _dot_nt(al, bh)


def _dot2_exact_rhs(a, b_bf16):
    ah, al = _split(a)
    return _dot(ah, b_bf16) + _dot(al, b_bf16)


def _silu(x):
    return x * jax.nn.sigmoid(x)


def _div_pow2(x, n):
    return lax.shift_right_logical(x, jnp.int32(n.bit_length() - 1))


def _mod_pow2(x, n):
    return x & (n - 1)


def _rms_scale(x):
    return lax.rsqrt(jnp.mean(x * x, axis=-1, keepdims=True) + RMS_EPS)


def _mod_kernel(c_ref, w_ref, b_ref, o_ref):
    o_ref[0] = _dot3(_silu(c_ref[...]), w_ref[0]) + b_ref[0]


def _modulation(c, ada_w, ada_b):
    depth, d, n6 = ada_w.shape
    bsz = c.shape[0]
    rows = -(-bsz // SUBLANES) * SUBLANES
    cpad = jnp.zeros((rows, d), F32).at[:bsz].set(c)
    tn = n6 // 4
    out = pl.pallas_call(
        _mod_kernel,
        out_shape=jax.ShapeDtypeStruct((depth, rows, n6), F32),
        grid=(depth, n6 // tn),
        in_specs=[pl.BlockSpec((rows, d), lambda l, j: (0, 0)),
                  pl.BlockSpec((1, d, tn), lambda l, j: (l, 0, j)),
                  pl.BlockSpec((1, 1, tn), lambda l, j: (l, 0, j))],
        out_specs=pl.BlockSpec((1, rows, tn), lambda l, j: (l, 0, j)),
        compiler_params=_cparams(("arbitrary", "arbitrary")),
        name="adaln_mod",
    )(cpad, ada_w, ada_b[:, None, :])
    return out[:, :bsz].reshape(depth, bsz, 6, d)


def _in_kernel(h_ref, g_ref, mod_ref, w_ref, zb_ref, zf_ref):
    x = h_ref[0]
    y = x * _rms_scale(x) * g_ref[...]
    a = y * (1.0 + mod_ref[0, 1:2, :]) + mod_ref[0, 0:1, :]
    z = _dot(a.astype(BF16), w_ref[...])
    nb = zb_ref.shape[-1]
    zb_ref[0] = z[:, :nb].astype(BF16)
    zf_ref[0] = z[:, nb:]


def _in_proj(h, g, mod, w_bf16, nb, tm):
    bsz, t, d = h.shape
    nz = w_bf16.shape[1]
    return pl.pallas_call(
        _in_kernel,
        out_shape=(jax.ShapeDtypeStruct((bsz, t, nb), BF16),
                   jax.ShapeDtypeStruct((bsz, t, nz - nb), F32)),
        grid=(bsz, t // tm),
        in_specs=[pl.BlockSpec((1, tm, d), lambda b, i: (b, i, 0)),
                  pl.BlockSpec((1, d), lambda b, i: (0, 0)),
                  pl.BlockSpec((1, 6, d), lambda b, i: (b, 0, 0)),
                  pl.BlockSpec((d, nz), lambda b, i: (0, 0))],
        out_specs=(pl.BlockSpec((1, tm, nb), lambda b, i: (b, i, 0)),
                   pl.BlockSpec((1, tm, nz - nb), lambda b, i: (b, i, 0))),
        compiler_params=_cparams(("arbitrary", "arbitrary")),
        name="in_proj",
    )(h, g[None, :], mod, w_bf16)


def _conv_kernel(z_ref, w_ref, o_ref, buf, *, tt, dc):
    @pl.when(pl.program_id(1) == 0)
    def _():
        buf[0:SUBLANES, :] = jnp.zeros((SUBLANES, dc), F32)

    z = z_ref[0].astype(F32)
    gate_b = z[:, :dc]
    v = z[:, dc:2 * dc] * z[:, 2 * dc:]
    buf[SUBLANES:, :] = v
    y = (w_ref[2:3, :] * v
         + w_ref[1:2, :] * buf[SUBLANES - 1:SUBLANES - 1 + tt, :]
         + w_ref[0:1, :] * buf[SUBLANES - 2:SUBLANES - 2 + tt, :])
    o_ref[0] = gate_b * y
    buf[0:SUBLANES, :] = v[tt - SUBLANES:, :]


def _short_conv(zb, conv_w, tt):
    bsz, t, _ = zb.shape
    dc = conv_w.shape[1]
    c3 = 3 * dc
    return pl.pallas_call(
        functools.partial(_conv_kernel, tt=tt, dc=dc),
        out_shape=jax.ShapeDtypeStruct((bsz, t, dc), F32),
        grid=(bsz, t // tt),
        in_specs=[pl.BlockSpec((1, tt, c3), lambda b, i: (b, i, 0)),
                  pl.BlockSpec((CONV_WIDTH, dc), lambda b, i: (0, 0))],
        out_specs=pl.BlockSpec((1, tt, dc), lambda b, i: (b, i, 0)),
        scratch_shapes=[pltpu.VMEM((tt + SUBLANES, dc), F32)],
        compiler_params=_cparams(("arbitrary", "arbitrary")),
        name="short_conv",
    )(zb, conv_w)


SSM_LANE_CHUNK = 2 * LANES


def _gelu_tanh(x):
    return 0.5 * x * (1.0 + jnp.tanh(math.sqrt(2.0 / math.pi) * (x + 0.044715 * (x * x * x))))


def _ssm_kernel(u_ref, wbu_ref, pre_ref, pim_ref, wc_ref, d_ref, wglu_ref, o_ref,
                bre, bim, cre, cim, *, tt, ns):
    @pl.when(pl.program_id(1) == 0)
    def _():
        cre[...] = jnp.zeros_like(cre)
        cim[...] = jnp.zeros_like(cim)

    u = u_ref[0]
    bu = _dot(u.astype(BF16), wbu_ref[...])
    bre[...] = bu[:, :ns]
    bim[...] = bu[:, ns:]
    lw = SSM_LANE_CHUNK
    row = lax.broadcasted_iota(I32, (SUBLANES, lw), 0)

    def local_scan(i, carry):
        r0 = pl.multiple_of(i * SUBLANES, SUBLANES)
        for lg in range(ns // lw):
            sl = slice(lg * lw, (lg + 1) * lw)
            xr = bre[pl.ds(r0, SUBLANES), sl]
            xi = bim[pl.ds(r0, SUBLANES), sl]
            for k in (1, 2, 4):
                ar = pre_ref[k - 1:k, sl]
                ai = pim_ref[k - 1:k, sl]
                sr = jnp.where(row >= k, pltpu.roll(xr, k, 0), 0.0)
                si = jnp.where(row >= k, pltpu.roll(xi, k, 0), 0.0)
                xr, xi = xr + ar * sr - ai * si, xi + ar * si + ai * sr
            bre[pl.ds(r0, SUBLANES), sl] = xr
            bim[pl.ds(r0, SUBLANES), sl] = xi
        return carry

    lax.fori_loop(0, tt // SUBLANES, local_scan, 0, unroll=2)

    def carry_in(i, carry):
        c_re, c_im = carry
        r0 = pl.multiple_of(i * SUBLANES, SUBLANES)
        pr = pre_ref[...]
        pi_ = pim_ref[...]
        xr = bre[pl.ds(r0, SUBLANES), :] + pr * c_re - pi_ * c_im
        xi = bim[pl.ds(r0, SUBLANES), :] + pr * c_im + pi_ * c_re
        bre[pl.ds(r0, SUBLANES), :] = xr
        bim[pl.ds(r0, SUBLANES), :] = xi
        return xr[SUBLANES - 1:], xi[SUBLANES - 1:]

    c_re, c_im = lax.fori_loop(0, tt // SUBLANES, carry_in, (cre[...], cim[...]))
    cre[...] = c_re
    cim[...] = c_im
    hcat = jnp.concatenate([bre[...], bim[...]], axis=1).astype(BF16)
    y = _dot(hcat, wc_ref[...]) + d_ref[...] * u
    g = _gelu_tanh(y)
    o_ref[0] = g * jax.nn.sigmoid(_dot(g.astype(BF16), wglu_ref[...]))


def _s5(u, lam_re, lam_im, b_re, b_im, c_re, c_im, d_skip, log_dt, w_glu, tt):
    bsz, t, _ = u.shape
    ds = d_skip.shape[0]
    g, p = lam_re.shape
    hc = SSM_GROUP
    ns = g * p
    lam = lax.complex(jnp.minimum(lam_re, SSM_MAX_RE), lam_im)
    dt = jnp.exp(log_dt)[:, None]
    lam_bar = jnp.exp(lam * dt)
    b_bar = ((lam_bar - 1.0) / lam)[..., None] * lax.complex(b_re, b_im)
    eye = jnp.eye(g, dtype=F32)
    wbu = jnp.concatenate(
        [jnp.einsum('gpc,gh->gchp', part, eye).reshape(g * hc, ns) for part in (b_bar.real, b_bar.imag)],
        axis=1).astype(BF16)
    wc = jnp.concatenate(
        [jnp.einsum('gcp,gh->gphc', part, eye).reshape(ns, g * hc) for part in (c_re, -c_im)],
        axis=0).astype(BF16)
    pows = jnp.exp(jnp.arange(1, SUBLANES + 1, dtype=F32)[:, None, None] * (lam * dt)[None])
    pows = pows.reshape(SUBLANES, ns)
    return pl.pallas_call(
        functools.partial(_ssm_kernel, tt=tt, ns=ns),
        out_shape=jax.ShapeDtypeStruct((bsz, t, ds), F32),
        grid=(bsz, t // tt),
        in_specs=[pl.BlockSpec((1, tt, ds), lambda b, i: (b, i, 0)),
                  pl.BlockSpec((ds, 2 * ns), lambda b, i: (0, 0)),
                  pl.BlockSpec((SUBLANES, ns), lambda b, i: (0, 0)),
                  pl.BlockSpec((SUBLANES, ns), lambda b, i: (0, 0)),
                  pl.BlockSpec((2 * ns, ds), lambda b, i: (0, 0)),
                  pl.BlockSpec((1, ds), lambda b, i: (0, 0)),
                  pl.BlockSpec((ds, ds), lambda b, i: (0, 0))],
        out_specs=pl.BlockSpec((1, tt, ds), lambda b, i: (b, i, 0)),
        scratch_shapes=[pltpu.VMEM((tt, ns), F32), pltpu.VMEM((tt, ns), F32),
                        pltpu.VMEM((1, ns), F32), pltpu.VMEM((1, ns), F32)],
        compiler_params=_cparams(("arbitrary", "arbitrary")),
        name="s5_scan",
    )(u, wbu, pows.real, pows.imag, wc, d_skip[None, :], w_glu.astype(BF16))


def _sb_kernel(q_ref, k_ref, v_ref, o_ref, acc, csum, *, tq, nh):
    qi = pl.program_id(1)
    acc[...] = jnp.zeros_like(acc)
    csum[...] = jnp.zeros_like(csum)
    ti = lax.broadcasted_iota(I32, (tq, tq), 0)
    ji = lax.broadcasted_iota(I32, (tq, tq), 1)
    later = (ti > ji).astype(BF16)

    def cond(state):
        kb, cmax = state
        return jnp.logical_and(kb >= 0, cmax > SB_SKIP_LOG)

    def body(state):
        kb, _ = state
        k0 = pl.multiple_of(kb * tq, tq)
        past = (k0 + ji) < (qi * tq + ti)
        cmax = jnp.float32(-jnp.inf)
        for h in range(nh):
            hs = slice(h * HEAD_DIM, (h + 1) * HEAD_DIM)
            z = _dot_nt(q_ref[0, :, hs], k_ref[0, pl.ds(k0, tq), hs])
            log_beta = jnp.minimum(z, 0.0) - jnp.log(1.0 + jnp.exp(-jnp.abs(z)))
            log_keep = jnp.where(past, log_beta - z, 0.0)
            hi, lo = _split(log_keep)
            inner = _dot(hi, later) + _dot(lo, later)
            c = csum[h]
            w = jnp.where(past, jnp.exp(log_beta + inner + c[:, :1]), 0.0)
            acc[:, hs] += _dot(w.astype(BF16), v_ref[0, pl.ds(k0, tq), hs])
            c = c + jnp.sum(log_keep, axis=1, keepdims=True)
            csum[h] = c
            cmax = jnp.maximum(cmax, jnp.max(c))
        return kb - 1, cmax

    lax.while_loop(cond, body, (qi, jnp.float32(0.0)))
    o_ref[0] = acc[...]


SB_Q_BLOCK, SB_K_BLOCK, SB_V_BLOCK = 3, 4, 5


def _stick_breaking(zb, dq, tq):
    bsz, t, _ = zb.shape
    nh = dq // HEAD_DIM
    return pl.pallas_call(
        functools.partial(_sb_kernel, tq=tq, nh=nh),
        out_shape=jax.ShapeDtypeStruct((bsz, t, dq), F32),
        grid=(bsz, t // tq),
        in_specs=[pl.BlockSpec((1, tq, dq), lambda b, i: (b, i, SB_Q_BLOCK)),
                  pl.BlockSpec((1, t, dq), lambda b, i: (b, 0, SB_K_BLOCK)),
                  pl.BlockSpec((1, t, dq), lambda b, i: (b, 0, SB_V_BLOCK))],
        out_specs=pl.BlockSpec((1, tq, dq), lambda b, i: (b, i, 0)),
        scratch_shapes=[pltpu.VMEM((tq, dq), F32), pltpu.VMEM((nh, tq, LANES), F32)],
        compiler_params=_cparams(("arbitrary", "arbitrary")),
        name="stick_breaking",
    )(zb, zb, zb)


def _head_norm(x, gsum_ref, gain):
    ss = _dot2_exact_rhs(x * x, gsum_ref[...]) * (1.0 / HEAD_DIM)
    return x * lax.rsqrt(ss + RMS_EPS) * gain


def _rope_lanes(x, cos, sin):
    w = x.shape[-1]
    lane = lax.broadcasted_iota(I32, x.shape, 1)
    first = _mod_pow2(lane, HEAD_DIM) < (HEAD_DIM // 2)
    partner = jnp.where(first, -pltpu.roll(x, w - HEAD_DIM // 2, 1), pltpu.roll(x, HEAD_DIM // 2, 1))
    return x * cos + partner * sin


def _nsa_prep_kernel(zq_ref, zks_ref, zvs_ref, zkw_ref, zvw_ref, ang_ref, gq_ref, gk_ref, gsq_ref, gsk_ref,
                     q_ref, kaug_ref, vaug_ref, kw_ref, vw_ref, *, tm, ngrp, nblk):
    ang = ang_ref[0]
    cos1 = jnp.cos(ang)
    sin1 = jnp.sin(ang)
    wq = zq_ref.shape[-1]
    cosq = jnp.concatenate([cos1] * (wq // LANES), axis=1)
    sinq = jnp.concatenate([sin1] * (wq // LANES), axis=1)
    q = _rope_lanes(_head_norm(zq_ref[0], gsq_ref, gq_ref[...]), cosq, sinq)
    q_ref[0] = q * (HEAD_DIM ** -0.5)
    ks = _rope_lanes(_head_norm(zks_ref[0], gsk_ref, gk_ref[...]), cos1, sin1)
    kw = _rope_lanes(_head_norm(zkw_ref[0], gsk_ref, gk_ref[...]), cos1, sin1)
    t0 = pl.program_id(1) * tm
    tok = t0 + lax.broadcasted_iota(I32, (tm, nblk), 0)
    blk = lax.broadcasted_iota(I32, (tm, nblk), 1)
    onehot = jnp.where(_div_pow2(tok, SLC_BLOCK) == blk, 1.0, 0.0).astype(BF16)
    vs = zvs_ref[0]
    vw = zvw_ref[0]
    lane = lax.broadcasted_iota(I32, (tm, LANES - HEAD_DIM), 1)
    ones_col = jnp.where(lane == 0, 1.0, 0.0).astype(BF16)
    for g in range(ngrp):
        sl = slice(g * HEAD_DIM, (g + 1) * HEAD_DIM)
        kaug_ref[0, g] = jnp.concatenate([onehot, ks[:, sl].astype(BF16)], axis=1)
        vaug_ref[0, g] = jnp.concatenate([vs[:, sl].astype(BF16), ones_col], axis=1)
        kw_ref[0, g] = kw[:, sl].astype(BF16)
        vw_ref[0, g] = vw[:, sl].astype(BF16)


ZF_Q_BLOCK = 1
ZF_KCMP, ZF_VCMP, ZF_KSLC, ZF_VSLC, ZF_KWIN, ZF_VWIN, ZF_GATE = 4, 5, 6, 7, 8, 9, 10


def _nsa_prep(zf, ang, gq, gk, wq, tm):
    bsz, t, _ = zf.shape
    ngrp = NSA_KV_HEADS
    nblk = t // SLC_BLOCK
    wk = ngrp * HEAD_DIM
    assert wk == LANES and wq == 2 * wk
    lane_q = jnp.arange(wq) // HEAD_DIM
    gsq = (lane_q[:, None] == lane_q[None, :]).astype(BF16)
    lane_k = jnp.arange(wk) // HEAD_DIM
    gsk = (lane_k[:, None] == lane_k[None, :]).astype(BF16)
    kvspec = lambda blk: pl.BlockSpec((1, tm, wk), lambda b, i: (b, i, blk))
    hspec = lambda w: pl.BlockSpec((1, ngrp, tm, w), lambda b, i: (b, 0, i, 0))
    return pl.pallas_call(
        functools.partial(_nsa_prep_kernel, tm=tm, ngrp=ngrp, nblk=nblk),
        out_shape=(jax.ShapeDtypeStruct((bsz, t, wq), F32),
                   jax.ShapeDtypeStruct((bsz, ngrp, t, HEAD_DIM + nblk), BF16),
                   jax.ShapeDtypeStruct((bsz, ngrp, t, LANES), BF16),
                   jax.ShapeDtypeStruct((bsz, ngrp, t, HEAD_DIM), BF16),
                   jax.ShapeDtypeStruct((bsz, ngrp, t, HEAD_DIM), BF16)),
        grid=(bsz, t // tm),
        in_specs=[pl.BlockSpec((1, tm, wq), lambda b, i: (b, i, ZF_Q_BLOCK)),
                  kvspec(ZF_KSLC), kvspec(ZF_VSLC), kvspec(ZF_KWIN), kvspec(ZF_VWIN),
                  pl.BlockSpec((1, tm, LANES), lambda b, i: (b, i, 0)),
                  pl.BlockSpec((1, wq), lambda b, i: (0, 0)),
                  pl.BlockSpec((1, wk), lambda b, i: (0, 0)),
                  pl.BlockSpec((wq, wq), lambda b, i: (0, 0)),
                  pl.BlockSpec((wk, wk), lambda b, i: (0, 0))],
        out_specs=(pl.BlockSpec((1, tm, wq), lambda b, i: (b, i, 0)),
                   hspec(HEAD_DIM + nblk), hspec(LANES), hspec(HEAD_DIM), hspec(HEAD_DIM)),
        compiler_params=_cparams(("arbitrary", "arbitrary")),
        name="nsa_prep",
    )(zf, zf, zf, zf, zf, ang, jnp.tile(gq, wq // HEAD_DIM)[None, :], jnp.tile(gk, ngrp)[None, :], gsq, gsk)


def _compress_kernel(ck_ref, cv_ref, w1k_ref, w2k_ref, w1v_ref, w2v_ref, pek_ref, pev_ref,
                     gk_ref, angc_ref, kc_ref, vc_ref, *, nch, half):
    def mlp(c_ref, w1_ref, w2_ref, pe_ref):
        x = c_ref[0, 0]
        w1 = w1_ref[...]
        first = _dot3(x, w1[:half])
        second = _dot3(x, w1[half:])
        bias = _dot3(pe_ref[...], w1)[0:1]
        hid = _silu(first + pltpu.roll(second, nch - 1, 0) + bias)
        return _dot3(hid, w2_ref[...])

    kc = mlp(ck_ref, w1k_ref, w2k_ref, pek_ref)
    kc = kc * _rms_scale(kc) * gk_ref[...]
    ang = angc_ref[0]
    half_d = HEAD_DIM // 2
    k1 = kc[:, :half_d]
    k2 = kc[:, half_d:]
    cos = jnp.cos(ang[:, :half_d])
    sin = jnp.sin(ang[:, :half_d])
    kc_ref[0, 0] = jnp.concatenate([k1 * cos - k2 * sin, k2 * cos + k1 * sin], axis=1)
    vc_ref[0, 0] = mlp(cv_ref, w1v_ref, w2v_ref, pev_ref)


def _compress(ck, cv, w1k, w2k, w1v, w2v, pos_k, pos_v, gk, angc):
    bsz, ngrp, nch, half = ck.shape
    hid = w1k.shape[1]

    def pe_rows(pe):
        return jnp.zeros((SUBLANES, 2 * half), F32).at[0].set(pe.reshape(-1))

    blk4 = lambda b, g: (b, g, 0, 0)
    full2 = lambda b, g: (0, 0)
    return pl.pallas_call(
        functools.partial(_compress_kernel, nch=nch, half=half),
        out_shape=(jax.ShapeDtypeStruct((bsz, ngrp, nch, HEAD_DIM), F32),
                   jax.ShapeDtypeStruct((bsz, ngrp, nch, HEAD_DIM), F32)),
        grid=(bsz, ngrp),
        in_specs=[pl.BlockSpec((1, 1, nch, half), blk4),
                  pl.BlockSpec((1, 1, nch, half), blk4),
                  pl.BlockSpec((2 * half, hid), full2),
                  pl.BlockSpec((hid, HEAD_DIM), full2),
                  pl.BlockSpec((2 * half, hid), full2),
                  pl.BlockSpec((hid, HEAD_DIM), full2),
                  pl.BlockSpec((SUBLANES, 2 * half), full2),
                  pl.BlockSpec((SUBLANES, 2 * half), full2),
                  pl.BlockSpec((1, HEAD_DIM), full2),
                  pl.BlockSpec((1, nch, HEAD_DIM), lambda b, g: (b, 0, 0))],
        out_specs=(pl.BlockSpec((1, 1, nch, HEAD_DIM), blk4),
                   pl.BlockSpec((1, 1, nch, HEAD_DIM), blk4)),
        compiler_params=_cparams(("arbitrary", "arbitrary")),
        name="nsa_compress",
    )(ck, cv, w1k, w2k, w1v, w2v, pe_rows(pos_k), pe_rows(pos_v), gk[None, :], angc)


def _gate_col(gates, head, branch):
    c = head * 3 + branch
    if isinstance(c, int):
        return gates[:, c:c + 1]
    lane = lax.broadcasted_iota(I32, gates.shape, 1)
    return jnp.sum(jnp.where(lane == c, gates, 0.0), axis=1, keepdims=True)


def _cmp_attn_kernel(q_ref, kc_ref, vct_ref, gl_ref, ovt_ref, oc_ref, sel_ref, *, tq, ngrp, hpg, nch, nblk):
    t = pl.program_id(1) * tq + lax.broadcasted_iota(I32, (1, tq), 1)
    cmp_end = lax.broadcasted_iota(I32, (nch, 1), 0) * CMP_STRIDE + (CMP_LEN - 1)
    visible = cmp_end <= t
    q = q_ref[0]
    gates = jax.nn.sigmoid(gl_ref[0])
    blk = lax.broadcasted_iota(I32, (nblk, tq), 0)
    cur = _div_pow2(t, SLC_BLOCK)
    forced = (blk == 0) | (blk == cur) | (blk == cur - 1)
    causal = blk <= cur
    lane = lax.broadcasted_iota(I32, (tq, hpg * HEAD_DIM), 1)
    outs = []
    for g in range(ngrp):
        kc = kc_ref[0, g]
        vct = vct_ref[0, g]
        psum = jnp.zeros((nch, tq), F32)
        o_t = []
        for hh in range(hpg):
            head = g * hpg + hh
            qh = q[:, head * HEAD_DIM:(head + 1) * HEAD_DIM]
            s = jnp.where(visible, _dot3_nt(kc, qh), NEG_INF)
            p = jnp.where(visible, jnp.exp(s - jnp.max(s, axis=0, keepdims=True)), 0.0)
            denom = jnp.sum(p, axis=0, keepdims=True)
            p = p / jnp.where(denom == 0.0, 1.0, denom)
            psum = psum + p
            o_t.append(_dot(vct, p.astype(BF16)))
        gate = _gate_col(gates, g * hpg, 0)
        for hh in range(1, hpg):
            gate = jnp.where(lane < hh * HEAD_DIM, gate, _gate_col(gates, g * hpg + hh, 0))
        outs.append(jnp.concatenate(o_t, axis=0).T * gate)
        hi, lo = _split(psum)
        imp = _dot(ovt_ref[...], hi) + _dot(ovt_ref[...], lo)
        val = jnp.where(causal, imp + jnp.where(forced, FORCE_BONUS, 0.0), NEG_INF)
        for _ in range(min(SLC_TOPK, nblk)):
            m = jnp.max(val, axis=0, keepdims=True)
            first = jnp.min(jnp.where(val == m, blk, nblk), axis=0, keepdims=True)
            val = jnp.where(blk == first, -jnp.inf, val)
        sel_ref[0, g] = jnp.where(val == -jnp.inf, 0.0, SEL_MASK_BIAS).T.astype(BF16)
    oc_ref[0] = jnp.concatenate(outs, axis=1)


def _cmp_attn(q, kc, vc, glog, tq):
    bsz, t, wq = q.shape
    ngrp, nch = kc.shape[1], kc.shape[2]
    hpg = wq // HEAD_DIM // ngrp
    nblk = t // SLC_BLOCK
    cs = jnp.arange(nch) * CMP_STRIDE
    ss = jnp.arange(nblk) * SLC_BLOCK
    ovt = ((cs[None, :] < ss[:, None] + SLC_BLOCK) & (cs[None, :] + CMP_LEN > ss[:, None])).astype(BF16)
    return pl.pallas_call(
        functools.partial(_cmp_attn_kernel, tq=tq, ngrp=ngrp, hpg=hpg, nch=nch, nblk=nblk),
        out_shape=(jax.ShapeDtypeStruct((bsz, t, wq), F32),
                   jax.ShapeDtypeStruct((bsz, ngrp, t, nblk), BF16)),
        grid=(bsz, t // tq),
        in_specs=[pl.BlockSpec((1, tq, wq), lambda b, i: (b, i, 0)),
                  pl.BlockSpec((1, ngrp, nch, HEAD_DIM), lambda b, i: (b, 0, 0, 0)),
                  pl.BlockSpec((1, ngrp, HEAD_DIM, nch), lambda b, i: (b, 0, 0, 0)),
                  pl.BlockSpec((1, tq, GATE_PAD), lambda b, i: (b, i, ZF_GATE)),
                  pl.BlockSpec((nblk, nch), lambda b, i: (0, 0))],
        out_specs=(pl.BlockSpec((1, tq, wq), lambda b, i: (b, i, 0)),
                   pl.BlockSpec((1, ngrp, tq, nblk), lambda b, i: (b, 0, i, 0))),
        compiler_params=_cparams(("arbitrary", "arbitrary")),
        name="nsa_cmp_select",
    )(q, kc, vc.transpose(0, 1, 3, 2).astype(BF16), glog, ovt)


SEL_ROW_CHUNK = 128


def _sel_attn_kernel(q_ref, sel_ref, kaug_ref, vaug_ref, gl_ref, o_ref, qa_s, m_s, acc, s_a, s_b, *, tq, tk, hpg):
    g = pl.program_id(1)
    qi = pl.program_id(2)
    q = q_ref[0]
    sel = sel_ref[0, 0]
    for hh in range(hpg):
        qa_s[hh * tq:(hh + 1) * tq, :] = jnp.concatenate(
            [sel, q[:, hh * HEAD_DIM:(hh + 1) * HEAD_DIM].astype(BF16)], axis=1)
    m_s[...] = jnp.full_like(m_s, NEG_INF)
    acc[...] = jnp.zeros_like(acc)
    rows = hpg * tq
    rc = SEL_ROW_CHUNK

    def scores(kt, buf):
        k0 = pl.multiple_of(kt * tk, tk)
        kblk = kaug_ref[0, 0, pl.ds(k0, tk), :]
        for c in range(rows // rc):
            rs = slice(c * rc, (c + 1) * rc)
            buf[rs, :] = _dot_nt(qa_s[rs, :], kblk)

    def consume(kt, buf, diag):
        k0 = pl.multiple_of(kt * tk, tk)
        vblk = vaug_ref[0, 0, pl.ds(k0, tk), :]
        for c in range(rows // rc):
            rs = slice(c * rc, (c + 1) * rc)
            s = buf[rs, :]
            if diag:
                t = qi * tq + _mod_pow2(c * rc + lax.broadcasted_iota(I32, (rc, tk), 0), tq)
                s = jnp.where(k0 + lax.broadcasted_iota(I32, (rc, tk), 1) <= t, s, NEG_INF)
            m_old = m_s[rs, :]
            m_new = jnp.maximum(m_old, jnp.max(s, axis=1, keepdims=True))
            alpha = jnp.exp(m_old - m_new)
            p = jnp.exp(s - m_new[:, :1])
            acc[rs, :] = alpha * acc[rs, :] + _dot(p.astype(BF16), vblk)
            m_s[rs, :] = m_new

    last = (qi * tq) // tk
    scores(0, s_a)

    def pair(j, carry):
        scores(2 * j + 1, s_b)
        consume(2 * j, s_a, False)
        scores(2 * j + 2, s_a)
        consume(2 * j + 1, s_b, False)
        return carry

    lax.fori_loop(0, last // 2, pair, 0)
    t0 = (last // 2) * 2

    @pl.when(t0 < last)
    def _():
        scores(last, s_b)
        consume(t0, s_a, False)
        consume(last, s_b, True)

    @pl.when(t0 == last)
    def _():
        consume(last, s_a, True)

    gates = jax.nn.sigmoid(gl_ref[0])
    a = acc[...]
    out = a[:, :HEAD_DIM] / a[:, HEAD_DIM:HEAD_DIM + 1]
    o_ref[0] = jnp.concatenate(
        [out[hh * tq:(hh + 1) * tq] * _gate_col(gates, g * hpg + hh, 1) for hh in range(hpg)], axis=1)


def _sel_attn(q, sel, kaug, vaug, glog, tq, tk):
    bsz, t, wq = q.shape
    ngrp = kaug.shape[1]
    hpg = wq // HEAD_DIM // ngrp
    gw = hpg * HEAD_DIM
    nblk = sel.shape[-1]
    ka = kaug.shape[-1]
    rows = hpg * tq
    assert tk % tq == 0 and rows % SEL_ROW_CHUNK == 0
    return pl.pallas_call(
        functools.partial(_sel_attn_kernel, tq=tq, tk=tk, hpg=hpg),
        out_shape=jax.ShapeDtypeStruct((bsz, t, wq), F32),
        grid=(bsz, ngrp, t // tq),
        in_specs=[pl.BlockSpec((1, tq, gw), lambda b, g, i: (b, i, g)),
                  pl.BlockSpec((1, 1, tq, nblk), lambda b, g, i: (b, g, i, 0)),
                  pl.BlockSpec((1, 1, t, ka), lambda b, g, i: (b, g, 0, 0)),
                  pl.BlockSpec((1, 1, t, LANES), lambda b, g, i: (b, g, 0, 0)),
                  pl.BlockSpec((1, tq, GATE_PAD), lambda b, g, i: (b, i, ZF_GATE))],
        out_specs=pl.BlockSpec((1, tq, gw), lambda b, g, i: (b, i, g)),
        scratch_shapes=[pltpu.VMEM((rows, ka), BF16), pltpu.VMEM((rows, LANES), F32),
                        pltpu.VMEM((rows, LANES), F32),
                        pltpu.VMEM((rows, tk), F32), pltpu.VMEM((rows, tk), F32)],
        compiler_params=_cparams(("arbitrary", "arbitrary", "arbitrary")),
        name="nsa_selected",
    )(q, sel, kaug, vaug, glog)


def _win_attn_kernel(q_ref, k_ref, v_ref, gl_ref, o_ref, *, tq, hpg, span):
    g = pl.program_id(1)
    qi = pl.program_id(2)
    q = q_ref[0]
    q2 = jnp.concatenate([q[:, hh * HEAD_DIM:(hh + 1) * HEAD_DIM] for hh in range(hpg)], axis=0).astype(BF16)
    base = pl.multiple_of(jnp.maximum(qi * tq - WINDOW, 0), tq)
    kblk = k_ref[0, 0, pl.ds(base, span), :]
    vblk = v_ref[0, 0, pl.ds(base, span), :]
    rows = hpg * tq
    t = qi * tq + _mod_pow2(lax.broadcasted_iota(I32, (rows, span), 0), tq)
    wpos = base + lax.broadcasted_iota(I32, (rows, span), 1)
    mask = (wpos <= t) & (wpos > t - WINDOW)
    s = jnp.where(mask, _dot_nt(q2, kblk), NEG_INF)
    p = jnp.where(mask, jnp.exp(s - jnp.max(s, axis=1, keepdims=True)), 0.0)
    p = p / jnp.sum(p, axis=1, keepdims=True)
    out = _dot(p.astype(BF16), vblk)
    gates = jax.nn.sigmoid(gl_ref[0])
    o_ref[0] = jnp.concatenate(
        [out[hh * tq:(hh + 1) * tq] * _gate_col(gates, g * hpg + hh, 2) for hh in range(hpg)], axis=1)


def _win_attn(q, kw, vw, glog, tq):
    bsz, t, wq = q.shape
    ngrp = kw.shape[1]
    hpg = wq // HEAD_DIM // ngrp
    gw = hpg * HEAD_DIM
    span = WINDOW + tq
    assert WINDOW % tq == 0 and t >= span
    return pl.pallas_call(
        functools.partial(_win_attn_kernel, tq=tq, hpg=hpg, span=span),
        out_shape=jax.ShapeDtypeStruct((bsz, t, wq), F32),
        grid=(bsz, ngrp, t // tq),
        in_specs=[pl.BlockSpec((1, tq, gw), lambda b, g, i: (b, i, g)),
                  pl.BlockSpec((1, 1, t, HEAD_DIM), lambda b, g, i: (b, g, 0, 0)),
                  pl.BlockSpec((1, 1, t, HEAD_DIM), lambda b, g, i: (b, g, 0, 0)),
                  pl.BlockSpec((1, tq, GATE_PAD), lambda b, g, i: (b, i, ZF_GATE))],
        out_specs=pl.BlockSpec((1, tq, gw), lambda b, g, i: (b, i, g)),
        compiler_params=_cparams(("arbitrary", "arbitrary", "arbitrary")),
        name="nsa_window",
    )(q, kw, vw, glog)


def _out_kernel(h_ref, yc_ref, ys_ref, yb_ref, oc_ref, os_ref, ow_ref, mod_ref, w_ref, o_ref):
    y = jnp.concatenate([yc_ref[0], ys_ref[0], yb_ref[0], oc_ref[0] + os_ref[0] + ow_ref[0]], axis=1)
    o_ref[0] = h_ref[0] + mod_ref[0, 2:3, :] * _dot(y.astype(BF16), w_ref[...])


def _out_proj(h, parts, mod, w_bf16, tm):
    bsz, t, d = h.shape
    wp = parts[0].shape[-1]
    tok = lambda b, i: (b, i, 0)
    return pl.pallas_call(
        _out_kernel,
        out_shape=jax.ShapeDtypeStruct((bsz, t, d), F32),
        grid=(bsz, t // tm),
        in_specs=[pl.BlockSpec((1, tm, d), tok)] + [pl.BlockSpec((1, tm, wp), tok)] * 6
        + [pl.BlockSpec((1, 6, d), lambda b, i: (b, 0, 0)),
           pl.BlockSpec((d, d), lambda b, i: (0, 0))],
        out_specs=pl.BlockSpec((1, tm, d), tok),
        compiler_params=_cparams(("arbitrary", "arbitrary")),
        name="out_proj",
    )(h, *parts, mod, w_bf16)


def _ffn_kernel(h_ref, g_ref, mod_ref, wg_ref, wu_ref, wd_ref, o_ref, a_s, acc):
    f = pl.program_id(2)

    @pl.when(f == 0)
    def _():
        x = h_ref[0]
        y = x * _rms_scale(x) * g_ref[...]
        a_s[...] = (y * (1.0 + mod_ref[0, 4:5, :]) + mod_ref[0, 3:4, :]).astype(BF16)
        acc[...] = jnp.zeros_like(acc)

    a = a_s[...]
    hid = _silu(_dot(a, wg_ref[...])) * _dot(a, wu_ref[...])
    acc[...] += _dot(hid.astype(BF16), wd_ref[...])

    @pl.when(f == pl.num_programs(2) - 1)
    def _():
        o_ref[0] = h_ref[0] + mod_ref[0, 5:6, :] * acc[...]


def _dense_ffn(h, g, mod, wg, wu, wd, tm, tf):
    bsz, t, d = h.shape
    f = wg.shape[1]
    tok = lambda b, i, j: (b, i, 0)
    return pl.pallas_call(
        _ffn_kernel,
        out_shape=jax.ShapeDtypeStruct((bsz, t, d), F32),
        grid=(bsz, t // tm, f // tf),
        in_specs=[pl.BlockSpec((1, tm, d), tok),
                  pl.BlockSpec((1, d), lambda b, i, j: (0, 0)),
                  pl.BlockSpec((1, 6, d), lambda b, i, j: (b, 0, 0)),
                  pl.BlockSpec((d, tf), lambda b, i, j: (0, j)),
                  pl.BlockSpec((d, tf), lambda b, i, j: (0, j)),
                  pl.BlockSpec((tf, d), lambda b, i, j: (j, 0))],
        out_specs=pl.BlockSpec((1, tm, d), tok),
        scratch_shapes=[pltpu.VMEM((tm, d), BF16), pltpu.VMEM((tm, d), F32)],
        compiler_params=_cparams(("arbitrary", "arbitrary", "arbitrary")),
        name="dense_swiglu",
    )(h, g[None, :], mod, wg, wu, wd)


MOE_CHUNK = 256


def _moe_route_kernel(h_ref, g_ref, mod_ref, wrt_ref, a_ref, gates_ref, rank_ref, cnt_ref, *, tm):
    x = h_ref[0]
    y = x * _rms_scale(x) * g_ref[...]
    a = y * (1.0 + mod_ref[0, 4:5, :]) + mod_ref[0, 3:4, :]
    a_ref[0] = a.astype(BF16)
    logits = _dot3_nt(wrt_ref[...], a)
    row = lax.broadcasted_iota(I32, logits.shape, 0)
    v1 = jnp.max(logits, axis=0, keepdims=True)
    i1 = jnp.min(jnp.where(logits == v1, row, N_EXPERTS), axis=0, keepdims=True)
    rest = jnp.where(row == i1, -jnp.inf, logits)
    v2 = jnp.max(rest, axis=0, keepdims=True)
    i2 = jnp.min(jnp.where(rest == v2, row, N_EXPERTS), axis=0, keepdims=True)
    e2 = jnp.exp(v2 - v1)
    gates_ref[0] = jnp.where(row == i1, 1.0 / (1.0 + e2), 0.0) + jnp.where(row == i2, e2 / (1.0 + e2), 0.0)
    routed = jnp.where(row == i1, 1.0, jnp.where(row == i2, 1.0, 0.0))
    si = lax.broadcasted_iota(I32, (MOE_CHUNK, MOE_CHUNK), 0)
    sj = lax.broadcasted_iota(I32, (MOE_CHUNK, MOE_CHUNK), 1)
    before = jnp.where(si < sj, 1.0, 0.0).astype(BF16)
    carry = jnp.zeros((N_EXPERTS, 1), F32)
    parts = []
    for c in range(tm // MOE_CHUNK):
        rc = routed[:, c * MOE_CHUNK:(c + 1) * MOE_CHUNK]
        parts.append(_dot(rc.astype(BF16), before) + carry)
        carry = carry + jnp.sum(rc, axis=1, keepdims=True)
    rank = jnp.concatenate(parts, axis=1)
    rank_ref[0] = jnp.where(routed > 0.0, rank, -1.0)
    cnt_ref[0] = jnp.broadcast_to(carry, (N_EXPERTS, LANES)).astype(I32)


def _moe_expert_kernel(cnt_ref, a_ref, rank_t_ref, rank_c_ref, gate_c_ref, wg_ref, wu_ref, wd_ref, o_ref,
                       xc, yacc, *, tm, ne):
    i = pl.program_id(0)
    e = pl.program_id(1)
    f = pl.program_id(2)
    nch = (cnt_ref[i * ne + e] + (MOE_CHUNK - 1)) // MOE_CHUNK
    d = o_ref.shape[-1]

    @pl.when(jnp.logical_and(e == 0, f == 0))
    def _():
        o_ref[...] = jnp.zeros_like(o_ref)

    @pl.when(f == 0)
    def _():
        rrow = rank_t_ref[0, pl.ds(e, 1), :]

        def gather(c, carry):
            r0 = pl.multiple_of(c * MOE_CHUNK, MOE_CHUNK)
            tgt = (r0 + lax.broadcasted_iota(I32, (MOE_CHUNK, tm), 0)).astype(F32)
            pick = jnp.where(rrow == tgt, 1.0, 0.0).astype(BF16)
            xc[pl.ds(r0, MOE_CHUNK), :] = _dot(pick, a_ref[...]).astype(BF16)
            yacc[pl.ds(r0, MOE_CHUNK), :] = jnp.zeros((MOE_CHUNK, d), F32)
            return carry

        lax.fori_loop(0, nch, gather, 0)

    def ffn(c, carry):
        r0 = pl.multiple_of(c * MOE_CHUNK, MOE_CHUNK)
        x = xc[pl.ds(r0, MOE_CHUNK), :]
        hid = _silu(_dot(x, wg_ref[0])) * _dot(x, wu_ref[0])
        yacc[pl.ds(r0, MOE_CHUNK), :] += _dot(hid.astype(BF16), wd_ref[0])
        return carry

    lax.fori_loop(0, nch, ffn, 0)

    @pl.when(f == pl.num_programs(2) - 1)
    def _():
        lane = lax.broadcasted_iota(I32, (tm, ne), 1)
        rcol = jnp.sum(jnp.where(lane == e, rank_c_ref[0], 0.0), axis=1, keepdims=True)
        gcol = jnp.sum(jnp.where(lane == e, gate_c_ref[0], 0.0), axis=1, keepdims=True)

        def scatter(c, carry):
            r0 = pl.multiple_of(c * MOE_CHUNK, MOE_CHUNK)
            tgt = (r0 + lax.broadcasted_iota(I32, (tm, MOE_CHUNK), 1)).astype(F32)
            place = jnp.where(rcol == tgt, 1.0, 0.0).astype(BF16)
            o_ref[...] += gcol * _dot(place, yacc[pl.ds(r0, MOE_CHUNK), :].astype(BF16))
            return carry

        lax.fori_loop(0, nch, scatter, 0)


def _residual_kernel(h_ref, y_ref, mod_ref, o_ref):
    o_ref[0] = h_ref[0] + mod_ref[0, 5:6, :] * y_ref[0]


def _moe_ffn(h, g, mod, router, wg, wu, wd, tm, tf):
    bsz, t, d = h.shape
    ne, _, f = wg.shape
    assert ne == N_EXPERTS and tm % MOE_CHUNK == 0
    nt = t // tm
    tok = lambda b, i: (b, i, 0)
    a, gates_t, rank_t, cnt = pl.pallas_call(
        functools.partial(_moe_route_kernel, tm=tm),
        out_shape=(jax.ShapeDtypeStruct((bsz, t, d), BF16),
                   jax.ShapeDtypeStruct((bsz * nt, ne, tm), F32),
                   jax.ShapeDtypeStruct((bsz * nt, ne, tm), F32),
                   jax.ShapeDtypeStruct((bsz * nt, ne, LANES), I32)),
        grid=(bsz, nt),
        in_specs=[pl.BlockSpec((1, tm, d), tok),
                  pl.BlockSpec((1, d), lambda b, i: (0, 0)),
                  pl.BlockSpec((1, 6, d), lambda b, i: (b, 0, 0)),
                  pl.BlockSpec((ne, d), lambda b, i: (0, 0))],
        out_specs=(pl.BlockSpec((1, tm, d), tok),
                   pl.BlockSpec((1, ne, tm), lambda b, i: (b * nt + i, 0, 0)),
                   pl.BlockSpec((1, ne, tm), lambda b, i: (b * nt + i, 0, 0)),
                   pl.BlockSpec((1, ne, LANES), lambda b, i: (b * nt + i, 0, 0))),
        compiler_params=_cparams(("arbitrary", "arbitrary")),
        name="moe_route",
    )(h, g[None, :], mod, router.T)
    counts = cnt[:, :, 0].reshape(-1)
    y = pl.pallas_call(
        functools.partial(_moe_expert_kernel, tm=tm, ne=ne),
        out_shape=jax.ShapeDtypeStruct((bsz * t, d), F32),
        grid_spec=pltpu.PrefetchScalarGridSpec(
            num_scalar_prefetch=1,
            grid=(bsz * nt, ne, f // tf),
            in_specs=[pl.BlockSpec((tm, d), lambda i, e, j, cnt: (i, 0)),
                      pl.BlockSpec((1, ne, tm), lambda i, e, j, cnt: (i, 0, 0)),
                      pl.BlockSpec((1, tm, ne), lambda i, e, j, cnt: (i, 0, 0)),
                      pl.BlockSpec((1, tm, ne), lambda i, e, j, cnt: (i, 0, 0)),
                      pl.BlockSpec((1, d, tf), lambda i, e, j, cnt: (e, 0, j)),
                      pl.BlockSpec((1, d, tf), lambda i, e, j, cnt: (e, 0, j)),
                      pl.BlockSpec((1, tf, d), lambda i, e, j, cnt: (e, j, 0))],
            out_specs=pl.BlockSpec((tm, d), lambda i, e, j, cnt: (i, 0)),
            scratch_shapes=[pltpu.VMEM((tm, d), BF16), pltpu.VMEM((tm, d), F32)]),
        compiler_params=_cparams(("arbitrary", "arbitrary", "arbitrary")),
        name="moe_experts",
    )(counts, a.reshape(bsz * t, d), rank_t, rank_t.transpose(0, 2, 1), gates_t.transpose(0, 2, 1), wg, wu, wd)
    tr = 512
    return pl.pallas_call(
        _residual_kernel,
        out_shape=jax.ShapeDtypeStruct((bsz, t, d), F32),
        grid=(bsz, t // tr),
        in_specs=[pl.BlockSpec((1, tr, d), tok), pl.BlockSpec((1, tr, d), tok),
                  pl.BlockSpec((1, 6, d), lambda b, i: (b, 0, 0))],
        out_specs=pl.BlockSpec((1, tr, d), tok),
        compiler_params=_cparams(("arbitrary", "arbitrary")),
        name="moe_residual",
    )(h, y.reshape(bsz, t, d), mod)


def _pack_w_in(w_in, dq):
    o1 = 3 * dq
    o2 = o1 + dq
    o3 = o2 + 3 * dq
    sb_q = w_in[:, o2:o2 + dq] * (HEAD_DIM ** -0.5)
    pad = jnp.zeros((w_in.shape[0], GATE_PAD - (w_in.shape[1] - (o3 + dq + 6 * NSA_KV_HEADS * HEAD_DIM))),
                    w_in.dtype)
    cols = [w_in[:, :o1], sb_q, w_in[:, o2 + dq:o3], w_in[:, o1:o2], w_in[:, o3:], pad]
    return jnp.concatenate(cols, axis=1).astype(BF16), o1 + 3 * dq


def _token_mixing(h, mod, positions, g_mix, w_in, w_out, conv_w, ssm, nsa):
    bsz, t, d = h.shape
    dq = d // 4
    w_packed, nb = _pack_w_in(w_in, dq)
    zb, zf = _in_proj(h, g_mix, mod, w_packed, nb, tm=512)
    y_conv = _short_conv(zb, conv_w, tt=512)
    y_ssm = _s5(zf, *ssm, tt=256)
    y_sb = _stick_breaking(zb, dq, tq=256)
    o_c, o_s, o_w = _nsa_mixer(zf, dq, positions, nsa)
    return _out_proj(h, [y_conv, y_ssm, y_sb, o_c, o_s, o_w], mod, w_out.astype(BF16), tm=512)


def _nsa_mixer(zf, wq, positions, nsa):
    bsz, t, _ = zf.shape
    ngrp = NSA_KV_HEADS
    gw = ngrp * HEAD_DIM
    q_norm_g, k_norm_g, pos_k, pos_v, k_w1, k_w2, v_w1, v_w2 = nsa
    half = HEAD_DIM // 2
    inv_freq = jnp.power(jnp.float32(ROPE_THETA), -jnp.arange(half, dtype=F32) / half)
    ang = positions.astype(F32)[..., None] * inv_freq
    ang_tok = jnp.tile(ang, (1, 1, LANES // half))
    q_rot, kaug, vaug, kwin, vwin = _nsa_prep(zf, ang_tok, q_norm_g, k_norm_g, wq, tm=512)

    nch = t // CMP_STRIDE

    def chunks(blk):
        x = zf[..., blk * gw:(blk + 1) * gw]
        return x.reshape(bsz, nch, CMP_STRIDE, ngrp, HEAD_DIM).transpose(0, 3, 1, 2, 4).reshape(
            bsz, ngrp, nch, CMP_STRIDE * HEAD_DIM)

    end_idx = jnp.minimum(jnp.arange(nch) * CMP_STRIDE + CMP_LEN - 1, t - 1)
    angc = jnp.tile(positions[:, end_idx].astype(F32)[..., None] * inv_freq, (1, 1, 2))
    kc, vc = _compress(chunks(ZF_KCMP), chunks(ZF_VCMP), k_w1, k_w2, v_w1, v_w2, pos_k, pos_v, k_norm_g, angc)
    o_c, sel = _cmp_attn(q_rot, kc, vc, zf, tq=256)
    o_s = _sel_attn(q_rot, sel, kaug, vaug, zf, tq=256, tk=512)
    o_w = _win_attn(q_rot, kwin, vwin, zf, tq=256)
    return o_c, o_s, o_w


def kernel(x, c, positions, ada_w, ada_b, norm_mix_g, norm_ffn_g, w_in, w_out, conv_w, ssm_lam_re, ssm_lam_im, ssm_b_re, ssm_b_im, ssm_c_re, ssm_c_im, ssm_d, ssm_log_dt, ssm_w_glu, nsa_q_norm_g, nsa_k_norm_g, cmp_pos_k, cmp_pos_v, cmp_k_w1, cmp_k_w2, cmp_v_w1, cmp_v_w2, ffn_w_gate, ffn_w_up, ffn_w_down, moe_router, moe_w_gate, moe_w_up, moe_w_down):
    depth = ada_w.shape[0]
    mods = _modulation(c, ada_w, ada_b)
    h = x
    for layer in range(depth):
        mod = mods[layer]
        ssm = (ssm_lam_re[layer], ssm_lam_im[layer], ssm_b_re[layer], ssm_b_im[layer], ssm_c_re[layer],
               ssm_c_im[layer], ssm_d[layer], ssm_log_dt[layer], ssm_w_glu[layer])
        nsa = (nsa_q_norm_g[layer], nsa_k_norm_g[layer], cmp_pos_k[layer], cmp_pos_v[layer],
               cmp_k_w1[layer], cmp_k_w2[layer], cmp_v_w1[layer], cmp_v_w2[layer])
        h = _token_mixing(h, mod, positions, norm_mix_g[layer], w_in[layer], w_out[layer], conv_w[layer],
                          ssm, nsa)
        i = layer // 2
        if layer % 2 == 0:
            h = _dense_ffn(h, norm_ffn_g[layer], mod, ffn_w_gate[i].astype(BF16), ffn_w_up[i].astype(BF16),
                           ffn_w_down[i].astype(BF16), tm=1024, tf=512)
        else:
            h = _moe_ffn(h, norm_ffn_g[layer], mod, moe_router[i], moe_w_gate[i].astype(BF16),
                         moe_w_up[i].astype(BF16), moe_w_down[i].astype(BF16), tm=min(2048, h.shape[1]),
                         tf=512)
    return h
```

```python
import functools
import math
from typing import NamedTuple

import jax
import jax.numpy as jnp
from jax import lax
from jax.experimental import pallas as pl
from jax.experimental.pallas import tpu as pltpu

F32 = jnp.float32
BF16 = jnp.bfloat16
I32 = jnp.int32

HEAD_DIM = 64
CONV_WIDTH = 3
SSM_GROUP = 16
SSM_STATE = 64
SSM_MAX_RE = -1e-4
NSA_KV_HEADS = 2
CMP_LEN = 32
CMP_STRIDE = 16
SLC_BLOCK = 64
SLC_TOPK = 16
WINDOW = 512
FORCE_BONUS = 1e4
NEG_INF = -1e30
ROPE_THETA = 10000.0
RMS_EPS = 1e-6
N_EXPERTS = 8

LANES = 128
SUBLANES = 8
VMEM_LIMIT = 56 * 1024 * 1024
SEL_MASK_BIAS = -30000.0
SB_SKIP_LOG = -110.0
GATE_PAD = LANES


def _cparams(sem):
    return pltpu.CompilerParams(dimension_semantics=sem, vmem_limit_bytes=VMEM_LIMIT)


def _dot(a, b):
    return jnp.dot(a, b, preferred_element_type=F32)


def _dot_nt(a, b):
    return lax.dot_general(a, b, (((1,), (1,)), ((), ())), preferred_element_type=F32)


def _split(x):
    hi = x.astype(BF16)
    lo = (x - hi.astype(F32)).astype(BF16)
    return hi, lo


def _dot3(a, b):
    ah, al = _split(a)
    bh, bl = _split(b)
    return _dot(ah, bh) + _dot(ah, bl) + _dot(al, bh)


def _dot3_nt(a, b):
    ah, al = _split(a)
    bh, bl = _split(b)
    return _dot_nt(ah, bh) + _dot_nt(ah, bl) + _dot_nt(al, bh)


def _dot2_exact_rhs(a, b_bf16):
    ah, al = _split(a)
    return _dot(ah, b_bf16) + _dot(al, b_bf16)


def _silu(x):
    return x * jax.nn.sigmoid(x)


def _div_pow2(x, n):
    return lax.shift_right_logical(x, jnp.int32(n.bit_length() - 1))


def _mod_pow2(x, n):
    return x & (n - 1)


def _rms_scale(x):
    return lax.rsqrt(jnp.mean(x * x, axis=-1, keepdims=True) + RMS_EPS)


def _mod_kernel(c_ref, w_ref, b_ref, o_ref):
    o_ref[0] = _dot3(_silu(c_ref[...]), w_ref[0]) + b_ref[0]


def _modulation(c, ada_w, ada_b):
    depth, d, n6 = ada_w.shape
    bsz = c.shape[0]
    rows = -(-bsz // SUBLANES) * SUBLANES
    cpad = jnp.zeros((rows, d), F32).at[:bsz].set(c)
    tn = n6 // 4
    out = pl.pallas_call(
        _mod_kernel,
        out_shape=jax.ShapeDtypeStruct((depth, rows, n6), F32),
        grid=(depth, n6 // tn),
        in_specs=[pl.BlockSpec((rows, d), lambda l, j: (0, 0)),
                  pl.BlockSpec((1, d, tn), lambda l, j: (l, 0, j)),
                  pl.BlockSpec((1, 1, tn), lambda l, j: (l, 0, j))],
        out_specs=pl.BlockSpec((1, rows, tn), lambda l, j: (l, 0, j)),
        compiler_params=_cparams(("arbitrary", "arbitrary")),
        name="adaln_mod",
    )(cpad, ada_w, ada_b[:, None, :])
    return out[:, :bsz].reshape(depth, bsz, 6, d)


def _in_kernel(h_ref, g_ref, mod_ref, w_ref, zb_ref, zs_ref, zf_ref):
    x = h_ref[0]
    y = x * _rms_scale(x) * g_ref[...]
    a = y * (1.0 + mod_ref[0, 1:2, :]) + mod_ref[0, 0:1, :]
    z = _dot(a.astype(BF16), w_ref[...])
    nb = zb_ref.shape[-1]
    ns = zs_ref.shape[-1]
    zb_ref[0] = z[:, :nb].astype(BF16)
    zs_ref[0] = z[:, nb:nb + ns]
    zf_ref[0] = z[:, nb + ns:]


def _in_proj(h, g, mod, w_bf16, nb, ns, tm):
    bsz, t, d = h.shape
    nz = w_bf16.shape[1]
    widths = (nb, ns, nz - nb - ns)
    return pl.pallas_call(
        _in_kernel,
        out_shape=tuple(jax.ShapeDtypeStruct((bsz, t, w), dt) for w, dt in zip(widths, (BF16, F32, F32))),
        grid=(bsz, t // tm),
        in_specs=[pl.BlockSpec((1, tm, d), lambda b, i: (b, i, 0)),
                  pl.BlockSpec((1, d), lambda b, i: (0, 0)),
                  pl.BlockSpec((1, 6, d), lambda b, i: (b, 0, 0)),
                  pl.BlockSpec((d, nz), lambda b, i: (0, 0))],
        out_specs=tuple(pl.BlockSpec((1, tm, w), lambda b, i: (b, i, 0)) for w in widths),
        compiler_params=_cparams(("arbitrary", "arbitrary")),
        name="in_proj",
    )(h, g[None, :], mod, w_bf16)


def _conv_kernel(z_ref, w_ref, o_ref, buf, *, tt, dc):
    @pl.when(pl.program_id(1) == 0)
    def _():
        buf[0:SUBLANES, :] = jnp.zeros((SUBLANES, dc), F32)

    z = z_ref[0].astype(F32)
    gate_b = z[:, :dc]
    v = z[:, dc:2 * dc] * z[:, 2 * dc:]
    buf[SUBLANES:, :] = v
    y = (w_ref[2:3, :] * v
         + w_ref[1:2, :] * buf[SUBLANES - 1:SUBLANES - 1 + tt, :]
         + w_ref[0:1, :] * buf[SUBLANES - 2:SUBLANES - 2 + tt, :])
    o_ref[0] = gate_b * y
    buf[0:SUBLANES, :] = v[tt - SUBLANES:, :]


def _short_conv(zb, conv_w, tt):
    bsz, t, _ = zb.shape
    dc = conv_w.shape[1]
    c3 = 3 * dc
    return pl.pallas_call(
        functools.partial(_conv_kernel, tt=tt, dc=dc),
        out_shape=jax.ShapeDtypeStruct((bsz, t, dc), F32),
        grid=(bsz, t // tt),
        in_specs=[pl.BlockSpec((1, tt, c3), lambda b, i: (b, i, 0)),
                  pl.BlockSpec((CONV_WIDTH, dc), lambda b, i: (0, 0))],
        out_specs=pl.BlockSpec((1, tt, dc), lambda b, i: (b, i, 0)),
        scratch_shapes=[pltpu.VMEM((tt + SUBLANES, dc), F32)],
        compiler_params=_cparams(("arbitrary", "arbitrary")),
        name="short_conv",
    )(zb, conv_w)


SSM_LANE_CHUNK = 2 * LANES


def _gelu_tanh(x):
    return 0.5 * x * (1.0 + jnp.tanh(math.sqrt(2.0 / math.pi) * (x + 0.044715 * (x * x * x))))


def _ssm_kernel(u_ref, tz_ref, we_ref, wd_ref, pre_ref, pim_ref, d_ref, o_ref,
                bre, bim, cin, cre, cim, *, tt, ns):
    @pl.when(pl.program_id(1) == 0)
    def _():
        cre[...] = jnp.zeros_like(cre)
        cim[...] = jnp.zeros_like(cim)

    u = u_ref[0]
    ub = u.astype(BF16)
    end = _dot(ub, we_ref[...])
    bre[...] = end[:, :ns]
    bim[...] = end[:, ns:]
    lw = SSM_LANE_CHUNK
    row = lax.broadcasted_iota(I32, (SUBLANES, lw), 0)

    def local_scan(i, carry):
        r0 = pl.multiple_of(i * SUBLANES, SUBLANES)
        for lg in range(ns // lw):
            sl = slice(lg * lw, (lg + 1) * lw)
            xr = bre[pl.ds(r0, SUBLANES), sl]
            xi = bim[pl.ds(r0, SUBLANES), sl]
            for k in (1, 2, 4):
                ar = pre_ref[k - 1:k, sl]
                ai = pim_ref[k - 1:k, sl]
                sr = jnp.where(row >= k, pltpu.roll(xr, k, 0), 0.0)
                si = jnp.where(row >= k, pltpu.roll(xi, k, 0), 0.0)
                xr, xi = xr + ar * sr - ai * si, xi + ar * si + ai * sr
            bre[pl.ds(r0, SUBLANES), sl] = xr
            bim[pl.ds(r0, SUBLANES), sl] = xi
        return carry

    lax.fori_loop(0, tt // SUBLANES, local_scan, 0, unroll=2)

    row_all = lax.broadcasted_iota(I32, (SUBLANES, ns), 0)

    def carry_in(i, carry):
        c_re, c_im = carry
        r0 = pl.multiple_of(i * SUBLANES, SUBLANES)
        pr = pre_ref[...]
        pi_ = pim_ref[...]
        xr = bre[pl.ds(r0, SUBLANES), :] + pr * c_re - pi_ * c_im
        xi = bim[pl.ds(r0, SUBLANES), :] + pr * c_im + pi_ * c_re
        cin[pl.ds(r0, SUBLANES), :ns] = jnp.where(row_all == 0, c_re, pltpu.roll(xr, 1, 0))
        cin[pl.ds(r0, SUBLANES), ns:] = jnp.where(row_all == 0, c_im, pltpu.roll(xi, 1, 0))
        return xr[SUBLANES - 1:], xi[SUBLANES - 1:]

    c_re, c_im = lax.fori_loop(0, tt // SUBLANES, carry_in, (cre[...], cim[...]))
    cre[...] = c_re
    cim[...] = c_im
    o_ref[0] = _dot(ub, tz_ref[...]) + _dot(cin[...].astype(BF16), wd_ref[...]) + d_ref[...] * u


SSM_CHUNK = 8


def _s5(u, lam_re, lam_im, b_re, b_im, c_re, c_im, d_skip, log_dt, tt):
    bsz, t, ds = u.shape
    g, p = lam_re.shape
    hc = SSM_GROUP
    ns = g * p
    nl = SSM_CHUNK
    lam = lax.complex(jnp.minimum(lam_re, SSM_MAX_RE), lam_im)
    dt = jnp.exp(log_dt)[:, None]
    lam_bar = jnp.exp(lam * dt)
    b_bar = ((lam_bar - 1.0) / lam)[..., None] * lax.complex(b_re, b_im)
    cmat = lax.complex(c_re, c_im)
    steps = jnp.arange(nl + 1, dtype=F32)[:, None, None]
    pw = jnp.exp(steps * (lam * dt)[None])
    in_grp = (jnp.arange(nl * ds) // hc) % g
    st_grp = jnp.arange(ns) // p
    rep = lambda n, m: (jnp.arange(m)[None, :] % n == jnp.arange(n)[:, None]).astype(F32)
    kern = jnp.einsum('gcp,tgp,gpd->tgdc', cmat, pw[:nl], b_bar).real.reshape(nl, ds, hc)
    zero = jnp.zeros((ds, hc), F32)
    rows_rc = jnp.concatenate(
        [jnp.concatenate([zero] * s + [kern[r - s] for r in range(s, nl)], axis=1) for s in range(nl)], axis=0)
    col_rc = (jnp.arange(nl * ds) // ds) * hc + jnp.arange(nl * ds) % hc
    spread_rc = (col_rc[None, :] == jnp.arange(nl * hc)[:, None]).astype(F32)
    tz = jnp.dot(rows_rc, spread_rc) * (in_grp[:, None] == in_grp[None, :])
    end = (pw[:nl][::-1][..., None] * b_bar[None]).transpose(0, 1, 3, 2).reshape(nl * ds, p)
    same_in_st = in_grp[:, None] == st_grp[None, :]
    we = jnp.concatenate([jnp.dot(part, rep(p, ns)) * same_in_st for part in (end.real, end.imag)], axis=1)
    dec = (cmat[None] * pw[1:, :, None, :]).transpose(3, 0, 1, 2).reshape(p, nl * ds)
    wd = jnp.concatenate([jnp.dot(rep(p, ns).T, part) * same_in_st.T for part in (dec.real, -dec.imag)], axis=0)
    pows = jnp.exp(jnp.arange(1, SUBLANES + 1, dtype=F32)[:, None, None] * nl * (lam * dt)[None])
    pows = pows.reshape(SUBLANES, ns)
    rows = t // nl
    const = lambda shape: pl.BlockSpec(shape, lambda b, i: (0, 0), pipeline_mode=pl.Buffered(1))
    y = pl.pallas_call(
        functools.partial(_ssm_kernel, tt=tt, ns=ns),
        out_shape=jax.ShapeDtypeStruct((bsz, rows, nl * ds), F32),
        grid=(bsz, rows // tt),
        in_specs=[pl.BlockSpec((1, tt, nl * ds), lambda b, i: (b, i, 0)),
                  const((nl * ds, nl * ds)), const((nl * ds, 2 * ns)), const((2 * ns, nl * ds)),
                  pl.BlockSpec((SUBLANES, ns), lambda b, i: (0, 0)),
                  pl.BlockSpec((SUBLANES, ns), lambda b, i: (0, 0)),
                  pl.BlockSpec((1, nl * ds), lambda b, i: (0, 0))],
        out_specs=pl.BlockSpec((1, tt, nl * ds), lambda b, i: (b, i, 0)),
        scratch_shapes=[pltpu.VMEM((tt, ns), F32), pltpu.VMEM((tt, ns), F32), pltpu.VMEM((tt, 2 * ns), F32),
                        pltpu.VMEM((1, ns), F32), pltpu.VMEM((1, ns), F32)],
        compiler_params=_cparams(("arbitrary", "arbitrary")),
        name="s5_scan",
    )(u.reshape(bsz, rows, nl * ds), tz.astype(BF16), we.astype(BF16), wd.astype(BF16), pows.real, pows.imag,
      jnp.tile(d_skip, nl)[None, :])
    return y.reshape(bsz, t, ds)


def _sb_kernel(q_ref, k_ref, v_ref, o_ref, acc, csum, *, tq, nh):
    qi = pl.program_id(1)
    acc[...] = jnp.zeros_like(acc)
    csum[...] = jnp.zeros_like(csum)
    ti = lax.broadcasted_iota(I32, (tq, tq), 0)
    ji = lax.broadcasted_iota(I32, (tq, tq), 1)
    later = (ti > ji).astype(BF16)

    def cond(state):
        kb, cmax = state
        return jnp.logical_and(kb >= 0, cmax > SB_SKIP_LOG)

    def block(kb, diagonal):
        k0 = pl.multiple_of(kb * tq, tq)
        past = ji < ti
        keep = (lambda x: jnp.where(past, x, 0.0)) if diagonal else (lambda x: x)
        heads = [slice(h * HEAD_DIM, (h + 1) * HEAD_DIM) for h in range(nh)]
        zs = [_dot_nt(q_ref[0, :, hs], k_ref[0, pl.ds(k0, tq), hs]) for hs in heads]
        log_betas, log_keeps, inners = [], [], []
        for z in zs:
            log_beta = jnp.minimum(z, 0.0) - jnp.log(1.0 + jnp.exp(-jnp.abs(z)))
            log_keep = keep(log_beta - z)
            hi, lo = _split(log_keep)
            log_betas.append(log_beta)
            log_keeps.append(log_keep)
            inners.append(_dot(hi, later) + _dot(lo, later))
        cmax = jnp.float32(-jnp.inf)
        for h, hs in enumerate(heads):
            c = csum[h]
            w = keep(jnp.exp(log_betas[h] + inners[h] + c[:, :1]))
            acc[:, hs] += _dot(w.astype(BF16), v_ref[0, pl.ds(k0, tq), hs])
            c = c + jnp.sum(log_keeps[h], axis=1, keepdims=True)
            csum[h] = c
            cmax = jnp.maximum(cmax, jnp.max(c))
        return cmax

    def body(state):
        kb, _ = state
        return kb - 1, block(kb, False)

    lax.while_loop(cond, body, (qi - 1, block(qi, True)))
    o_ref[0] = acc[...]


SB_Q_BLOCK, SB_K_BLOCK, SB_V_BLOCK = 3, 4, 5


def _stick_breaking(zb, dq, tq):
    bsz, t, _ = zb.shape
    nh = dq // HEAD_DIM
    return pl.pallas_call(
        functools.partial(_sb_kernel, tq=tq, nh=nh),
        out_shape=jax.ShapeDtypeStruct((bsz, t, dq), F32),
        grid=(bsz, t // tq),
        in_specs=[pl.BlockSpec((1, tq, dq), lambda b, i: (b, i, SB_Q_BLOCK)),
                  pl.BlockSpec((1, t, dq), lambda b, i: (b, 0, SB_K_BLOCK)),
                  pl.BlockSpec((1, t, dq), lambda b, i: (b, 0, SB_V_BLOCK))],
        out_specs=pl.BlockSpec((1, tq, dq), lambda b, i: (b, i, 0)),
        scratch_shapes=[pltpu.VMEM((tq, dq), F32), pltpu.VMEM((nh, tq, LANES), F32)],
        compiler_params=_cparams(("arbitrary", "arbitrary")),
        name="stick_breaking",
    )(zb, zb, zb)


def _head_norm(x, gsum_ref, gain):
    ss = _dot2_exact_rhs(x * x, gsum_ref[...]) * (1.0 / HEAD_DIM)
    return x * lax.rsqrt(ss + RMS_EPS) * gain


def _rope_lanes(x, cos, sin):
    w = x.shape[-1]
    lane = lax.broadcasted_iota(I32, x.shape, 1)
    first = _mod_pow2(lane, HEAD_DIM) < (HEAD_DIM // 2)
    partner = jnp.where(first, -pltpu.roll(x, w - HEAD_DIM // 2, 1), pltpu.roll(x, HEAD_DIM // 2, 1))
    return x * cos + partner * sin


def _nsa_prep_kernel(zq_ref, zkc_ref, zvc_ref, zks_ref, zvs_ref, zkw_ref, zvw_ref, cos_ref, sin_ref, gq_ref,
                     gk_ref, gsq_ref, gsk_ref, q_ref, ck_ref, cv_ref, kaug_ref, vaug_ref, kw_ref, vw_ref,
                     *, tm, ngrp, nblk):
    cos1 = cos_ref[0]
    sin1 = sin_ref[0]
    wq = zq_ref.shape[-1]
    cosq = jnp.concatenate([cos1] * (wq // LANES), axis=1)
    sinq = jnp.concatenate([sin1] * (wq // LANES), axis=1)
    q = _rope_lanes(_head_norm(zq_ref[0], gsq_ref, gq_ref[...]), cosq, sinq)
    q_ref[0] = q * (HEAD_DIM ** -0.5)
    ks = _rope_lanes(_head_norm(zks_ref[0], gsk_ref, gk_ref[...]), cos1, sin1)
    kw = _rope_lanes(_head_norm(zkw_ref[0], gsk_ref, gk_ref[...]), cos1, sin1)
    t0 = pl.program_id(1) * tm
    tok = t0 + lax.broadcasted_iota(I32, (tm, nblk), 0)
    blk = lax.broadcasted_iota(I32, (tm, nblk), 1)
    onehot = jnp.where(_div_pow2(tok, SLC_BLOCK) == blk, 1.0, 0.0).astype(BF16)
    vs = zvs_ref[0]
    vw = zvw_ref[0]
    lane = lax.broadcasted_iota(I32, (tm, LANES - HEAD_DIM), 1)
    ones_col = jnp.where(lane == 0, 1.0, 0.0).astype(BF16)
    for g in range(ngrp):
        sl = slice(g * HEAD_DIM, (g + 1) * HEAD_DIM)
        kaug_ref[0, g] = jnp.concatenate([onehot, ks[:, sl].astype(BF16)], axis=1)
        vaug_ref[0, g] = jnp.concatenate([vs[:, sl].astype(BF16), ones_col], axis=1)
        kw_ref[0, g] = kw[:, sl].astype(BF16)
        vw_ref[0, g] = jnp.concatenate([vw[:, sl].astype(BF16), ones_col], axis=1)
    for j in range(CMP_STRIDE):
        rows = pl.ds(j, tm // CMP_STRIDE, stride=CMP_STRIDE)
        js = slice(j * HEAD_DIM, (j + 1) * HEAD_DIM)
        kj = zkc_ref[0, rows, :]
        vj = zvc_ref[0, rows, :]
        for g in range(ngrp):
            sl = slice(g * HEAD_DIM, (g + 1) * HEAD_DIM)
            ck_ref[0, g, :, js] = kj[:, sl]
            cv_ref[0, g, :, js] = vj[:, sl]


ZF_Q_BLOCK = 0
ZF_KCMP, ZF_VCMP, ZF_KSLC, ZF_VSLC, ZF_KWIN, ZF_VWIN, ZF_GATE = 2, 3, 4, 5, 6, 7, 8


def _rope_kernel(ang_ref, cos_ref, sin_ref):
    ang = ang_ref[...]
    cos_ref[...] = jnp.cos(ang)
    sin_ref[...] = jnp.sin(ang)


def _rope_tables(positions):
    bsz, t = positions.shape
    half = HEAD_DIM // 2
    inv_freq = jnp.power(jnp.float32(ROPE_THETA), -jnp.arange(half, dtype=F32) / half)
    ang = (positions.astype(F32)[..., None] * inv_freq).reshape(bsz, t * half // LANES, LANES)
    rows = ang.shape[1]
    spec = pl.BlockSpec((1, rows, LANES), lambda b: (b, 0, 0))
    cos, sin = pl.pallas_call(
        _rope_kernel,
        out_shape=(jax.ShapeDtypeStruct(ang.shape, F32), jax.ShapeDtypeStruct(ang.shape, F32)),
        grid=(bsz,),
        in_specs=[spec],
        out_specs=(spec, spec),
        compiler_params=_cparams(("arbitrary",)),
        name="rope_tables",
    )(ang)
    widen = lambda x: jnp.tile(x.reshape(bsz, t, half), (1, 1, LANES // half))
    return widen(cos), widen(sin)


def _nsa_prep(zf, rope, gq, gk, wq, tm):
    bsz, t, _ = zf.shape
    cos, sin = rope
    ngrp = NSA_KV_HEADS
    nblk = t // SLC_BLOCK
    wk = ngrp * HEAD_DIM
    assert wk == LANES and wq == 2 * wk
    lane_q = jnp.arange(wq) // HEAD_DIM
    gsq = (lane_q[:, None] == lane_q[None, :]).astype(BF16)
    lane_k = jnp.arange(wk) // HEAD_DIM
    gsk = (lane_k[:, None] == lane_k[None, :]).astype(BF16)
    kvspec = lambda blk: pl.BlockSpec((1, tm, wk), lambda b, i: (b, i, blk))
    hspec = lambda w: pl.BlockSpec((1, ngrp, tm, w), lambda b, i: (b, 0, i, 0))
    cspec = pl.BlockSpec((1, ngrp, tm // CMP_STRIDE, CMP_STRIDE * HEAD_DIM), lambda b, i: (b, 0, i, 0))
    return pl.pallas_call(
        functools.partial(_nsa_prep_kernel, tm=tm, ngrp=ngrp, nblk=nblk),
        out_shape=(jax.ShapeDtypeStruct((bsz, t, wq), F32),
                   jax.ShapeDtypeStruct((bsz, ngrp, t // CMP_STRIDE, CMP_STRIDE * HEAD_DIM), F32),
                   jax.ShapeDtypeStruct((bsz, ngrp, t // CMP_STRIDE, CMP_STRIDE * HEAD_DIM), F32),
                   jax.ShapeDtypeStruct((bsz, ngrp, t, HEAD_DIM + nblk), BF16),
                   jax.ShapeDtypeStruct((bsz, ngrp, t, LANES), BF16),
                   jax.ShapeDtypeStruct((bsz, ngrp, t, HEAD_DIM), BF16),
                   jax.ShapeDtypeStruct((bsz, ngrp, t, LANES), BF16)),
        grid=(bsz, t // tm),
        in_specs=[pl.BlockSpec((1, tm, wq), lambda b, i: (b, i, ZF_Q_BLOCK)),
                  kvspec(ZF_KCMP), kvspec(ZF_VCMP),
                  kvspec(ZF_KSLC), kvspec(ZF_VSLC), kvspec(ZF_KWIN), kvspec(ZF_VWIN),
                  pl.BlockSpec((1, tm, LANES), lambda b, i: (b, i, 0)),
                  pl.BlockSpec((1, tm, LANES), lambda b, i: (b, i, 0)),
                  pl.BlockSpec((1, wq), lambda b, i: (0, 0)),
                  pl.BlockSpec((1, wk), lambda b, i: (0, 0)),
                  pl.BlockSpec((wq, wq), lambda b, i: (0, 0)),
                  pl.BlockSpec((wk, wk), lambda b, i: (0, 0))],
        out_specs=(pl.BlockSpec((1, tm, wq), lambda b, i: (b, i, 0)),
                   cspec, cspec,
                   hspec(HEAD_DIM + nblk), hspec(LANES), hspec(HEAD_DIM), hspec(LANES)),
        compiler_params=_cparams(("arbitrary", "arbitrary")),
        name="nsa_prep",
    )(zf, zf, zf, zf, zf, zf, zf, cos, sin, jnp.tile(gq, wq // HEAD_DIM)[None, :], jnp.tile(gk, ngrp)[None, :],
      gsq, gsk)


def _compress_kernel(ck_ref, cv_ref, w1k_ref, w2k_ref, w1v_ref, w2v_ref, pek_ref, pev_ref,
                     gk_ref, angc_ref, kc_ref, vc_ref, *, nch, half):
    def mlp(c_ref, w1_ref, w2_ref, pe_ref):
        x = c_ref[0, 0]
        w1 = w1_ref[...]
        first = _dot3(x, w1[:half])
        second = _dot3(x, w1[half:])
        bias = _dot3(pe_ref[...], w1)[0:1]
        hid = _silu(first + pltpu.roll(second, nch - 1, 0) + bias)
        return _dot3(hid, w2_ref[...])

    kc = mlp(ck_ref, w1k_ref, w2k_ref, pek_ref)
    kc = kc * _rms_scale(kc) * gk_ref[...]
    ang = angc_ref[0]
    half_d = HEAD_DIM // 2
    k1 = kc[:, :half_d]
    k2 = kc[:, half_d:]
    cos = jnp.cos(ang[:, :half_d])
    sin = jnp.sin(ang[:, :half_d])
    kc_ref[0, 0] = jnp.concatenate([k1 * cos - k2 * sin, k2 * cos + k1 * sin], axis=1)
    vc_ref[0, 0] = mlp(cv_ref, w1v_ref, w2v_ref, pev_ref)


def _compress(ck, cv, w1k, w2k, w1v, w2v, pos_k, pos_v, gk, angc):
    bsz, ngrp, nch, half = ck.shape
    hid = w1k.shape[1]

    def pe_rows(pe):
        return jnp.zeros((SUBLANES, 2 * half), F32).at[0].set(pe.reshape(-1))

    blk4 = lambda b, g: (b, g, 0, 0)
    full2 = lambda b, g: (0, 0)
    return pl.pallas_call(
        functools.partial(_compress_kernel, nch=nch, half=half),
        out_shape=(jax.ShapeDtypeStruct((bsz, ngrp, nch, HEAD_DIM), F32),
                   jax.ShapeDtypeStruct((bsz, ngrp, nch, HEAD_DIM), F32)),
        grid=(bsz, ngrp),
        in_specs=[pl.BlockSpec((1, 1, nch, half), blk4),
                  pl.BlockSpec((1, 1, nch, half), blk4),
                  pl.BlockSpec((2 * half, hid), full2),
                  pl.BlockSpec((hid, HEAD_DIM), full2),
                  pl.BlockSpec((2 * half, hid), full2),
                  pl.BlockSpec((hid, HEAD_DIM), full2),
                  pl.BlockSpec((SUBLANES, 2 * half), full2),
                  pl.BlockSpec((SUBLANES, 2 * half), full2),
                  pl.BlockSpec((1, HEAD_DIM), full2),
                  pl.BlockSpec((1, nch, HEAD_DIM), lambda b, g: (b, 0, 0))],
        out_specs=(pl.BlockSpec((1, 1, nch, HEAD_DIM), blk4),
                   pl.BlockSpec((1, 1, nch, HEAD_DIM), blk4)),
        compiler_params=_cparams(("arbitrary", "arbitrary")),
        name="nsa_compress",
    )(ck, cv, w1k, w2k, w1v, w2v, pe_rows(pos_k), pe_rows(pos_v), gk[None, :], angc)


def _gate_col(gates, head, branch):
    c = head * 3 + branch
    if isinstance(c, int):
        return gates[:, c:c + 1]
    lane = lax.broadcasted_iota(I32, gates.shape, 1)
    return jnp.sum(jnp.where(lane == c, gates, 0.0), axis=1, keepdims=True)


def _cmp_attn_kernel(q_ref, kc_ref, vct_ref, gl_ref, ovt_ref, oc_ref, sel_ref, *, tq, ngrp, hpg, nch, nblk):
    t = pl.program_id(1) * tq + lax.broadcasted_iota(I32, (1, tq), 1)
    cmp_end = lax.broadcasted_iota(I32, (nch, 1), 0) * CMP_STRIDE + (CMP_LEN - 1)
    visible = cmp_end <= t
    q = q_ref[0]
    gates = jax.nn.sigmoid(gl_ref[0])
    blk = lax.broadcasted_iota(I32, (nblk, tq), 0)
    cur = _div_pow2(t, SLC_BLOCK)
    forced = (blk == 0) | (blk == cur) | (blk == cur - 1)
    causal = blk <= cur
    lane = lax.broadcasted_iota(I32, (tq, hpg * HEAD_DIM), 1)
    outs = []
    for g in range(ngrp):
        kc = kc_ref[0, g]
        vct = vct_ref[0, g]
        psum = jnp.zeros((nch, tq), F32)
        o_t = []
        for hh in range(hpg):
            head = g * hpg + hh
            qh = q[:, head * HEAD_DIM:(head + 1) * HEAD_DIM]
            s = jnp.where(visible, _dot3_nt(kc, qh), NEG_INF)
            p = jnp.exp(s - jnp.maximum(jnp.max(s, axis=0, keepdims=True), 0.1 * NEG_INF))
            denom = jnp.sum(p, axis=0, keepdims=True)
            p = p * (1.0 / jnp.where(denom == 0.0, 1.0, denom))
            psum = psum + p
            o_t.append(_dot(vct, p.astype(BF16)))
        gate = _gate_col(gates, g * hpg, 0)
        for hh in range(1, hpg):
            gate = jnp.where(lane < hh * HEAD_DIM, gate, _gate_col(gates, g * hpg + hh, 0))
        outs.append(jnp.concatenate(o_t, axis=0).T * gate)
        hi, lo = _split(psum)
        imp = _dot(ovt_ref[...], hi) + _dot(ovt_ref[...], lo)
        val = jnp.where(causal, imp + jnp.where(forced, FORCE_BONUS, 0.0), NEG_INF)
        for _ in range(min(SLC_TOPK, nblk)):
            m = jnp.max(val, axis=0, keepdims=True)
            first = jnp.min(jnp.where(val == m, blk, nblk), axis=0, keepdims=True)
            val = jnp.where(blk == first, -jnp.inf, val)
        sel_ref[0, g] = jnp.where(val == -jnp.inf, 0.0, SEL_MASK_BIAS).T.astype(BF16)
    oc_ref[0] = jnp.concatenate(outs, axis=1)


def _cmp_attn(q, kc, vc, glog, tq):
    bsz, t, wq = q.shape
    ngrp, nch = kc.shape[1], kc.shape[2]
    hpg = wq // HEAD_DIM // ngrp
    nblk = t // SLC_BLOCK
    cs = jnp.arange(nch) * CMP_STRIDE
    ss = jnp.arange(nblk) * SLC_BLOCK
    ovt = ((cs[None, :] < ss[:, None] + SLC_BLOCK) & (cs[None, :] + CMP_LEN > ss[:, None])).astype(BF16)
    return pl.pallas_call(
        functools.partial(_cmp_attn_kernel, tq=tq, ngrp=ngrp, hpg=hpg, nch=nch, nblk=nblk),
        out_shape=(jax.ShapeDtypeStruct((bsz, t, wq), F32),
                   jax.ShapeDtypeStruct((bsz, ngrp, t, nblk), BF16)),
        grid=(bsz, t // tq),
        in_specs=[pl.BlockSpec((1, tq, wq), lambda b, i: (b, i, 0)),
                  pl.BlockSpec((1, ngrp, nch, HEAD_DIM), lambda b, i: (b, 0, 0, 0)),
                  pl.BlockSpec((1, ngrp, HEAD_DIM, nch), lambda b, i: (b, 0, 0, 0)),
                  pl.BlockSpec((1, tq, GATE_PAD), lambda b, i: (b, i, ZF_GATE)),
                  pl.BlockSpec((nblk, nch), lambda b, i: (0, 0))],
        out_specs=(pl.BlockSpec((1, tq, wq), lambda b, i: (b, i, 0)),
                   pl.BlockSpec((1, ngrp, tq, nblk), lambda b, i: (b, 0, i, 0))),
        compiler_params=_cparams(("arbitrary", "arbitrary")),
        name="nsa_cmp_select",
    )(q, kc, vc.transpose(0, 1, 3, 2).astype(BF16), glog, ovt)


SEL_ROW_CHUNK = 128


def _sel_attn_kernel(q_ref, sel_ref, kaug_ref, vaug_ref, gl_ref, o_ref, qa_s, m_s, acc, s_a, s_b, *, tq, tk, hpg):
    g = pl.program_id(1)
    qi = pl.program_id(2)
    q = q_ref[0]
    sel = sel_ref[0, 0]
    for hh in range(hpg):
        qa_s[hh * tq:(hh + 1) * tq, :] = jnp.concatenate(
            [sel, q[:, hh * HEAD_DIM:(hh + 1) * HEAD_DIM].astype(BF16)], axis=1)
    m_s[...] = jnp.full_like(m_s, NEG_INF)
    acc[...] = jnp.zeros_like(acc)
    rows = hpg * tq
    rc = SEL_ROW_CHUNK

    def scores(kt, buf):
        k0 = pl.multiple_of(kt * tk, tk)
        kblk = kaug_ref[0, 0, pl.ds(k0, tk), :]
        for c in range(rows // rc):
            rs = slice(c * rc, (c + 1) * rc)
            buf[rs, :] = _dot_nt(qa_s[rs, :], kblk)

    def consume(kt, buf, diag):
        k0 = pl.multiple_of(kt * tk, tk)
        vblk = vaug_ref[0, 0, pl.ds(k0, tk), :]
        for c in range(rows // rc):
            rs = slice(c * rc, (c + 1) * rc)
            s = buf[rs, :]
            if diag:
                t = qi * tq + _mod_pow2(c * rc + lax.broadcasted_iota(I32, (rc, tk), 0), tq)
                s = jnp.where(k0 + lax.broadcasted_iota(I32, (rc, tk), 1) <= t, s, NEG_INF)
            m_old = m_s[rs, :]
            m_new = jnp.maximum(m_old, jnp.max(s, axis=1, keepdims=True))
            alpha = jnp.exp(m_old - m_new)
            p = jnp.exp(s - m_new[:, :1])
            acc[rs, :] = alpha * acc[rs, :] + _dot(p.astype(BF16), vblk)
            m_s[rs, :] = m_new

    last = (qi * tq) // tk
    scores(0, s_a)

    def two_tiles(k):
        scores(k + 1, s_b)
        consume(k, s_a, False)
        scores(k + 2, s_a)
        consume(k + 1, s_b, False)

    def quad(j, carry):
        two_tiles(4 * j)
        two_tiles(4 * j + 2)
        return carry

    def pair(j, carry):
        two_tiles(4 * (last // 4) + 2 * j)
        return carry

    lax.fori_loop(0, last // 4, quad, 0)
    lax.fori_loop(0, (last % 4) // 2, pair, 0)
    t0 = (last // 2) * 2

    @pl.when(t0 < last)
    def _():
        scores(last, s_b)
        consume(t0, s_a, False)
        consume(last, s_b, True)

    @pl.when(t0 == last)
    def _():
        consume(last, s_a, True)

    gates = jax.nn.sigmoid(gl_ref[0])
    a = acc[...]
    out = a[:, :HEAD_DIM] / a[:, HEAD_DIM:HEAD_DIM + 1]
    o_ref[0] = jnp.concatenate(
        [out[hh * tq:(hh + 1) * tq] * _gate_col(gates, g * hpg + hh, 1) for hh in range(hpg)], axis=1)


def _sel_attn(q, sel, kaug, vaug, glog, tq, tk):
    bsz, t, wq = q.shape
    ngrp = kaug.shape[1]
    hpg = wq // HEAD_DIM // ngrp
    gw = hpg * HEAD_DIM
    nblk = sel.shape[-1]
    ka = kaug.shape[-1]
    rows = hpg * tq
    assert tk % tq == 0 and rows % SEL_ROW_CHUNK == 0
    return pl.pallas_call(
        functools.partial(_sel_attn_kernel, tq=tq, tk=tk, hpg=hpg),
        out_shape=jax.ShapeDtypeStruct((bsz, t, wq), F32),
        grid=(bsz, ngrp, t // tq),
        in_specs=[pl.BlockSpec((1, tq, gw), lambda b, g, i: (b, i, g)),
                  pl.BlockSpec((1, 1, tq, nblk), lambda b, g, i: (b, g, i, 0)),
                  pl.BlockSpec((1, 1, t, ka), lambda b, g, i: (b, g, 0, 0)),
                  pl.BlockSpec((1, 1, t, LANES), lambda b, g, i: (b, g, 0, 0)),
                  pl.BlockSpec((1, tq, GATE_PAD), lambda b, g, i: (b, i, ZF_GATE))],
        out_specs=pl.BlockSpec((1, tq, gw), lambda b, g, i: (b, i, g)),
        scratch_shapes=[pltpu.VMEM((rows, ka), BF16), pltpu.VMEM((rows, LANES), F32),
                        pltpu.VMEM((rows, LANES), F32),
                        pltpu.VMEM((rows, tk), F32), pltpu.VMEM((rows, tk), F32)],
        compiler_params=_cparams(("arbitrary", "arbitrary", "arbitrary")),
        name="nsa_selected",
    )(q, sel, kaug, vaug, glog)


def _win_attn_kernel(q_ref, k_ref, v_ref, gl_ref, bias_ref, o_ref, *, tq, hpg, span):
    g = pl.program_id(1)
    qi = pl.program_id(2)
    q = q_ref[0]
    q2 = jnp.concatenate([q[:, hh * HEAD_DIM:(hh + 1) * HEAD_DIM] for hh in range(hpg)], axis=0).astype(BF16)
    base = pl.multiple_of(jnp.maximum(qi * tq - WINDOW, 0), tq)
    kblk = k_ref[0, 0, pl.ds(base, span), :]
    vblk = v_ref[0, 0, pl.ds(base, span), :]
    gates = jax.nn.sigmoid(gl_ref[0])

    def attend(bias):
        rc = SEL_ROW_CHUNK
        chunks = [slice(c * rc, (c + 1) * rc) for c in range(hpg * tq // rc)]
        scores = [_dot_nt(q2[rs], kblk) + bias[rs] for rs in chunks]
        outs = []
        for s in scores:
            p = jnp.exp(s - jnp.max(s, axis=1, keepdims=True))
            acc = _dot(p.astype(BF16), vblk)
            outs.append(acc[:, :HEAD_DIM] / acc[:, HEAD_DIM:HEAD_DIM + 1])
        out = jnp.concatenate(outs, axis=0)
        o_ref[0] = jnp.concatenate(
            [out[hh * tq:(hh + 1) * tq] * _gate_col(gates, g * hpg + hh, 2) for hh in range(hpg)], axis=1)

    @pl.when(qi * tq >= WINDOW)
    def _():
        attend(jnp.concatenate([bias_ref[...]] * hpg, axis=0))

    @pl.when(qi * tq < WINDOW)
    def _():
        rows = hpg * tq
        t = qi * tq + _mod_pow2(lax.broadcasted_iota(I32, (rows, span), 0), tq)
        wpos = lax.broadcasted_iota(I32, (rows, span), 1)
        attend(jnp.where((wpos <= t) & (wpos > t - WINDOW), 0.0, NEG_INF))


def _win_attn(q, kw, vw, glog, tq):
    bsz, t, wq = q.shape
    ngrp = kw.shape[1]
    hpg = wq // HEAD_DIM // ngrp
    gw = hpg * HEAD_DIM
    span = WINDOW + tq
    assert WINDOW % tq == 0 and t >= span
    rel = jnp.arange(span)[None, :] - WINDOW - jnp.arange(tq)[:, None]
    band = jnp.where((rel <= 0) & (rel > -WINDOW), 0.0, NEG_INF).astype(F32)
    return pl.pallas_call(
        functools.partial(_win_attn_kernel, tq=tq, hpg=hpg, span=span),
        out_shape=jax.ShapeDtypeStruct((bsz, t, wq), F32),
        grid=(bsz, ngrp, t // tq),
        in_specs=[pl.BlockSpec((1, tq, gw), lambda b, g, i: (b, i, g)),
                  pl.BlockSpec((1, 1, t, HEAD_DIM), lambda b, g, i: (b, g, 0, 0)),
                  pl.BlockSpec((1, 1, t, LANES), lambda b, g, i: (b, g, 0, 0)),
                  pl.BlockSpec((1, tq, GATE_PAD), lambda b, g, i: (b, i, ZF_GATE)),
                  pl.BlockSpec((tq, span), lambda b, g, i: (0, 0))],
        out_specs=pl.BlockSpec((1, tq, gw), lambda b, g, i: (b, i, g)),
        compiler_params=_cparams(("arbitrary", "arbitrary", "arbitrary")),
        name="nsa_window",
    )(q, kw, vw, glog, band)


def _out_kernel(h_ref, yc_ref, ys_ref, yb_ref, oc_ref, os_ref, ow_ref, mod_ref, wglu_ref, w_ref, o_ref):
    g = _gelu_tanh(ys_ref[0])
    y_ssm = g * jax.nn.sigmoid(_dot(g.astype(BF16), wglu_ref[...]))
    y = jnp.concatenate([yc_ref[0], y_ssm, yb_ref[0], oc_ref[0] + os_ref[0] + ow_ref[0]], axis=1)
    o_ref[0] = h_ref[0] + mod_ref[0, 2:3, :] * _dot(y.astype(BF16), w_ref[...])


def _out_proj(h, parts, mod, w_glu_bf16, w_bf16, tm):
    bsz, t, d = h.shape
    wp = parts[0].shape[-1]
    tok = lambda b, i: (b, i, 0)
    return pl.pallas_call(
        _out_kernel,
        out_shape=jax.ShapeDtypeStruct((bsz, t, d), F32),
        grid=(bsz, t // tm),
        in_specs=[pl.BlockSpec((1, tm, d), tok)] + [pl.BlockSpec((1, tm, wp), tok)] * 6
        + [pl.BlockSpec((1, 6, d), lambda b, i: (b, 0, 0)),
           pl.BlockSpec((wp, wp), lambda b, i: (0, 0)),
           pl.BlockSpec((d, d), lambda b, i: (0, 0))],
        out_specs=pl.BlockSpec((1, tm, d), tok),
        compiler_params=_cparams(("arbitrary", "arbitrary")),
        name="out_proj",
    )(h, *parts, mod, w_glu_bf16, w_bf16)


def _ffn_kernel(h_ref, g_ref, mod_ref, wg_ref, wu_ref, wd_ref, o_ref, a_s, acc):
    f = pl.program_id(2)

    @pl.when(f == 0)
    def _():
        x = h_ref[0]
        y = x * _rms_scale(x) * g_ref[...]
        a_s[...] = (y * (1.0 + mod_ref[0, 4:5, :]) + mod_ref[0, 3:4, :]).astype(BF16)
        acc[...] = jnp.zeros_like(acc)

    a = a_s[...]
    hid = _silu(_dot(a, wg_ref[...])) * _dot(a, wu_ref[...])
    acc[...] += _dot(hid.astype(BF16), wd_ref[...])

    @pl.when(f == pl.num_programs(2) - 1)
    def _():
        o_ref[0] = h_ref[0] + mod_ref[0, 5:6, :] * acc[...]


def _dense_ffn(h, g, mod, wg, wu, wd, tm, tf):
    bsz, t, d = h.shape
    f = wg.shape[1]
    tok = lambda b, i, j: (b, i, 0)
    return pl.pallas_call(
        _ffn_kernel,
        out_shape=jax.ShapeDtypeStruct((bsz, t, d), F32),
        grid=(bsz, t // tm, f // tf),
        in_specs=[pl.BlockSpec((1, tm, d), tok),
                  pl.BlockSpec((1, d), lambda b, i, j: (0, 0)),
                  pl.BlockSpec((1, 6, d), lambda b, i, j: (b, 0, 0)),
                  pl.BlockSpec((d, tf), lambda b, i, j: (0, j)),
                  pl.BlockSpec((d, tf), lambda b, i, j: (0, j)),
                  pl.BlockSpec((tf, d), lambda b, i, j: (j, 0))],
        out_specs=pl.BlockSpec((1, tm, d), tok),
        scratch_shapes=[pltpu.VMEM((tm, d), BF16), pltpu.VMEM((tm, d), F32)],
        compiler_params=_cparams(("arbitrary", "arbitrary", "arbitrary")),
        name="dense_swiglu",
    )(h, g[None, :], mod, wg, wu, wd)


MOE_CHUNK = 256


def _moe_route_kernel(h_ref, g_ref, mod_ref, wrt_ref, a_ref, gates_ref, rank_ref, cnt_ref, *, tm):
    x = h_ref[0]
    y = x * _rms_scale(x) * g_ref[...]
    a = y * (1.0 + mod_ref[0, 4:5, :]) + mod_ref[0, 3:4, :]
    a_ref[0] = a.astype(BF16)
    logits = _dot3_nt(wrt_ref[...], a)
    row = lax.broadcasted_iota(I32, logits.shape, 0)
    v1 = jnp.max(logits, axis=0, keepdims=True)
    i1 = jnp.min(jnp.where(logits == v1, row, N_EXPERTS), axis=0, keepdims=True)
    rest = jnp.where(row == i1, -jnp.inf, logits)
    v2 = jnp.max(rest, axis=0, keepdims=True)
    i2 = jnp.min(jnp.where(rest == v2, row, N_EXPERTS), axis=0, keepdims=True)
    e2 = jnp.exp(v2 - v1)
    gates_ref[0] = jnp.where(row == i1, 1.0 / (1.0 + e2), 0.0) + jnp.where(row == i2, e2 / (1.0 + e2), 0.0)
    routed = jnp.where(row == i1, 1.0, jnp.where(row == i2, 1.0, 0.0))
    si = lax.broadcasted_iota(I32, (MOE_CHUNK, MOE_CHUNK), 0)
    sj = lax.broadcasted_iota(I32, (MOE_CHUNK, MOE_CHUNK), 1)
    before = jnp.where(si < sj, 1.0, 0.0).astype(BF16)
    carry = jnp.zeros((N_EXPERTS, 1), F32)
    lane = lax.broadcasted_iota(I32, (N_EXPERTS, LANES), 1)
    starts = jnp.zeros((N_EXPERTS, LANES), F32)
    parts = []
    nsub = tm // MOE_CHUNK
    for c in range(nsub):
        rc = routed[:, c * MOE_CHUNK:(c + 1) * MOE_CHUNK]
        parts.append(_dot(rc.astype(BF16), before) + carry)
        starts = jnp.where(lane == c, carry, starts)
        carry = carry + jnp.sum(rc, axis=1, keepdims=True)
    rank = jnp.concatenate(parts, axis=1)
    rank_ref[0] = jnp.where(routed > 0.0, rank, -1.0)
    cnt_ref[0] = jnp.where(lane == nsub, carry, starts).astype(I32)


MOE_WINDOW = 5


def _moe_expert_kernel(cnt_ref, a_ref, rank_t_ref, rank_c_ref, gate_c_ref, wg_ref, wu_ref, wd_ref, o_ref,
                       xc, yacc, rc_s, gc_s, *, tm, ne):
    i = pl.program_id(0)
    e = pl.program_id(1)
    f = pl.program_id(2)
    nsub = tm // MOE_CHUNK
    win = min(MOE_WINDOW, nsub)
    base = (i * ne + e) * (nsub + 1)
    nch = (cnt_ref[base + nsub] + (MOE_CHUNK - 1)) // MOE_CHUNK
    d = o_ref.shape[-1]

    def window(c):
        lo = c * MOE_CHUNK
        first = jnp.int32(0)
        stop = jnp.int32(0)
        for s in range(nsub):
            first += (cnt_ref[base + s + 1] <= lo).astype(I32)
            stop += (cnt_ref[base + s] < lo + MOE_CHUNK).astype(I32)
        s0 = jnp.minimum(first, nsub - win)
        return s0, stop <= s0 + win

    def pick_rows(c, s0, nsubs):
        tgt = (c * MOE_CHUNK + lax.broadcasted_iota(I32, (MOE_CHUNK, MOE_CHUNK), 0)).astype(F32)
        return jnp.concatenate(
            [jnp.where(rank_t_ref[0, e, pl.ds(s0 + j, 1), :] == tgt, 1.0, 0.0).astype(BF16)
             for j in range(nsubs)], axis=1)

    @pl.when(jnp.logical_and(e == 0, f == 0))
    def _():
        o_ref[...] = jnp.zeros_like(o_ref)

    @pl.when(f == 0)
    def _():
        def gather(c, carry):
            r0 = pl.multiple_of(c * MOE_CHUNK, MOE_CHUNK)
            s0, fits = window(c)

            @pl.when(fits)
            def _():
                t0 = pl.multiple_of(s0 * MOE_CHUNK, MOE_CHUNK)
                xc[pl.ds(r0, MOE_CHUNK), :] = _dot(pick_rows(c, s0, win),
                                                   a_ref[pl.ds(t0, win * MOE_CHUNK), :]).astype(BF16)

            @pl.when(jnp.logical_not(fits))
            def _():
                xc[pl.ds(r0, MOE_CHUNK), :] = _dot(pick_rows(c, 0, nsub), a_ref[...]).astype(BF16)

            yacc[pl.ds(r0, MOE_CHUNK), :] = jnp.zeros((MOE_CHUNK, d), F32)
            return carry

        lax.fori_loop(0, nch, gather, 0)

    def ffn(c, carry):
        r0 = pl.multiple_of(c * MOE_CHUNK, MOE_CHUNK)
        x = xc[pl.ds(r0, MOE_CHUNK), :]
        hid = _silu(_dot(x, wg_ref[0])) * _dot(x, wu_ref[0])
        yacc[pl.ds(r0, MOE_CHUNK), :] += _dot(hid.astype(BF16), wd_ref[0])
        return carry

    lax.fori_loop(0, nch, ffn, 0)

    @pl.when(f == pl.num_programs(2) - 1)
    def _():
        lane = lax.broadcasted_iota(I32, (tm, ne), 1)
        rc_s[...] = jnp.sum(jnp.where(lane == e, rank_c_ref[0], 0.0), axis=1, keepdims=True)
        gc_s[...] = jnp.sum(jnp.where(lane == e, gate_c_ref[0], 0.0), axis=1, keepdims=True)

        def put(c, t0, ntok):
            r0 = pl.multiple_of(c * MOE_CHUNK, MOE_CHUNK)
            rows = pl.ds(t0, ntok)
            tgt = (r0 + lax.broadcasted_iota(I32, (ntok, MOE_CHUNK), 1)).astype(F32)
            place = jnp.where(rc_s[rows, :] == tgt, 1.0, 0.0).astype(BF16)
            o_ref[rows, :] += gc_s[rows, :] * _dot(place, yacc[pl.ds(r0, MOE_CHUNK), :].astype(BF16))

        def scatter(c, carry):
            s0, fits = window(c)

            @pl.when(fits)
            def _():
                put(c, pl.multiple_of(s0 * MOE_CHUNK, MOE_CHUNK), win * MOE_CHUNK)

            @pl.when(jnp.logical_not(fits))
            def _():
                put(c, 0, tm)

            return carry

        lax.fori_loop(0, nch, scatter, 0)


def _residual_kernel(h_ref, y_ref, mod_ref, o_ref):
    o_ref[0] = h_ref[0] + mod_ref[0, 5:6, :] * y_ref[0]


def _moe_ffn(h, g, mod, router, wg, wu, wd, tm, tf):
    bsz, t, d = h.shape
    ne, _, f = wg.shape
    assert ne == N_EXPERTS and tm % MOE_CHUNK == 0
    nt = t // tm
    tok = lambda b, i: (b, i, 0)
    a, gates_t, rank_t, cnt = pl.pallas_call(
        functools.partial(_moe_route_kernel, tm=tm),
        out_shape=(jax.ShapeDtypeStruct((bsz, t, d), BF16),
                   jax.ShapeDtypeStruct((bsz * nt, ne, tm), F32),
                   jax.ShapeDtypeStruct((bsz * nt, ne, tm), F32),
                   jax.ShapeDtypeStruct((bsz * nt, ne, LANES), I32)),
        grid=(bsz, nt),
        in_specs=[pl.BlockSpec((1, tm, d), tok),
                  pl.BlockSpec((1, d), lambda b, i: (0, 0)),
                  pl.BlockSpec((1, 6, d), lambda b, i: (b, 0, 0)),
                  pl.BlockSpec((ne, d), lambda b, i: (0, 0))],
        out_specs=(pl.BlockSpec((1, tm, d), tok),
                   pl.BlockSpec((1, ne, tm), lambda b, i: (b * nt + i, 0, 0)),
                   pl.BlockSpec((1, ne, tm), lambda b, i: (b * nt + i, 0, 0)),
                   pl.BlockSpec((1, ne, LANES), lambda b, i: (b * nt + i, 0, 0))),
        compiler_params=_cparams(("arbitrary", "arbitrary")),
        name="moe_route",
    )(h, g[None, :], mod, router.T)
    nsub = tm // MOE_CHUNK
    counts = cnt[:, :, :nsub + 1].reshape(-1)
    y = pl.pallas_call(
        functools.partial(_moe_expert_kernel, tm=tm, ne=ne),
        out_shape=jax.ShapeDtypeStruct((bsz * t, d), F32),
        grid_spec=pltpu.PrefetchScalarGridSpec(
            num_scalar_prefetch=1,
            grid=(bsz * nt, ne, f // tf),
            in_specs=[pl.BlockSpec((tm, d), lambda i, e, j, cnt: (i, 0)),
                      pl.BlockSpec((1, ne, nsub, MOE_CHUNK), lambda i, e, j, cnt: (i, 0, 0, 0)),
                      pl.BlockSpec((1, tm, ne), lambda i, e, j, cnt: (i, 0, 0)),
                      pl.BlockSpec((1, tm, ne), lambda i, e, j, cnt: (i, 0, 0)),
                      pl.BlockSpec((1, d, tf), lambda i, e, j, cnt: (e, 0, j)),
                      pl.BlockSpec((1, d, tf), lambda i, e, j, cnt: (e, 0, j)),
                      pl.BlockSpec((1, tf, d), lambda i, e, j, cnt: (e, j, 0))],
            out_specs=pl.BlockSpec((tm, d), lambda i, e, j, cnt: (i, 0)),
            scratch_shapes=[pltpu.VMEM((tm, d), BF16), pltpu.VMEM((tm, d), F32),
                            pltpu.VMEM((tm, 1), F32), pltpu.VMEM((tm, 1), F32)]),
        compiler_params=_cparams(("arbitrary", "arbitrary", "arbitrary")),
        name="moe_experts",
    )(counts, a.reshape(bsz * t, d), rank_t.reshape(bsz * nt, ne, nsub, MOE_CHUNK), rank_t.transpose(0, 2, 1),
      gates_t.transpose(0, 2, 1), wg, wu, wd)
    tr = _Tiles().proj
    return pl.pallas_call(
        _residual_kernel,
        out_shape=jax.ShapeDtypeStruct((bsz, t, d), F32),
        grid=(bsz, t // tr),
        in_specs=[pl.BlockSpec((1, tr, d), tok), pl.BlockSpec((1, tr, d), tok),
                  pl.BlockSpec((1, 6, d), lambda b, i: (b, 0, 0))],
        out_specs=pl.BlockSpec((1, tr, d), tok),
        compiler_params=_cparams(("arbitrary", "arbitrary")),
        name="moe_residual",
    )(h, y.reshape(bsz, t, d), mod)


class _Tiles(NamedTuple):
    proj: int = 512
    attn_q: int = 256
    sel_k: int = 512
    s5_rows: int = 256
    ffn_m: int = 1024
    moe_m: int = 2048
    ffn_f: int = 512


def _tiles(t):
    base = _Tiles()
    return base._replace(s5_rows=min(base.s5_rows, t // SSM_CHUNK), ffn_m=min(base.ffn_m, t),
                         moe_m=min(base.moe_m, t))


def _pack_w_in(w_in, dq):
    o1 = 3 * dq
    o2 = o1 + dq
    o3 = o2 + 3 * dq
    sb_q = w_in[:, o2:o2 + dq] * (HEAD_DIM ** -0.5)
    pad = jnp.zeros((w_in.shape[0], GATE_PAD - (w_in.shape[1] - (o3 + dq + 6 * NSA_KV_HEADS * HEAD_DIM))),
                    w_in.dtype)
    cols = [w_in[:, :o1], sb_q, w_in[:, o2 + dq:o3], w_in[:, o1:o2], w_in[:, o3:], pad]
    return jnp.concatenate(cols, axis=1).astype(BF16), o1 + 3 * dq


def _token_mixing(h, mod, positions, rope, g_mix, w_in, w_out, conv_w, ssm, nsa):
    bsz, t, d = h.shape
    dq = d // 4
    tl = _tiles(t)
    w_packed, nb = _pack_w_in(w_in, dq)
    zb, zs, zf = _in_proj(h, g_mix, mod, w_packed, nb, dq, tm=tl.proj)
    y_conv = _short_conv(zb, conv_w, tt=tl.proj)
    y_ssm = _s5(zs, *ssm[:-1], tt=tl.s5_rows)
    y_sb = _stick_breaking(zb, dq, tq=tl.attn_q)
    o_c, o_s, o_w = _nsa_mixer(zf, dq, positions, rope, nsa)
    return _out_proj(h, [y_conv, y_ssm, y_sb, o_c, o_s, o_w], mod, ssm[-1].astype(BF16), w_out.astype(BF16),
                     tm=tl.proj)


def _nsa_mixer(zf, wq, positions, rope, nsa):
    bsz, t, _ = zf.shape
    q_norm_g, k_norm_g, pos_k, pos_v, k_w1, k_w2, v_w1, v_w2 = nsa
    half = HEAD_DIM // 2
    inv_freq = jnp.power(jnp.float32(ROPE_THETA), -jnp.arange(half, dtype=F32) / half)
    tl = _tiles(t)
    q_rot, ck, cv, kaug, vaug, kwin, vwin = _nsa_prep(zf, rope, q_norm_g, k_norm_g, wq, tm=tl.proj)
    nch = t // CMP_STRIDE
    end_idx = jnp.minimum(jnp.arange(nch) * CMP_STRIDE + CMP_LEN - 1, t - 1)
    angc = jnp.tile(positions[:, end_idx].astype(F32)[..., None] * inv_freq, (1, 1, 2))
    kc, vc = _compress(ck, cv, k_w1, k_w2, v_w1, v_w2, pos_k, pos_v, k_norm_g, angc)
    o_c, sel = _cmp_attn(q_rot, kc, vc, zf, tq=tl.attn_q)
    o_s = _sel_attn(q_rot, sel, kaug, vaug, zf, tq=tl.attn_q, tk=tl.sel_k)
    o_w = _win_attn(q_rot, kwin, vwin, zf, tq=tl.attn_q)
    return o_c, o_s, o_w


def kernel(x, c, positions, ada_w, ada_b, norm_mix_g, norm_ffn_g, w_in, w_out, conv_w, ssm_lam_re, ssm_lam_im, ssm_b_re, ssm_b_im, ssm_c_re, ssm_c_im, ssm_d, ssm_log_dt, ssm_w_glu, nsa_q_norm_g, nsa_k_norm_g, cmp_pos_k, cmp_pos_v, cmp_k_w1, cmp_k_w2, cmp_v_w1, cmp_v_w2, ffn_w_gate, ffn_w_up, ffn_w_down, moe_router, moe_w_gate, moe_w_up, moe_w_down):
    depth = ada_w.shape[0]
    mods = _modulation(c, ada_w, ada_b)
    rope = _rope_tables(positions)
    h = x
    for layer in range(depth):
        mod = mods[layer]
        ssm = (ssm_lam_re[layer], ssm_lam_im[layer], ssm_b_re[layer], ssm_b_im[layer], ssm_c_re[layer],
               ssm_c_im[layer], ssm_d[layer], ssm_log_dt[layer], ssm_w_glu[layer])
        nsa = (nsa_q_norm_g[layer], nsa_k_norm_g[layer], cmp_pos_k[layer], cmp_pos_v[layer],
               cmp_k_w1[layer], cmp_k_w2[layer], cmp_v_w1[layer], cmp_v_w2[layer])
        h = _token_mixing(h, mod, positions, rope, norm_mix_g[layer], w_in[layer], w_out[layer],
                          conv_w[layer], ssm, nsa)
        i = layer // 2
        tl = _tiles(h.shape[1])
        if layer % 2 == 0:
            h = _dense_ffn(h, norm_ffn_g[layer], mod, ffn_w_gate[i].astype(BF16), ffn_w_up[i].astype(BF16),
                           ffn_w_down[i].astype(BF16), tm=tl.ffn_m, tf=tl.ffn_f)
        else:
            h = _moe_ffn(h, norm_ffn_g[layer], mod, moe_router[i], moe_w_gate[i].astype(BF16),
                         moe_w_up[i].astype(BF16), moe_w_down[i].astype(BF16), tm=tl.moe_m, tf=tl.ffn_f)
    return h
```

```python
import functools
import math
from typing import NamedTuple

import jax
import jax.numpy as jnp
from jax import lax
from jax.experimental import pallas as pl
from jax.experimental.pallas import tpu as pltpu

F32 = jnp.float32
BF16 = jnp.bfloat16
I32 = jnp.int32

HEAD_DIM = 64
CONV_WIDTH = 3
SSM_GROUP = 16
SSM_STATE = 64
SSM_MAX_RE = -1e-4
NSA_KV_HEADS = 2
CMP_LEN = 32
CMP_STRIDE = 16
SLC_BLOCK = 64
SLC_TOPK = 16
WINDOW = 512
FORCE_BONUS = 1e4
NEG_INF = -1e30
ROPE_THETA = 10000.0
RMS_EPS = 1e-6
N_EXPERTS = 8

LANES = 128
SUBLANES = 8
VMEM_LIMIT = 56 * 1024 * 1024
SEL_MASK_BIAS = -30000.0
SB_SKIP_LOG = -110.0
GATE_PAD = LANES


def _cparams(sem):
    return pltpu.CompilerParams(dimension_semantics=sem, vmem_limit_bytes=VMEM_LIMIT)


def _dot(a, b):
    return jnp.dot(a, b, preferred_element_type=F32)


def _dot_nt(a, b):
    return lax.dot_general(a, b, (((1,), (1,)), ((), ())), preferred_element_type=F32)


def _split(x):
    hi = x.astype(BF16)
    lo = (x - hi.astype(F32)).astype(BF16)
    return hi, lo


def _dot3(a, b):
    ah, al = _split(a)
    bh, bl = _split(b)
    return _dot(ah, bh) + _dot(ah, bl) + _dot(al, bh)


def _dot3_nt(a, b):
    ah, al = _split(a)
    bh, bl = _split(b)
    return _dot_nt(ah, bh) + _dot_nt(ah, bl) + _dot_nt(al, bh)


def _dot2_exact_rhs(a, b_bf16):
    ah, al = _split(a)
    return _dot(ah, b_bf16) + _dot(al, b_bf16)


def _silu(x):
    return x * jax.nn.sigmoid(x)


def _div_pow2(x, n):
    return lax.shift_right_logical(x, jnp.int32(n.bit_length() - 1))


def _mod_pow2(x, n):
    return x & (n - 1)


def _rms_scale(x):
    return lax.rsqrt(jnp.mean(x * x, axis=-1, keepdims=True) + RMS_EPS)


def _mod_kernel(c_ref, w_ref, b_ref, o_ref):
    o_ref[0] = _dot3(_silu(c_ref[...]), w_ref[0]) + b_ref[0]


def _modulation(c, ada_w, ada_b):
    depth, d, n6 = ada_w.shape
    bsz = c.shape[0]
    rows = -(-bsz // SUBLANES) * SUBLANES
    cpad = jnp.zeros((rows, d), F32).at[:bsz].set(c)
    tn = n6 // 4
    out = pl.pallas_call(
        _mod_kernel,
        out_shape=jax.ShapeDtypeStruct((depth, rows, n6), F32),
        grid=(depth, n6 // tn),
        in_specs=[pl.BlockSpec((rows, d), lambda l, j: (0, 0)),
                  pl.BlockSpec((1, d, tn), lambda l, j: (l, 0, j)),
                  pl.BlockSpec((1, 1, tn), lambda l, j: (l, 0, j))],
        out_specs=pl.BlockSpec((1, rows, tn), lambda l, j: (l, 0, j)),
        compiler_params=_cparams(("arbitrary", "arbitrary")),
        name="adaln_mod",
    )(cpad, ada_w, ada_b[:, None, :])
    return out[:, :bsz].reshape(depth, bsz, 6, d)


def _in_kernel(h_ref, g_ref, mod_ref, w_ref, zb_ref, zs_ref, zf_ref):
    x = h_ref[0]
    y = x * _rms_scale(x) * g_ref[...]
    a = y * (1.0 + mod_ref[0, 1:2, :]) + mod_ref[0, 0:1, :]
    z = _dot(a.astype(BF16), w_ref[...])
    nb = zb_ref.shape[-1]
    ns = zs_ref.shape[-1]
    zb_ref[0] = z[:, :nb].astype(BF16)
    zs_ref[0] = z[:, nb:nb + ns]
    zf_ref[0] = z[:, nb + ns:]


def _in_proj(h, g, mod, w_bf16, nb, ns, tm):
    bsz, t, d = h.shape
    nz = w_bf16.shape[1]
    widths = (nb, ns, nz - nb - ns)
    return pl.pallas_call(
        _in_kernel,
        out_shape=tuple(jax.ShapeDtypeStruct((bsz, t, w), dt) for w, dt in zip(widths, (BF16, F32, F32))),
        grid=(bsz, t // tm),
        in_specs=[pl.BlockSpec((1, tm, d), lambda b, i: (b, i, 0)),
                  pl.BlockSpec((1, d), lambda b, i: (0, 0)),
                  pl.BlockSpec((1, 6, d), lambda b, i: (b, 0, 0)),
                  pl.BlockSpec((d, nz), lambda b, i: (0, 0))],
        out_specs=tuple(pl.BlockSpec((1, tm, w), lambda b, i: (b, i, 0)) for w in widths),
        compiler_params=_cparams(("arbitrary", "arbitrary")),
        name="in_proj",
    )(h, g[None, :], mod, w_bf16)


def _conv_kernel(z_ref, w_ref, o_ref, buf, *, tt, dc):
    @pl.when(pl.program_id(1) == 0)
    def _():
        buf[0:SUBLANES, :] = jnp.zeros((SUBLANES, dc), F32)

    z = z_ref[0].astype(F32)
    gate_b = z[:, :dc]
    v = z[:, dc:2 * dc] * z[:, 2 * dc:]
    buf[SUBLANES:, :] = v
    y = (w_ref[2:3, :] * v
         + w_ref[1:2, :] * buf[SUBLANES - 1:SUBLANES - 1 + tt, :]
         + w_ref[0:1, :] * buf[SUBLANES - 2:SUBLANES - 2 + tt, :])
    o_ref[0] = (gate_b * y).astype(o_ref.dtype)
    buf[0:SUBLANES, :] = v[tt - SUBLANES:, :]


def _short_conv(zb, conv_w, tt):
    bsz, t, _ = zb.shape
    dc = conv_w.shape[1]
    c3 = 3 * dc
    return pl.pallas_call(
        functools.partial(_conv_kernel, tt=tt, dc=dc),
        out_shape=jax.ShapeDtypeStruct((bsz, t, dc), BF16),
        grid=(bsz, t // tt),
        in_specs=[pl.BlockSpec((1, tt, c3), lambda b, i: (b, i, 0)),
                  pl.BlockSpec((CONV_WIDTH, dc), lambda b, i: (0, 0))],
        out_specs=pl.BlockSpec((1, tt, dc), lambda b, i: (b, i, 0)),
        scratch_shapes=[pltpu.VMEM((tt + SUBLANES, dc), F32)],
        compiler_params=_cparams(("arbitrary", "arbitrary")),
        name="short_conv",
    )(zb, conv_w)


SSM_LANE_CHUNK = 2 * LANES


def _gelu_tanh(x):
    return 0.5 * x * (1.0 + jnp.tanh(math.sqrt(2.0 / math.pi) * (x + 0.044715 * (x * x * x))))


def _ssm_kernel(u_ref, tz_ref, we_ref, wd_ref, pre_ref, pim_ref, d_ref, o_ref,
                bre, bim, cin, cre, cim, *, tt, ns):
    @pl.when(pl.program_id(1) == 0)
    def _():
        cre[...] = jnp.zeros_like(cre)
        cim[...] = jnp.zeros_like(cim)

    u = u_ref[0]
    ub = u.astype(BF16)
    end = _dot(ub, we_ref[...])
    bre[...] = end[:, :ns]
    bim[...] = end[:, ns:]
    lw = SSM_LANE_CHUNK
    row = lax.broadcasted_iota(I32, (SUBLANES, lw), 0)

    def local_scan(i, carry):
        r0 = pl.multiple_of(i * SUBLANES, SUBLANES)
        for lg in range(ns // lw):
            sl = slice(lg * lw, (lg + 1) * lw)
            xr = bre[pl.ds(r0, SUBLANES), sl]
            xi = bim[pl.ds(r0, SUBLANES), sl]
            for k in (1, 2, 4):
                ar = pre_ref[k - 1:k, sl]
                ai = pim_ref[k - 1:k, sl]
                sr = jnp.where(row >= k, pltpu.roll(xr, k, 0), 0.0)
                si = jnp.where(row >= k, pltpu.roll(xi, k, 0), 0.0)
                xr, xi = xr + ar * sr - ai * si, xi + ar * si + ai * sr
            bre[pl.ds(r0, SUBLANES), sl] = xr
            bim[pl.ds(r0, SUBLANES), sl] = xi
        return carry

    lax.fori_loop(0, tt // SUBLANES, local_scan, 0, unroll=2)

    row_all = lax.broadcasted_iota(I32, (SUBLANES, ns), 0)

    def carry_in(i, carry):
        c_re, c_im = carry
        r0 = pl.multiple_of(i * SUBLANES, SUBLANES)
        pr = pre_ref[...]
        pi_ = pim_ref[...]
        xr = bre[pl.ds(r0, SUBLANES), :] + pr * c_re - pi_ * c_im
        xi = bim[pl.ds(r0, SUBLANES), :] + pr * c_im + pi_ * c_re
        cin[pl.ds(r0, SUBLANES), :ns] = jnp.where(row_all == 0, c_re, pltpu.roll(xr, 1, 0))
        cin[pl.ds(r0, SUBLANES), ns:] = jnp.where(row_all == 0, c_im, pltpu.roll(xi, 1, 0))
        return xr[SUBLANES - 1:], xi[SUBLANES - 1:]

    c_re, c_im = lax.fori_loop(0, tt // SUBLANES, carry_in, (cre[...], cim[...]))
    cre[...] = c_re
    cim[...] = c_im
    o_ref[0] = _dot(ub, tz_ref[...]) + _dot(cin[...].astype(BF16), wd_ref[...]) + d_ref[...] * u


SSM_CHUNK = 8


def _s5(u, lam_re, lam_im, b_re, b_im, c_re, c_im, d_skip, log_dt, tt):
    bsz, t, ds = u.shape
    g, p = lam_re.shape
    hc = SSM_GROUP
    ns = g * p
    nl = SSM_CHUNK
    lam = lax.complex(jnp.minimum(lam_re, SSM_MAX_RE), lam_im)
    dt = jnp.exp(log_dt)[:, None]
    lam_bar = jnp.exp(lam * dt)
    b_bar = ((lam_bar - 1.0) / lam)[..., None] * lax.complex(b_re, b_im)
    cmat = lax.complex(c_re, c_im)
    steps = jnp.arange(nl + 1, dtype=F32)[:, None, None]
    pw = jnp.exp(steps * (lam * dt)[None])
    in_grp = (jnp.arange(nl * ds) // hc) % g
    st_grp = jnp.arange(ns) // p
    rep = lambda n, m: (jnp.arange(m)[None, :] % n == jnp.arange(n)[:, None]).astype(F32)
    kern = jnp.einsum('gcp,tgp,gpd->tgdc', cmat, pw[:nl], b_bar).real.reshape(nl, ds, hc)
    zero = jnp.zeros((ds, hc), F32)
    rows_rc = jnp.concatenate(
        [jnp.concatenate([zero] * s + [kern[r - s] for r in range(s, nl)], axis=1) for s in range(nl)], axis=0)
    col_rc = (jnp.arange(nl * ds) // ds) * hc + jnp.arange(nl * ds) % hc
    spread_rc = (col_rc[None, :] == jnp.arange(nl * hc)[:, None]).astype(F32)
    tz = jnp.dot(rows_rc, spread_rc) * (in_grp[:, None] == in_grp[None, :])
    end = (pw[:nl][::-1][..., None] * b_bar[None]).transpose(0, 1, 3, 2).reshape(nl * ds, p)
    same_in_st = in_grp[:, None] == st_grp[None, :]
    we = jnp.concatenate([jnp.dot(part, rep(p, ns)) * same_in_st for part in (end.real, end.imag)], axis=1)
    dec = (cmat[None] * pw[1:, :, None, :]).transpose(3, 0, 1, 2).reshape(p, nl * ds)
    wd = jnp.concatenate([jnp.dot(rep(p, ns).T, part) * same_in_st.T for part in (dec.real, -dec.imag)], axis=0)
    pows = jnp.exp(jnp.arange(1, SUBLANES + 1, dtype=F32)[:, None, None] * nl * (lam * dt)[None])
    pows = pows.reshape(SUBLANES, ns)
    rows = t // nl
    const = lambda shape: pl.BlockSpec(shape, lambda b, i: (0, 0), pipeline_mode=pl.Buffered(1))
    y = pl.pallas_call(
        functools.partial(_ssm_kernel, tt=tt, ns=ns),
        out_shape=jax.ShapeDtypeStruct((bsz, rows, nl * ds), F32),
        grid=(bsz, rows // tt),
        in_specs=[pl.BlockSpec((1, tt, nl * ds), lambda b, i: (b, i, 0)),
                  const((nl * ds, nl * ds)), const((nl * ds, 2 * ns)), const((2 * ns, nl * ds)),
                  pl.BlockSpec((SUBLANES, ns), lambda b, i: (0, 0)),
                  pl.BlockSpec((SUBLANES, ns), lambda b, i: (0, 0)),
                  pl.BlockSpec((1, nl * ds), lambda b, i: (0, 0))],
        out_specs=pl.BlockSpec((1, tt, nl * ds), lambda b, i: (b, i, 0)),
        scratch_shapes=[pltpu.VMEM((tt, ns), F32), pltpu.VMEM((tt, ns), F32), pltpu.VMEM((tt, 2 * ns), F32),
                        pltpu.VMEM((1, ns), F32), pltpu.VMEM((1, ns), F32)],
        compiler_params=_cparams(("arbitrary", "arbitrary")),
        name="s5_scan",
    )(u.reshape(bsz, rows, nl * ds), tz.astype(BF16), we.astype(BF16), wd.astype(BF16), pows.real, pows.imag,
      jnp.tile(d_skip, nl)[None, :])
    return y.reshape(bsz, t, ds)


def _sb_kernel(q_ref, k_ref, v_ref, o_ref, acc, csum, *, tq, nh):
    qi = pl.program_id(1)
    acc[...] = jnp.zeros_like(acc)
    csum[...] = jnp.zeros_like(csum)
    ti = lax.broadcasted_iota(I32, (tq, tq), 0)
    ji = lax.broadcasted_iota(I32, (tq, tq), 1)
    later = (ti > ji).astype(BF16)

    def cond(state):
        kb, cmax = state
        return jnp.logical_and(kb >= 0, cmax > SB_SKIP_LOG)

    def block(kb, diagonal):
        k0 = pl.multiple_of(kb * tq, tq)
        past = ji < ti
        keep = (lambda x: jnp.where(past, x, 0.0)) if diagonal else (lambda x: x)
        heads = [slice(h * HEAD_DIM, (h + 1) * HEAD_DIM) for h in range(nh)]
        zs = [_dot_nt(q_ref[0, :, hs], k_ref[0, pl.ds(k0, tq), hs]) for hs in heads]
        log_betas, log_keeps, inners = [], [], []
        for z in zs:
            log_beta = jnp.minimum(z, 0.0) - jnp.log(1.0 + jnp.exp(-jnp.abs(z)))
            log_keep = keep(log_beta - z)
            hi, lo = _split(log_keep)
            log_betas.append(log_beta)
            log_keeps.append(log_keep)
            inners.append(_dot(hi, later) + _dot(lo, later))
        cmax = jnp.float32(-jnp.inf)
        for h, hs in enumerate(heads):
            c = csum[h]
            w = keep(jnp.exp(log_betas[h] + inners[h] + c[:, :1]))
            acc[:, hs] += _dot(w.astype(BF16), v_ref[0, pl.ds(k0, tq), hs])
            c = c + jnp.sum(log_keeps[h], axis=1, keepdims=True)
            csum[h] = c
            cmax = jnp.maximum(cmax, jnp.max(c))
        return cmax

    def body(state):
        kb, _ = state
        return kb - 1, block(kb, False)

    lax.while_loop(cond, body, (qi - 1, block(qi, True)))
    o_ref[0] = acc[...].astype(o_ref.dtype)


SB_Q_BLOCK, SB_K_BLOCK, SB_V_BLOCK = 3, 4, 5


def _stick_breaking(zb, dq, tq):
    bsz, t, _ = zb.shape
    nh = dq // HEAD_DIM
    return pl.pallas_call(
        functools.partial(_sb_kernel, tq=tq, nh=nh),
        out_shape=jax.ShapeDtypeStruct((bsz, t, dq), BF16),
        grid=(bsz, t // tq),
        in_specs=[pl.BlockSpec((1, tq, dq), lambda b, i: (b, i, SB_Q_BLOCK)),
                  pl.BlockSpec((1, t, dq), lambda b, i: (b, 0, SB_K_BLOCK)),
                  pl.BlockSpec((1, t, dq), lambda b, i: (b, 0, SB_V_BLOCK))],
        out_specs=pl.BlockSpec((1, tq, dq), lambda b, i: (b, i, 0)),
        scratch_shapes=[pltpu.VMEM((tq, dq), F32), pltpu.VMEM((nh, tq, LANES), F32)],
        compiler_params=_cparams(("arbitrary", "arbitrary")),
        name="stick_breaking",
    )(zb, zb, zb)


def _head_norm(x, gsum_ref, gain):
    ss = _dot2_exact_rhs(x * x, gsum_ref[...]) * (1.0 / HEAD_DIM)
    return x * lax.rsqrt(ss + RMS_EPS) * gain


def _rope_lanes(x, cos, sin):
    w = x.shape[-1]
    lane = lax.broadcasted_iota(I32, x.shape, 1)
    first = _mod_pow2(lane, HEAD_DIM) < (HEAD_DIM // 2)
    partner = jnp.where(first, -pltpu.roll(x, w - HEAD_DIM // 2, 1), pltpu.roll(x, HEAD_DIM // 2, 1))
    return x * cos + partner * sin


def _nsa_prep_kernel(zq_ref, zkc_ref, zvc_ref, zks_ref, zvs_ref, zkw_ref, zvw_ref, cos_ref, sin_ref, gq_ref,
                     gk_ref, gsq_ref, gsk_ref, q_ref, ck_ref, cv_ref, kaug_ref, vaug_ref, kw_ref, vw_ref,
                     *, tm, ngrp, nblk):
    cos1 = cos_ref[0]
    sin1 = sin_ref[0]
    wq = zq_ref.shape[-1]
    cosq = jnp.concatenate([cos1] * (wq // LANES), axis=1)
    sinq = jnp.concatenate([sin1] * (wq // LANES), axis=1)
    q = _rope_lanes(_head_norm(zq_ref[0], gsq_ref, gq_ref[...]), cosq, sinq)
    q_ref[0] = q * (HEAD_DIM ** -0.5)
    ks = _rope_lanes(_head_norm(zks_ref[0], gsk_ref, gk_ref[...]), cos1, sin1)
    kw = _rope_lanes(_head_norm(zkw_ref[0], gsk_ref, gk_ref[...]), cos1, sin1)
    t0 = pl.program_id(1) * tm
    tok = t0 + lax.broadcasted_iota(I32, (tm, nblk), 0)
    blk = lax.broadcasted_iota(I32, (tm, nblk), 1)
    onehot = jnp.where(_div_pow2(tok, SLC_BLOCK) == blk, 1.0, 0.0).astype(BF16)
    vs = zvs_ref[0]
    vw = zvw_ref[0]
    lane = lax.broadcasted_iota(I32, (tm, LANES - HEAD_DIM), 1)
    ones_col = jnp.where(lane == 0, 1.0, 0.0).astype(BF16)
    for g in range(ngrp):
        sl = slice(g * HEAD_DIM, (g + 1) * HEAD_DIM)
        kaug_ref[0, g] = jnp.concatenate([onehot, ks[:, sl].astype(BF16)], axis=1)
        vaug_ref[0, g] = jnp.concatenate([vs[:, sl].astype(BF16), ones_col], axis=1)
        kw_ref[0, g] = kw[:, sl].astype(BF16)
        vw_ref[0, g] = jnp.concatenate([vw[:, sl].astype(BF16), ones_col], axis=1)
    for j in range(CMP_STRIDE):
        rows = pl.ds(j, tm // CMP_STRIDE, stride=CMP_STRIDE)
        js = slice(j * HEAD_DIM, (j + 1) * HEAD_DIM)
        kj = zkc_ref[0, rows, :]
        vj = zvc_ref[0, rows, :]
        for g in range(ngrp):
            sl = slice(g * HEAD_DIM, (g + 1) * HEAD_DIM)
            ck_ref[0, g, :, js] = kj[:, sl]
            cv_ref[0, g, :, js] = vj[:, sl]


ZF_Q_BLOCK = 0
ZF_KCMP, ZF_VCMP, ZF_KSLC, ZF_VSLC, ZF_KWIN, ZF_VWIN, ZF_GATE = 2, 3, 4, 5, 6, 7, 8


def _rope_kernel(ang_ref, cos_ref, sin_ref):
    ang = ang_ref[...]
    cos_ref[...] = jnp.cos(ang)
    sin_ref[...] = jnp.sin(ang)


def _rope_tables(positions):
    bsz, t = positions.shape
    half = HEAD_DIM // 2
    inv_freq = jnp.power(jnp.float32(ROPE_THETA), -jnp.arange(half, dtype=F32) / half)
    ang = (positions.astype(F32)[..., None] * inv_freq).reshape(bsz, t * half // LANES, LANES)
    rows = ang.shape[1]
    spec = pl.BlockSpec((1, rows, LANES), lambda b: (b, 0, 0))
    cos, sin = pl.pallas_call(
        _rope_kernel,
        out_shape=(jax.ShapeDtypeStruct(ang.shape, F32), jax.ShapeDtypeStruct(ang.shape, F32)),
        grid=(bsz,),
        in_specs=[spec],
        out_specs=(spec, spec),
        compiler_params=_cparams(("arbitrary",)),
        name="rope_tables",
    )(ang)
    widen = lambda x: jnp.tile(x.reshape(bsz, t, half), (1, 1, LANES // half))
    return widen(cos), widen(sin)


def _nsa_prep(zf, rope, gq, gk, wq, tm):
    bsz, t, _ = zf.shape
    cos, sin = rope
    ngrp = NSA_KV_HEADS
    nblk = t // SLC_BLOCK
    wk = ngrp * HEAD_DIM
    assert wk == LANES and wq == 2 * wk
    lane_q = jnp.arange(wq) // HEAD_DIM
    gsq = (lane_q[:, None] == lane_q[None, :]).astype(BF16)
    lane_k = jnp.arange(wk) // HEAD_DIM
    gsk = (lane_k[:, None] == lane_k[None, :]).astype(BF16)
    kvspec = lambda blk: pl.BlockSpec((1, tm, wk), lambda b, i: (b, i, blk))
    hspec = lambda w: pl.BlockSpec((1, ngrp, tm, w), lambda b, i: (b, 0, i, 0))
    cspec = pl.BlockSpec((1, ngrp, tm // CMP_STRIDE, CMP_STRIDE * HEAD_DIM), lambda b, i: (b, 0, i, 0))
    return pl.pallas_call(
        functools.partial(_nsa_prep_kernel, tm=tm, ngrp=ngrp, nblk=nblk),
        out_shape=(jax.ShapeDtypeStruct((bsz, t, wq), F32),
                   jax.ShapeDtypeStruct((bsz, ngrp, t // CMP_STRIDE, CMP_STRIDE * HEAD_DIM), F32),
                   jax.ShapeDtypeStruct((bsz, ngrp, t // CMP_STRIDE, CMP_STRIDE * HEAD_DIM), F32),
                   jax.ShapeDtypeStruct((bsz, ngrp, t, HEAD_DIM + nblk), BF16),
                   jax.ShapeDtypeStruct((bsz, ngrp, t, LANES), BF16),
                   jax.ShapeDtypeStruct((bsz, ngrp, t, HEAD_DIM), BF16),
                   jax.ShapeDtypeStruct((bsz, ngrp, t, LANES), BF16)),
        grid=(bsz, t // tm),
        in_specs=[pl.BlockSpec((1, tm, wq), lambda b, i: (b, i, ZF_Q_BLOCK)),
                  kvspec(ZF_KCMP), kvspec(ZF_VCMP),
                  kvspec(ZF_KSLC), kvspec(ZF_VSLC), kvspec(ZF_KWIN), kvspec(ZF_VWIN),
                  pl.BlockSpec((1, tm, LANES), lambda b, i: (b, i, 0)),
                  pl.BlockSpec((1, tm, LANES), lambda b, i: (b, i, 0)),
                  pl.BlockSpec((1, wq), lambda b, i: (0, 0)),
                  pl.BlockSpec((1, wk), lambda b, i: (0, 0)),
                  pl.BlockSpec((wq, wq), lambda b, i: (0, 0)),
                  pl.BlockSpec((wk, wk), lambda b, i: (0, 0))],
        out_specs=(pl.BlockSpec((1, tm, wq), lambda b, i: (b, i, 0)),
                   cspec, cspec,
                   hspec(HEAD_DIM + nblk), hspec(LANES), hspec(HEAD_DIM), hspec(LANES)),
        compiler_params=_cparams(("arbitrary", "arbitrary")),
        name="nsa_prep",
    )(zf, zf, zf, zf, zf, zf, zf, cos, sin, jnp.tile(gq, wq // HEAD_DIM)[None, :], jnp.tile(gk, ngrp)[None, :],
      gsq, gsk)


def _compress_kernel(ck_ref, cv_ref, w1k_ref, w2k_ref, w1v_ref, w2v_ref, pek_ref, pev_ref,
                     gk_ref, angc_ref, kc_ref, vc_ref, *, nch, half):
    def mlp(c_ref, w1_ref, w2_ref, pe_ref):
        x = c_ref[0, 0]
        w1 = w1_ref[...]
        first = _dot3(x, w1[:half])
        second = _dot3(x, w1[half:])
        bias = _dot3(pe_ref[...], w1)[0:1]
        hid = _silu(first + pltpu.roll(second, nch - 1, 0) + bias)
        return _dot3(hid, w2_ref[...])

    kc = mlp(ck_ref, w1k_ref, w2k_ref, pek_ref)
    kc = kc * _rms_scale(kc) * gk_ref[...]
    ang = angc_ref[0]
    half_d = HEAD_DIM // 2
    k1 = kc[:, :half_d]
    k2 = kc[:, half_d:]
    cos = jnp.cos(ang[:, :half_d])
    sin = jnp.sin(ang[:, :half_d])
    kc_ref[0, 0] = jnp.concatenate([k1 * cos - k2 * sin, k2 * cos + k1 * sin], axis=1)
    vc_ref[0, 0] = mlp(cv_ref, w1v_ref, w2v_ref, pev_ref)


def _compress(ck, cv, w1k, w2k, w1v, w2v, pos_k, pos_v, gk, angc):
    bsz, ngrp, nch, half = ck.shape
    hid = w1k.shape[1]

    def pe_rows(pe):
        return jnp.zeros((SUBLANES, 2 * half), F32).at[0].set(pe.reshape(-1))

    blk4 = lambda b, g: (b, g, 0, 0)
    full2 = lambda b, g: (0, 0)
    return pl.pallas_call(
        functools.partial(_compress_kernel, nch=nch, half=half),
        out_shape=(jax.ShapeDtypeStruct((bsz, ngrp, nch, HEAD_DIM), F32),
                   jax.ShapeDtypeStruct((bsz, ngrp, nch, HEAD_DIM), F32)),
        grid=(bsz, ngrp),
        in_specs=[pl.BlockSpec((1, 1, nch, half), blk4),
                  pl.BlockSpec((1, 1, nch, half), blk4),
                  pl.BlockSpec((2 * half, hid), full2),
                  pl.BlockSpec((hid, HEAD_DIM), full2),
                  pl.BlockSpec((2 * half, hid), full2),
                  pl.BlockSpec((hid, HEAD_DIM), full2),
                  pl.BlockSpec((SUBLANES, 2 * half), full2),
                  pl.BlockSpec((SUBLANES, 2 * half), full2),
                  pl.BlockSpec((1, HEAD_DIM), full2),
                  pl.BlockSpec((1, nch, HEAD_DIM), lambda b, g: (b, 0, 0))],
        out_specs=(pl.BlockSpec((1, 1, nch, HEAD_DIM), blk4),
                   pl.BlockSpec((1, 1, nch, HEAD_DIM), blk4)),
        compiler_params=_cparams(("arbitrary", "arbitrary")),
        name="nsa_compress",
    )(ck, cv, w1k, w2k, w1v, w2v, pe_rows(pos_k), pe_rows(pos_v), gk[None, :], angc)


def _gate_col(gates, head, branch):
    c = head * 3 + branch
    if isinstance(c, int):
        return gates[:, c:c + 1]
    lane = lax.broadcasted_iota(I32, gates.shape, 1)
    return jnp.sum(jnp.where(lane == c, gates, 0.0), axis=1, keepdims=True)


def _cmp_attn_kernel(q_ref, kc_ref, vct_ref, gl_ref, ovt_ref, oc_ref, sel_ref, *, tq, ngrp, hpg, nch, nblk):
    t = pl.program_id(1) * tq + lax.broadcasted_iota(I32, (1, tq), 1)
    cmp_end = lax.broadcasted_iota(I32, (nch, 1), 0) * CMP_STRIDE + (CMP_LEN - 1)
    visible = cmp_end <= t
    q = q_ref[0]
    gates = jax.nn.sigmoid(gl_ref[0])
    blk = lax.broadcasted_iota(I32, (nblk, tq), 0)
    cur = _div_pow2(t, SLC_BLOCK)
    forced = (blk == 0) | (blk == cur) | (blk == cur - 1)
    causal = blk <= cur
    lane = lax.broadcasted_iota(I32, (tq, hpg * HEAD_DIM), 1)
    outs = []
    for g in range(ngrp):
        kc = kc_ref[0, g]
        vct = vct_ref[0, g]
        psum = jnp.zeros((nch, tq), F32)
        o_t = []
        for hh in range(hpg):
            head = g * hpg + hh
            qh = q[:, head * HEAD_DIM:(head + 1) * HEAD_DIM]
            s = jnp.where(visible, _dot3_nt(kc, qh), NEG_INF)
            p = jnp.exp(s - jnp.maximum(jnp.max(s, axis=0, keepdims=True), 0.1 * NEG_INF))
            denom = jnp.sum(p, axis=0, keepdims=True)
            p = p * (1.0 / jnp.where(denom == 0.0, 1.0, denom))
            psum = psum + p
            o_t.append(_dot(vct, p.astype(BF16)))
        gate = _gate_col(gates, g * hpg, 0)
        for hh in range(1, hpg):
            gate = jnp.where(lane < hh * HEAD_DIM, gate, _gate_col(gates, g * hpg + hh, 0))
        outs.append(jnp.concatenate(o_t, axis=0).T * gate)
        hi, lo = _split(psum)
        imp = _dot(ovt_ref[...], hi) + _dot(ovt_ref[...], lo)
        val = jnp.where(causal, imp + jnp.where(forced, FORCE_BONUS, 0.0), NEG_INF)
        for _ in range(min(SLC_TOPK, nblk)):
            m = jnp.max(val, axis=0, keepdims=True)
            first = jnp.min(jnp.where(val == m, blk, nblk), axis=0, keepdims=True)
            val = jnp.where(blk == first, -jnp.inf, val)
        sel_ref[0, g] = jnp.where(val == -jnp.inf, 0.0, SEL_MASK_BIAS).T.astype(BF16)
    oc_ref[0] = jnp.concatenate(outs, axis=1).astype(oc_ref.dtype)


def _cmp_attn(q, kc, vc, glog, tq):
    bsz, t, wq = q.shape
    ngrp, nch = kc.shape[1], kc.shape[2]
    hpg = wq // HEAD_DIM // ngrp
    nblk = t // SLC_BLOCK
    cs = jnp.arange(nch) * CMP_STRIDE
    ss = jnp.arange(nblk) * SLC_BLOCK
    ovt = ((cs[None, :] < ss[:, None] + SLC_BLOCK) & (cs[None, :] + CMP_LEN > ss[:, None])).astype(BF16)
    return pl.pallas_call(
        functools.partial(_cmp_attn_kernel, tq=tq, ngrp=ngrp, hpg=hpg, nch=nch, nblk=nblk),
        out_shape=(jax.ShapeDtypeStruct((bsz, t, wq), BF16),
                   jax.ShapeDtypeStruct((bsz, ngrp, t, nblk), BF16)),
        grid=(bsz, t // tq),
        in_specs=[pl.BlockSpec((1, tq, wq), lambda b, i: (b, i, 0)),
                  pl.BlockSpec((1, ngrp, nch, HEAD_DIM), lambda b, i: (b, 0, 0, 0)),
                  pl.BlockSpec((1, ngrp, HEAD_DIM, nch), lambda b, i: (b, 0, 0, 0)),
                  pl.BlockSpec((1, tq, GATE_PAD), lambda b, i: (b, i, ZF_GATE)),
                  pl.BlockSpec((nblk, nch), lambda b, i: (0, 0))],
        out_specs=(pl.BlockSpec((1, tq, wq), lambda b, i: (b, i, 0)),
                   pl.BlockSpec((1, ngrp, tq, nblk), lambda b, i: (b, 0, i, 0))),
        compiler_params=_cparams(("arbitrary", "arbitrary")),
        name="nsa_cmp_select",
    )(q, kc, vc.transpose(0, 1, 3, 2).astype(BF16), glog, ovt)


SEL_ROW_CHUNK = 128


def _sel_attn_kernel(q_ref, sel_ref, kaug_ref, vaug_ref, gl_ref, o_ref, qa_s, m_s, acc, s_a, s_b, *, tq, tk, hpg):
    g = pl.program_id(1)
    qi = pl.program_id(2)
    q = q_ref[0]
    sel = sel_ref[0, 0]
    for hh in range(hpg):
        qa_s[hh * tq:(hh + 1) * tq, :] = jnp.concatenate(
            [sel, q[:, hh * HEAD_DIM:(hh + 1) * HEAD_DIM].astype(BF16)], axis=1)
    m_s[...] = jnp.full_like(m_s, NEG_INF)
    acc[...] = jnp.zeros_like(acc)
    rows = hpg * tq
    rc = SEL_ROW_CHUNK

    def scores(kt, buf):
        k0 = pl.multiple_of(kt * tk, tk)
        kblk = kaug_ref[0, 0, pl.ds(k0, tk), :]
        for c in range(rows // rc):
            rs = slice(c * rc, (c + 1) * rc)
            buf[rs, :] = _dot_nt(qa_s[rs, :], kblk)

    def consume(kt, buf, diag):
        k0 = pl.multiple_of(kt * tk, tk)
        vblk = vaug_ref[0, 0, pl.ds(k0, tk), :]
        for c in range(rows // rc):
            rs = slice(c * rc, (c + 1) * rc)
            s = buf[rs, :]
            if diag:
                t = qi * tq + _mod_pow2(c * rc + lax.broadcasted_iota(I32, (rc, tk), 0), tq)
                s = jnp.where(k0 + lax.broadcasted_iota(I32, (rc, tk), 1) <= t, s, NEG_INF)
            m_old = m_s[rs, :]
            m_new = jnp.maximum(m_old, jnp.max(s, axis=1, keepdims=True))
            alpha = jnp.exp(m_old - m_new)
            p = jnp.exp(s - m_new[:, :1])
            acc[rs, :] = alpha * acc[rs, :] + _dot(p.astype(BF16), vblk)
            m_s[rs, :] = m_new

    last = (qi * tq) // tk
    scores(0, s_a)

    def two_tiles(k):
        scores(k + 1, s_b)
        consume(k, s_a, False)
        scores(k + 2, s_a)
        consume(k + 1, s_b, False)

    def quad(j, carry):
        two_tiles(4 * j)
        two_tiles(4 * j + 2)
        return carry

    def pair(j, carry):
        two_tiles(4 * (last // 4) + 2 * j)
        return carry

    lax.fori_loop(0, last // 4, quad, 0)
    lax.fori_loop(0, (last % 4) // 2, pair, 0)
    t0 = (last // 2) * 2

    @pl.when(t0 < last)
    def _():
        scores(last, s_b)
        consume(t0, s_a, False)
        consume(last, s_b, True)

    @pl.when(t0 == last)
    def _():
        consume(last, s_a, True)

    gates = jax.nn.sigmoid(gl_ref[0])
    a = acc[...]
    out = a[:, :HEAD_DIM] / a[:, HEAD_DIM:HEAD_DIM + 1]
    o_ref[0] = jnp.concatenate(
        [out[hh * tq:(hh + 1) * tq] * _gate_col(gates, g * hpg + hh, 1) for hh in range(hpg)],
        axis=1).astype(o_ref.dtype)


def _sel_attn(q, sel, kaug, vaug, glog, tq, tk):
    bsz, t, wq = q.shape
    ngrp = kaug.shape[1]
    hpg = wq // HEAD_DIM // ngrp
    gw = hpg * HEAD_DIM
    nblk = sel.shape[-1]
    ka = kaug.shape[-1]
    rows = hpg * tq
    assert tk % tq == 0 and rows % SEL_ROW_CHUNK == 0
    return pl.pallas_call(
        functools.partial(_sel_attn_kernel, tq=tq, tk=tk, hpg=hpg),
        out_shape=jax.ShapeDtypeStruct((bsz, t, wq), BF16),
        grid=(bsz, ngrp, t // tq),
        in_specs=[pl.BlockSpec((1, tq, gw), lambda b, g, i: (b, i, g)),
                  pl.BlockSpec((1, 1, tq, nblk), lambda b, g, i: (b, g, i, 0)),
                  pl.BlockSpec((1, 1, t, ka), lambda b, g, i: (b, g, 0, 0)),
                  pl.BlockSpec((1, 1, t, LANES), lambda b, g, i: (b, g, 0, 0)),
                  pl.BlockSpec((1, tq, GATE_PAD), lambda b, g, i: (b, i, ZF_GATE))],
        out_specs=pl.BlockSpec((1, tq, gw), lambda b, g, i: (b, i, g)),
        scratch_shapes=[pltpu.VMEM((rows, ka), BF16), pltpu.VMEM((rows, LANES), F32),
                        pltpu.VMEM((rows, LANES), F32),
                        pltpu.VMEM((rows, tk), F32), pltpu.VMEM((rows, tk), F32)],
        compiler_params=_cparams(("arbitrary", "arbitrary", "arbitrary")),
        name="nsa_selected",
    )(q, sel, kaug, vaug, glog)


def _win_attn_kernel(q_ref, k_ref, v_ref, gl_ref, bias_ref, o_ref, *, tq, hpg, span):
    g = pl.program_id(1)
    qi = pl.program_id(2)
    q = q_ref[0]
    q2 = jnp.concatenate([q[:, hh * HEAD_DIM:(hh + 1) * HEAD_DIM] for hh in range(hpg)], axis=0).astype(BF16)
    base = pl.multiple_of(jnp.maximum(qi * tq - WINDOW, 0), tq)
    kblk = k_ref[0, 0, pl.ds(base, span), :]
    vblk = v_ref[0, 0, pl.ds(base, span), :]
    gates = jax.nn.sigmoid(gl_ref[0])

    def attend(bias):
        rc = SEL_ROW_CHUNK
        chunks = [slice(c * rc, (c + 1) * rc) for c in range(hpg * tq // rc)]
        scores = [_dot_nt(q2[rs], kblk) + bias[rs] for rs in chunks]
        outs = []
        for s in scores:
            p = jnp.exp(s - jnp.max(s, axis=1, keepdims=True))
            acc = _dot(p.astype(BF16), vblk)
            outs.append(acc[:, :HEAD_DIM] / acc[:, HEAD_DIM:HEAD_DIM + 1])
        out = jnp.concatenate(outs, axis=0)
        o_ref[0] = jnp.concatenate(
            [out[hh * tq:(hh + 1) * tq] * _gate_col(gates, g * hpg + hh, 2) for hh in range(hpg)],
            axis=1).astype(o_ref.dtype)

    @pl.when(qi * tq >= WINDOW)
    def _():
        attend(jnp.concatenate([bias_ref[...]] * hpg, axis=0))

    @pl.when(qi * tq < WINDOW)
    def _():
        rows = hpg * tq
        t = qi * tq + _mod_pow2(lax.broadcasted_iota(I32, (rows, span), 0), tq)
        wpos = lax.broadcasted_iota(I32, (rows, span), 1)
        attend(jnp.where((wpos <= t) & (wpos > t - WINDOW), 0.0, NEG_INF))


def _win_attn(q, kw, vw, glog, tq):
    bsz, t, wq = q.shape
    ngrp = kw.shape[1]
    hpg = wq // HEAD_DIM // ngrp
    gw = hpg * HEAD_DIM
    span = WINDOW + tq
    assert WINDOW % tq == 0 and t >= span
    rel = jnp.arange(span)[None, :] - WINDOW - jnp.arange(tq)[:, None]
    band = jnp.where((rel <= 0) & (rel > -WINDOW), 0.0, NEG_INF).astype(F32)
    return pl.pallas_call(
        functools.partial(_win_attn_kernel, tq=tq, hpg=hpg, span=span),
        out_shape=jax.ShapeDtypeStruct((bsz, t, wq), BF16),
        grid=(bsz, ngrp, t // tq),
        in_specs=[pl.BlockSpec((1, tq, gw), lambda b, g, i: (b, i, g)),
                  pl.BlockSpec((1, 1, t, HEAD_DIM), lambda b, g, i: (b, g, 0, 0)),
                  pl.BlockSpec((1, 1, t, LANES), lambda b, g, i: (b, g, 0, 0)),
                  pl.BlockSpec((1, tq, GATE_PAD), lambda b, g, i: (b, i, ZF_GATE)),
                  pl.BlockSpec((tq, span), lambda b, g, i: (0, 0))],
        out_specs=pl.BlockSpec((1, tq, gw), lambda b, g, i: (b, i, g)),
        compiler_params=_cparams(("arbitrary", "arbitrary", "arbitrary")),
        name="nsa_window",
    )(q, kw, vw, glog, band)


def _out_kernel(h_ref, yc_ref, ys_ref, yb_ref, oc_ref, os_ref, ow_ref, mod_ref, wglu_ref, w_ref, o_ref):
    g = _gelu_tanh(ys_ref[0])
    y_ssm = g * jax.nn.sigmoid(_dot(g.astype(BF16), wglu_ref[...]))
    nsa = oc_ref[0].astype(F32) + os_ref[0].astype(F32) + ow_ref[0].astype(F32)
    y = jnp.concatenate([yc_ref[0], y_ssm.astype(BF16), yb_ref[0], nsa.astype(BF16)], axis=1)
    o_ref[0] = h_ref[0] + mod_ref[0, 2:3, :] * _dot(y, w_ref[...])


def _out_proj(h, parts, mod, w_glu_bf16, w_bf16, tm):
    bsz, t, d = h.shape
    wp = parts[0].shape[-1]
    tok = lambda b, i: (b, i, 0)
    return pl.pallas_call(
        _out_kernel,
        out_shape=jax.ShapeDtypeStruct((bsz, t, d), F32),
        grid=(bsz, t // tm),
        in_specs=[pl.BlockSpec((1, tm, d), tok)] + [pl.BlockSpec((1, tm, wp), tok)] * 6
        + [pl.BlockSpec((1, 6, d), lambda b, i: (b, 0, 0)),
           pl.BlockSpec((wp, wp), lambda b, i: (0, 0)),
           pl.BlockSpec((d, d), lambda b, i: (0, 0))],
        out_specs=pl.BlockSpec((1, tm, d), tok),
        compiler_params=_cparams(("arbitrary", "arbitrary")),
        name="out_proj",
    )(h, *parts, mod, w_glu_bf16, w_bf16)


def _ffn_kernel(h_ref, g_ref, mod_ref, wg_ref, wu_ref, wd_ref, o_ref, a_s, acc):
    f = pl.program_id(2)

    @pl.when(f == 0)
    def _():
        x = h_ref[0]
        y = x * _rms_scale(x) * g_ref[...]
        a_s[...] = (y * (1.0 + mod_ref[0, 4:5, :]) + mod_ref[0, 3:4, :]).astype(BF16)
        acc[...] = jnp.zeros_like(acc)

    a = a_s[...]
    hid = _silu(_dot(a, wg_ref[...])) * _dot(a, wu_ref[...])
    acc[...] += _dot(hid.astype(BF16), wd_ref[...])

    @pl.when(f == pl.num_programs(2) - 1)
    def _():
        o_ref[0] = h_ref[0] + mod_ref[0, 5:6, :] * acc[...]


def _dense_ffn(h, g, mod, wg, wu, wd, tm, tf):
    bsz, t, d = h.shape
    f = wg.shape[1]
    tok = lambda b, i, j: (b, i, 0)
    return pl.pallas_call(
        _ffn_kernel,
        out_shape=jax.ShapeDtypeStruct((bsz, t, d), F32),
        grid=(bsz, t // tm, f // tf),
        in_specs=[pl.BlockSpec((1, tm, d), tok),
                  pl.BlockSpec((1, d), lambda b, i, j: (0, 0)),
                  pl.BlockSpec((1, 6, d), lambda b, i, j: (b, 0, 0)),
                  pl.BlockSpec((d, tf), lambda b, i, j: (0, j)),
                  pl.BlockSpec((d, tf), lambda b, i, j: (0, j)),
                  pl.BlockSpec((tf, d), lambda b, i, j: (j, 0))],
        out_specs=pl.BlockSpec((1, tm, d), tok),
        scratch_shapes=[pltpu.VMEM((tm, d), BF16), pltpu.VMEM((tm, d), F32)],
        compiler_params=_cparams(("arbitrary", "arbitrary", "arbitrary")),
        name="dense_swiglu",
    )(h, g[None, :], mod, wg, wu, wd)


MOE_CHUNK = 256


def _moe_route_kernel(h_ref, g_ref, mod_ref, wrt_ref, a_ref, gates_ref, rank_ref, cnt_ref, *, tm):
    x = h_ref[0]
    y = x * _rms_scale(x) * g_ref[...]
    a = y * (1.0 + mod_ref[0, 4:5, :]) + mod_ref[0, 3:4, :]
    a_ref[0] = a.astype(BF16)
    logits = _dot3_nt(wrt_ref[...], a)
    row = lax.broadcasted_iota(I32, logits.shape, 0)
    v1 = jnp.max(logits, axis=0, keepdims=True)
    i1 = jnp.min(jnp.where(logits == v1, row, N_EXPERTS), axis=0, keepdims=True)
    rest = jnp.where(row == i1, -jnp.inf, logits)
    v2 = jnp.max(rest, axis=0, keepdims=True)
    i2 = jnp.min(jnp.where(rest == v2, row, N_EXPERTS), axis=0, keepdims=True)
    e2 = jnp.exp(v2 - v1)
    gates_ref[0] = jnp.where(row == i1, 1.0 / (1.0 + e2), 0.0) + jnp.where(row == i2, e2 / (1.0 + e2), 0.0)
    routed = jnp.where(row == i1, 1.0, jnp.where(row == i2, 1.0, 0.0))
    si = lax.broadcasted_iota(I32, (MOE_CHUNK, MOE_CHUNK), 0)
    sj = lax.broadcasted_iota(I32, (MOE_CHUNK, MOE_CHUNK), 1)
    before = jnp.where(si < sj, 1.0, 0.0).astype(BF16)
    carry = jnp.zeros((N_EXPERTS, 1), F32)
    lane = lax.broadcasted_iota(I32, (N_EXPERTS, LANES), 1)
    starts = jnp.zeros((N_EXPERTS, LANES), F32)
    parts = []
    nsub = tm // MOE_CHUNK
    for c in range(nsub):
        rc = routed[:, c * MOE_CHUNK:(c + 1) * MOE_CHUNK]
        parts.append(_dot(rc.astype(BF16), before) + carry)
        starts = jnp.where(lane == c, carry, starts)
        carry = carry + jnp.sum(rc, axis=1, keepdims=True)
    rank = jnp.concatenate(parts, axis=1)
    rank_ref[0] = jnp.where(routed > 0.0, rank, -1.0)
    cnt_ref[0] = jnp.where(lane == nsub, carry, starts).astype(I32)


MOE_WINDOW = 5


def _moe_expert_kernel(cnt_ref, a_ref, rank_t_ref, rank_c_ref, gate_c_ref, wg_ref, wu_ref, wd_ref, o_ref,
                       xc, yacc, rc_s, gc_s, *, tm, ne):
    i = pl.program_id(0)
    e = pl.program_id(1)
    f = pl.program_id(2)
    nsub = tm // MOE_CHUNK
    win = min(MOE_WINDOW, nsub)
    half = MOE_CHUNK // 2
    base = (i * ne + e) * (nsub + 1)
    total = cnt_ref[base + nsub]
    rem = total % MOE_CHUNK
    nbig = total // MOE_CHUNK + (rem > half).astype(I32)
    has_tail = jnp.logical_and(rem > 0, rem <= half)
    tail0 = pl.multiple_of(nbig * MOE_CHUNK, MOE_CHUNK)
    d = o_ref.shape[-1]

    def big(fn):
        def body(c, carry):
            fn(pl.multiple_of(c * MOE_CHUNK, MOE_CHUNK), MOE_CHUNK)
            return carry

        lax.fori_loop(0, nbig, body, 0)

        @pl.when(has_tail)
        def _():
            fn(tail0, half)

    def window(r0, nrows):
        first = jnp.int32(0)
        stop = jnp.int32(0)
        for s in range(nsub):
            first += (cnt_ref[base + s + 1] <= r0).astype(I32)
            stop += (cnt_ref[base + s] < r0 + nrows).astype(I32)
        s0 = jnp.minimum(first, nsub - win)
        return s0, stop <= s0 + win

    def pick_rows(r0, nrows, s0, nsubs):
        tgt = (r0 + lax.broadcasted_iota(I32, (nrows, MOE_CHUNK), 0)).astype(F32)
        return jnp.concatenate(
            [jnp.where(rank_t_ref[0, e, pl.ds(s0 + j, 1), :] == tgt, 1.0, 0.0).astype(BF16)
             for j in range(nsubs)], axis=1)

    @pl.when(jnp.logical_and(e == 0, f == 0))
    def _():
        o_ref[...] = jnp.zeros_like(o_ref)

    @pl.when(f == 0)
    def _():
        def gather(r0, nrows):
            s0, fits = window(r0, nrows)

            @pl.when(fits)
            def _():
                t0 = pl.multiple_of(s0 * MOE_CHUNK, MOE_CHUNK)
                xc[pl.ds(r0, nrows), :] = _dot(pick_rows(r0, nrows, s0, win),
                                               a_ref[pl.ds(t0, win * MOE_CHUNK), :]).astype(BF16)

            @pl.when(jnp.logical_not(fits))
            def _():
                xc[pl.ds(r0, nrows), :] = _dot(pick_rows(r0, nrows, 0, nsub), a_ref[...]).astype(BF16)

            yacc[pl.ds(r0, nrows), :] = jnp.zeros((nrows, d), F32)

        big(gather)

    def ffn(r0, nrows):
        rows = [pl.ds(r0, half), pl.ds(r0 + half, half)] if nrows > half else [pl.ds(r0, nrows)]
        xs = [xc[r, :] for r in rows]
        gate_up = [(_dot(x, wg_ref[0]), _dot(x, wu_ref[0])) for x in xs]
        for r, (gt, up) in zip(rows, gate_up):
            yacc[r, :] += _dot((_silu(gt) * up).astype(BF16), wd_ref[0])

    big(ffn)

    @pl.when(f == pl.num_programs(2) - 1)
    def _():
        lane = lax.broadcasted_iota(I32, (tm, ne), 1)
        rc_s[...] = jnp.sum(jnp.where(lane == e, rank_c_ref[0], 0.0), axis=1, keepdims=True)
        gc_s[...] = jnp.sum(jnp.where(lane == e, gate_c_ref[0], 0.0), axis=1, keepdims=True)

        def put(r0, nrows, t0, ntok):
            rows = pl.ds(t0, ntok)
            tgt = (r0 + lax.broadcasted_iota(I32, (ntok, nrows), 1)).astype(F32)
            place = jnp.where(rc_s[rows, :] == tgt, 1.0, 0.0).astype(BF16)
            o_ref[rows, :] += gc_s[rows, :] * _dot(place, yacc[pl.ds(r0, nrows), :].astype(BF16))

        def scatter(r0, nrows):
            s0, fits = window(r0, nrows)

            @pl.when(fits)
            def _():
                put(r0, nrows, pl.multiple_of(s0 * MOE_CHUNK, MOE_CHUNK), win * MOE_CHUNK)

            @pl.when(jnp.logical_not(fits))
            def _():
                put(r0, nrows, 0, tm)

        big(scatter)


def _residual_kernel(h_ref, y_ref, mod_ref, o_ref):
    o_ref[0] = h_ref[0] + mod_ref[0, 5:6, :] * y_ref[0]


def _moe_ffn(h, g, mod, router, wg, wu, wd, tm, tf):
    bsz, t, d = h.shape
    ne, _, f = wg.shape
    assert ne == N_EXPERTS and tm % MOE_CHUNK == 0
    nt = t // tm
    tok = lambda b, i: (b, i, 0)
    a, gates_t, rank_t, cnt = pl.pallas_call(
        functools.partial(_moe_route_kernel, tm=tm),
        out_shape=(jax.ShapeDtypeStruct((bsz, t, d), BF16),
                   jax.ShapeDtypeStruct((bsz * nt, ne, tm), F32),
                   jax.ShapeDtypeStruct((bsz * nt, ne, tm), F32),
                   jax.ShapeDtypeStruct((bsz * nt, ne, LANES), I32)),
        grid=(bsz, nt),
        in_specs=[pl.BlockSpec((1, tm, d), tok),
                  pl.BlockSpec((1, d), lambda b, i: (0, 0)),
                  pl.BlockSpec((1, 6, d), lambda b, i: (b, 0, 0)),
                  pl.BlockSpec((ne, d), lambda b, i: (0, 0))],
        out_specs=(pl.BlockSpec((1, tm, d), tok),
                   pl.BlockSpec((1, ne, tm), lambda b, i: (b * nt + i, 0, 0)),
                   pl.BlockSpec((1, ne, tm), lambda b, i: (b * nt + i, 0, 0)),
                   pl.BlockSpec((1, ne, LANES), lambda b, i: (b * nt + i, 0, 0))),
        compiler_params=_cparams(("arbitrary", "arbitrary")),
        name="moe_route",
    )(h, g[None, :], mod, router.T)
    nsub = tm // MOE_CHUNK
    counts = cnt[:, :, :nsub + 1].reshape(-1)
    y = pl.pallas_call(
        functools.partial(_moe_expert_kernel, tm=tm, ne=ne),
        out_shape=jax.ShapeDtypeStruct((bsz * t, d), F32),
        grid_spec=pltpu.PrefetchScalarGridSpec(
            num_scalar_prefetch=1,
            grid=(bsz * nt, ne, f // tf),
            in_specs=[pl.BlockSpec((tm, d), lambda i, e, j, cnt: (i, 0)),
                      pl.BlockSpec((1, ne, nsub, MOE_CHUNK), lambda i, e, j, cnt: (i, 0, 0, 0)),
                      pl.BlockSpec((1, tm, ne), lambda i, e, j, cnt: (i, 0, 0)),
                      pl.BlockSpec((1, tm, ne), lambda i, e, j, cnt: (i, 0, 0)),
                      pl.BlockSpec((1, d, tf), lambda i, e, j, cnt: (e, 0, j)),
                      pl.BlockSpec((1, d, tf), lambda i, e, j, cnt: (e, 0, j)),
                      pl.BlockSpec((1, tf, d), lambda i, e, j, cnt: (e, j, 0))],
            out_specs=pl.BlockSpec((tm, d), lambda i, e, j, cnt: (i, 0)),
            scratch_shapes=[pltpu.VMEM((tm, d), BF16), pltpu.VMEM((tm, d), F32),
                            pltpu.VMEM((tm, 1), F32), pltpu.VMEM((tm, 1), F32)]),
        compiler_params=_cparams(("arbitrary", "arbitrary", "arbitrary")),
        name="moe_experts",
    )(counts, a.reshape(bsz * t, d), rank_t.reshape(bsz * nt, ne, nsub, MOE_CHUNK), rank_t.transpose(0, 2, 1),
      gates_t.transpose(0, 2, 1), wg, wu, wd)
    tr = _Tiles().proj
    return pl.pallas_call(
        _residual_kernel,
        out_shape=jax.ShapeDtypeStruct((bsz, t, d), F32),
        grid=(bsz, t // tr),
        in_specs=[pl.BlockSpec((1, tr, d), tok), pl.BlockSpec((1, tr, d), tok),
                  pl.BlockSpec((1, 6, d), lambda b, i: (b, 0, 0))],
        out_specs=pl.BlockSpec((1, tr, d), tok),
        compiler_params=_cparams(("arbitrary", "arbitrary")),
        name="moe_residual",
    )(h, y.reshape(bsz, t, d), mod)


class _Tiles(NamedTuple):
    proj: int = 512
    attn_q: int = 256
    sel_k: int = 512
    s5_rows: int = 256
    ffn_m: int = 1024
    moe_m: int = 2048
    ffn_f: int = 512


def _tiles(t):
    base = _Tiles()
    return base._replace(s5_rows=min(base.s5_rows, t // SSM_CHUNK), ffn_m=min(base.ffn_m, t),
                         moe_m=min(base.moe_m, t))


def _pack_w_in(w_in, dq):
    o1 = 3 * dq
    o2 = o1 + dq
    o3 = o2 + 3 * dq
    sb_q = w_in[:, o2:o2 + dq] * (HEAD_DIM ** -0.5)
    pad = jnp.zeros((w_in.shape[0], GATE_PAD - (w_in.shape[1] - (o3 + dq + 6 * NSA_KV_HEADS * HEAD_DIM))),
                    w_in.dtype)
    cols = [w_in[:, :o1], sb_q, w_in[:, o2 + dq:o3], w_in[:, o1:o2], w_in[:, o3:], pad]
    return jnp.concatenate(cols, axis=1).astype(BF16), o1 + 3 * dq


def _token_mixing(h, mod, positions, rope, g_mix, w_in, w_out, conv_w, ssm, nsa):
    bsz, t, d = h.shape
    dq = d // 4
    tl = _tiles(t)
    w_packed, nb = _pack_w_in(w_in, dq)
    zb, zs, zf = _in_proj(h, g_mix, mod, w_packed, nb, dq, tm=tl.proj)
    y_conv = _short_conv(zb, conv_w, tt=tl.proj)
    y_ssm = _s5(zs, *ssm[:-1], tt=tl.s5_rows)
    y_sb = _stick_breaking(zb, dq, tq=tl.attn_q)
    o_c, o_s, o_w = _nsa_mixer(zf, dq, positions, rope, nsa)
    return _out_proj(h, [y_conv, y_ssm, y_sb, o_c, o_s, o_w], mod, ssm[-1].astype(BF16), w_out.astype(BF16),
                     tm=tl.proj)


def _nsa_mixer(zf, wq, positions, rope, nsa):
    bsz, t, _ = zf.shape
    q_norm_g, k_norm_g, pos_k, pos_v, k_w1, k_w2, v_w1, v_w2 = nsa
    half = HEAD_DIM // 2
    inv_freq = jnp.power(jnp.float32(ROPE_THETA), -jnp.arange(half, dtype=F32) / half)
    tl = _tiles(t)
    q_rot, ck, cv, kaug, vaug, kwin, vwin = _nsa_prep(zf, rope, q_norm_g, k_norm_g, wq, tm=tl.proj)
    nch = t // CMP_STRIDE
    end_idx = jnp.minimum(jnp.arange(nch) * CMP_STRIDE + CMP_LEN - 1, t - 1)
    angc = jnp.tile(positions[:, end_idx].astype(F32)[..., None] * inv_freq, (1, 1, 2))
    kc, vc = _compress(ck, cv, k_w1, k_w2, v_w1, v_w2, pos_k, pos_v, k_norm_g, angc)
    o_c, sel = _cmp_attn(q_rot, kc, vc, zf, tq=tl.attn_q)
    o_s = _sel_attn(q_rot, sel, kaug, vaug, zf, tq=tl.attn_q, tk=tl.sel_k)
    o_w = _win_attn(q_rot, kwin, vwin, zf, tq=tl.attn_q)
    return o_c, o_s, o_w


def kernel(x, c, positions, ada_w, ada_b, norm_mix_g, norm_ffn_g, w_in, w_out, conv_w, ssm_lam_re, ssm_lam_im, ssm_b_re, ssm_b_im, ssm_c_re, ssm_c_im, ssm_d, ssm_log_dt, ssm_w_glu, nsa_q_norm_g, nsa_k_norm_g, cmp_pos_k, cmp_pos_v, cmp_k_w1, cmp_k_w2, cmp_v_w1, cmp_v_w2, ffn_w_gate, ffn_w_up, ffn_w_down, moe_router, moe_w_gate, moe_w_up, moe_w_down):
    depth = ada_w.shape[0]
    mods = _modulation(c, ada_w, ada_b)
    rope = _rope_tables(positions)
    h = x
    for layer in range(depth):
        mod = mods[layer]
        ssm = (ssm_lam_re[layer], ssm_lam_im[layer], ssm_b_re[layer], ssm_b_im[layer], ssm_c_re[layer],
               ssm_c_im[layer], ssm_d[layer], ssm_log_dt[layer], ssm_w_glu[layer])
        nsa = (nsa_q_norm_g[layer], nsa_k_norm_g[layer], cmp_pos_k[layer], cmp_pos_v[layer],
               cmp_k_w1[layer], cmp_k_w2[layer], cmp_v_w1[layer], cmp_v_w2[layer])
        h = _token_mixing(h, mod, positions, rope, norm_mix_g[layer], w_in[layer], w_out[layer],
                          conv_w[layer], ssm, nsa)
        i = layer // 2
        tl = _tiles(h.shape[1])
        if layer % 2 == 0:
            h = _dense_ffn(h, norm_ffn_g[layer], mod, ffn_w_gate[i].astype(BF16), ffn_w_up[i].astype(BF16),
                           ffn_w_down[i].astype(BF16), tm=tl.ffn_m, tf=tl.ffn_f)
        else:
            h = _moe_ffn(h, norm_ffn_g[layer], mod, moe_router[i], moe_w_gate[i].astype(BF16),
                         moe_w_up[i].astype(BF16), moe_w_down[i].astype(BF16), tm=tl.moe_m, tf=tl.ffn_f)
    return h
```

```python
import functools
import math
from typing import NamedTuple

import jax
import jax.numpy as jnp
from jax import lax
from jax.experimental import pallas as pl
from jax.experimental.pallas import tpu as pltpu

F32 = jnp.float32
BF16 = jnp.bfloat16
I32 = jnp.int32

HEAD_DIM = 64
CONV_WIDTH = 3
SSM_GROUP = 16
SSM_STATE = 64
SSM_MAX_RE = -1e-4
NSA_KV_HEADS = 2
CMP_LEN = 32
CMP_STRIDE = 16
SLC_BLOCK = 64
SLC_TOPK = 16
WINDOW = 512
FORCE_BONUS = 1e4
NEG_INF = -1e30
ROPE_THETA = 10000.0
RMS_EPS = 1e-6
N_EXPERTS = 8

LANES = 128
SUBLANES = 8
VMEM_LIMIT = 56 * 1024 * 1024
SEL_MASK_BIAS = -30000.0
SB_SKIP_LOG = -110.0
GATE_PAD = LANES


def _cparams(sem):
    return pltpu.CompilerParams(dimension_semantics=sem, vmem_limit_bytes=VMEM_LIMIT)


def _dot(a, b):
    return jnp.dot(a, b, preferred_element_type=F32)


def _dot_nt(a, b):
    return lax.dot_general(a, b, (((1,), (1,)), ((), ())), preferred_element_type=F32)


def _split(x):
    hi = x.astype(BF16)
    lo = (x - hi.astype(F32)).astype(BF16)
    return hi, lo


def _dot3(a, b):
    ah, al = _split(a)
    bh, bl = _split(b)
    return _dot(ah, bh) + _dot(ah, bl) + _dot(al, bh)


def _dot3_nt(a, b):
    ah, al = _split(a)
    bh, bl = _split(b)
    return _dot_nt(ah, bh) + _dot_nt(ah, bl) + _dot_nt(al, bh)


def _dot2_exact_rhs(a, b_bf16):
    ah, al = _split(a)
    return _dot(ah, b_bf16) + _dot(al, b_bf16)


def _silu(x):
    return x * jax.nn.sigmoid(x)


def _div_pow2(x, n):
    return lax.shift_right_logical(x, jnp.int32(n.bit_length() - 1))


def _mod_pow2(x, n):
    return x & (n - 1)


def _rms_scale(x):
    return lax.rsqrt(jnp.mean(x * x, axis=-1, keepdims=True) + RMS_EPS)


def _mod_kernel(c_ref, w_ref, b_ref, o_ref):
    o_ref[0] = _dot3(_silu(c_ref[...]), w_ref[0]) + b_ref[0]


def _modulation(c, ada_w, ada_b):
    depth, d, n6 = ada_w.shape
    bsz = c.shape[0]
    rows = -(-bsz // SUBLANES) * SUBLANES
    cpad = jnp.zeros((rows, d), F32).at[:bsz].set(c)
    tn = n6 // 4
    out = pl.pallas_call(
        _mod_kernel,
        out_shape=jax.ShapeDtypeStruct((depth, rows, n6), F32),
        grid=(depth, n6 // tn),
        in_specs=[pl.BlockSpec((rows, d), lambda l, j: (0, 0)),
                  pl.BlockSpec((1, d, tn), lambda l, j: (l, 0, j)),
                  pl.BlockSpec((1, 1, tn), lambda l, j: (l, 0, j))],
        out_specs=pl.BlockSpec((1, rows, tn), lambda l, j: (l, 0, j)),
        compiler_params=_cparams(("arbitrary", "arbitrary")),
        name="adaln_mod",
    )(cpad, ada_w, ada_b[:, None, :])
    return out[:, :bsz].reshape(depth, bsz, 6, d)


def _in_kernel(h_ref, g_ref, mod_ref, w_ref, zb_ref, zs_ref, zf_ref):
    x = h_ref[0]
    y = x * _rms_scale(x) * g_ref[...]
    a = y * (1.0 + mod_ref[0, 1:2, :]) + mod_ref[0, 0:1, :]
    z = _dot(a.astype(BF16), w_ref[...])
    nb = zb_ref.shape[-1]
    ns = zs_ref.shape[-1]
    zb_ref[0] = z[:, :nb].astype(BF16)
    zs_ref[0] = z[:, nb:nb + ns]
    zf_ref[0] = z[:, nb + ns:]


def _in_proj(h, g, mod, w_bf16, nb, ns, tm):
    bsz, t, d = h.shape
    nz = w_bf16.shape[1]
    widths = (nb, ns, nz - nb - ns)
    return pl.pallas_call(
        _in_kernel,
        out_shape=tuple(jax.ShapeDtypeStruct((bsz, t, w), dt) for w, dt in zip(widths, (BF16, F32, F32))),
        grid=(bsz, t // tm),
        in_specs=[pl.BlockSpec((1, tm, d), lambda b, i: (b, i, 0)),
                  pl.BlockSpec((1, d), lambda b, i: (0, 0)),
                  pl.BlockSpec((1, 6, d), lambda b, i: (b, 0, 0)),
                  pl.BlockSpec((d, nz), lambda b, i: (0, 0))],
        out_specs=tuple(pl.BlockSpec((1, tm, w), lambda b, i: (b, i, 0)) for w in widths),
        compiler_params=_cparams(("arbitrary", "arbitrary")),
        name="in_proj",
    )(h, g[None, :], mod, w_bf16)


def _conv_kernel(z_ref, w_ref, o_ref, buf, *, tt, dc):
    @pl.when(pl.program_id(1) == 0)
    def _():
        buf[0:SUBLANES, :] = jnp.zeros((SUBLANES, dc), F32)

    z = z_ref[0].astype(F32)
    gate_b = z[:, :dc]
    v = z[:, dc:2 * dc] * z[:, 2 * dc:]
    buf[SUBLANES:, :] = v
    y = (w_ref[2:3, :] * v
         + w_ref[1:2, :] * buf[SUBLANES - 1:SUBLANES - 1 + tt, :]
         + w_ref[0:1, :] * buf[SUBLANES - 2:SUBLANES - 2 + tt, :])
    o_ref[0] = (gate_b * y).astype(o_ref.dtype)
    buf[0:SUBLANES, :] = v[tt - SUBLANES:, :]


def _short_conv(zb, conv_w, tt):
    bsz, t, _ = zb.shape
    dc = conv_w.shape[1]
    c3 = 3 * dc
    return pl.pallas_call(
        functools.partial(_conv_kernel, tt=tt, dc=dc),
        out_shape=jax.ShapeDtypeStruct((bsz, t, dc), BF16),
        grid=(bsz, t // tt),
        in_specs=[pl.BlockSpec((1, tt, c3), lambda b, i: (b, i, 0)),
                  pl.BlockSpec((CONV_WIDTH, dc), lambda b, i: (0, 0))],
        out_specs=pl.BlockSpec((1, tt, dc), lambda b, i: (b, i, 0)),
        scratch_shapes=[pltpu.VMEM((tt + SUBLANES, dc), F32)],
        compiler_params=_cparams(("arbitrary", "arbitrary")),
        name="short_conv",
    )(zb, conv_w)


SSM_LANE_CHUNK = 2 * LANES


def _gelu_tanh(x):
    return 0.5 * x * (1.0 + jnp.tanh(math.sqrt(2.0 / math.pi) * (x + 0.044715 * (x * x * x))))


def _ssm_kernel(u_ref, tz_ref, we_ref, wd_ref, pre_ref, pim_ref, d_ref, o_ref,
                bre, bim, cin, cre, cim, *, tt, ns):
    @pl.when(pl.program_id(1) == 0)
    def _():
        cre[...] = jnp.zeros_like(cre)
        cim[...] = jnp.zeros_like(cim)

    u = u_ref[0]
    ub = u.astype(BF16)
    end = _dot(ub, we_ref[...])
    bre[...] = end[:, :ns]
    bim[...] = end[:, ns:]
    lw = SSM_LANE_CHUNK
    row = lax.broadcasted_iota(I32, (SUBLANES, lw), 0)

    def local_scan(i, carry):
        r0 = pl.multiple_of(i * SUBLANES, SUBLANES)
        for lg in range(ns // lw):
            sl = slice(lg * lw, (lg + 1) * lw)
            xr = bre[pl.ds(r0, SUBLANES), sl]
            xi = bim[pl.ds(r0, SUBLANES), sl]
            for k in (1, 2, 4):
                ar = pre_ref[k - 1:k, sl]
                ai = pim_ref[k - 1:k, sl]
                sr = jnp.where(row >= k, pltpu.roll(xr, k, 0), 0.0)
                si = jnp.where(row >= k, pltpu.roll(xi, k, 0), 0.0)
                xr, xi = xr + ar * sr - ai * si, xi + ar * si + ai * sr
            bre[pl.ds(r0, SUBLANES), sl] = xr
            bim[pl.ds(r0, SUBLANES), sl] = xi
        return carry

    lax.fori_loop(0, tt // SUBLANES, local_scan, 0, unroll=2)

    row_all = lax.broadcasted_iota(I32, (SUBLANES, ns), 0)

    def carry_in(i, carry):
        c_re, c_im = carry
        r0 = pl.multiple_of(i * SUBLANES, SUBLANES)
        pr = pre_ref[...]
        pi_ = pim_ref[...]
        xr = bre[pl.ds(r0, SUBLANES), :] + pr * c_re - pi_ * c_im
        xi = bim[pl.ds(r0, SUBLANES), :] + pr * c_im + pi_ * c_re
        cin[pl.ds(r0, SUBLANES), :ns] = jnp.where(row_all == 0, c_re, pltpu.roll(xr, 1, 0))
        cin[pl.ds(r0, SUBLANES), ns:] = jnp.where(row_all == 0, c_im, pltpu.roll(xi, 1, 0))
        return xr[SUBLANES - 1:], xi[SUBLANES - 1:]

    c_re, c_im = lax.fori_loop(0, tt // SUBLANES, carry_in, (cre[...], cim[...]))
    cre[...] = c_re
    cim[...] = c_im
    o_ref[0] = _dot(ub, tz_ref[...]) + _dot(cin[...].astype(BF16), wd_ref[...]) + d_ref[...] * u


SSM_CHUNK = 8


def _s5(u, lam_re, lam_im, b_re, b_im, c_re, c_im, d_skip, log_dt, tt):
    bsz, t, ds = u.shape
    g, p = lam_re.shape
    hc = SSM_GROUP
    ns = g * p
    nl = SSM_CHUNK
    lam = lax.complex(jnp.minimum(lam_re, SSM_MAX_RE), lam_im)
    dt = jnp.exp(log_dt)[:, None]
    lam_bar = jnp.exp(lam * dt)
    b_bar = ((lam_bar - 1.0) / lam)[..., None] * lax.complex(b_re, b_im)
    cmat = lax.complex(c_re, c_im)
    steps = jnp.arange(nl + 1, dtype=F32)[:, None, None]
    pw = jnp.exp(steps * (lam * dt)[None])
    in_grp = (jnp.arange(nl * ds) // hc) % g
    st_grp = jnp.arange(ns) // p
    rep = lambda n, m: (jnp.arange(m)[None, :] % n == jnp.arange(n)[:, None]).astype(F32)
    kern = jnp.einsum('gcp,tgp,gpd->tgdc', cmat, pw[:nl], b_bar).real.reshape(nl, ds, hc)
    zero = jnp.zeros((ds, hc), F32)
    rows_rc = jnp.concatenate(
        [jnp.concatenate([zero] * s + [kern[r - s] for r in range(s, nl)], axis=1) for s in range(nl)], axis=0)
    col_rc = (jnp.arange(nl * ds) // ds) * hc + jnp.arange(nl * ds) % hc
    spread_rc = (col_rc[None, :] == jnp.arange(nl * hc)[:, None]).astype(F32)
    tz = jnp.dot(rows_rc, spread_rc) * (in_grp[:, None] == in_grp[None, :])
    end = jnp.einsum('sgp,gpd->sgdp', pw[:nl][::-1], b_bar).reshape(nl * ds, p)
    same_in_st = in_grp[:, None] == st_grp[None, :]
    we = jnp.concatenate([jnp.dot(part, rep(p, ns)) * same_in_st for part in (end.real, end.imag)], axis=1)
    dec = jnp.einsum('gcp,rgp->rgcp', cmat, pw[1:]).reshape(nl * ds, p)
    wd = jnp.concatenate([(jnp.dot(part, rep(p, ns)) * same_in_st).T for part in (dec.real, -dec.imag)], axis=0)
    pows = jnp.exp(jnp.arange(1, SUBLANES + 1, dtype=F32)[:, None, None] * nl * (lam * dt)[None])
    pows = pows.reshape(SUBLANES, ns)
    rows = t // nl
    const = lambda shape: pl.BlockSpec(shape, lambda b, i: (0, 0), pipeline_mode=pl.Buffered(1))
    y = pl.pallas_call(
        functools.partial(_ssm_kernel, tt=tt, ns=ns),
        out_shape=jax.ShapeDtypeStruct((bsz, rows, nl * ds), F32),
        grid=(bsz, rows // tt),
        in_specs=[pl.BlockSpec((1, tt, nl * ds), lambda b, i: (b, i, 0)),
                  const((nl * ds, nl * ds)), const((nl * ds, 2 * ns)), const((2 * ns, nl * ds)),
                  pl.BlockSpec((SUBLANES, ns), lambda b, i: (0, 0)),
                  pl.BlockSpec((SUBLANES, ns), lambda b, i: (0, 0)),
                  pl.BlockSpec((1, nl * ds), lambda b, i: (0, 0))],
        out_specs=pl.BlockSpec((1, tt, nl * ds), lambda b, i: (b, i, 0)),
        scratch_shapes=[pltpu.VMEM((tt, ns), F32), pltpu.VMEM((tt, ns), F32), pltpu.VMEM((tt, 2 * ns), F32),
                        pltpu.VMEM((1, ns), F32), pltpu.VMEM((1, ns), F32)],
        compiler_params=_cparams(("arbitrary", "arbitrary")),
        name="s5_scan",
    )(u.reshape(bsz, rows, nl * ds), tz.astype(BF16), we.astype(BF16), wd.astype(BF16), pows.real, pows.imag,
      jnp.tile(d_skip, nl)[None, :])
    return y.reshape(bsz, t, ds)


def _sb_kernel(q_ref, k_ref, v_ref, o_ref, acc, csum, *, tq, nh):
    qi = pl.program_id(1)
    acc[...] = jnp.zeros_like(acc)
    csum[...] = jnp.zeros_like(csum)
    ti = lax.broadcasted_iota(I32, (tq, tq), 0)
    ji = lax.broadcasted_iota(I32, (tq, tq), 1)
    later = (ti > ji).astype(BF16)

    def cond(state):
        kb, cmax = state
        return jnp.logical_and(kb >= 0, cmax > SB_SKIP_LOG)

    def block(kb, diagonal):
        k0 = pl.multiple_of(kb * tq, tq)
        past = ji < ti
        keep = (lambda x: jnp.where(past, x, 0.0)) if diagonal else (lambda x: x)
        heads = [slice(h * HEAD_DIM, (h + 1) * HEAD_DIM) for h in range(nh)]
        zs = [_dot_nt(q_ref[0, :, hs], k_ref[0, pl.ds(k0, tq), hs]) for hs in heads]
        log_betas, log_keeps, inners = [], [], []
        for z in zs:
            log_beta = jnp.minimum(z, 0.0) - jnp.log(1.0 + jnp.exp(-jnp.abs(z)))
            log_keep = keep(log_beta - z)
            hi, lo = _split(log_keep)
            log_betas.append(log_beta)
            log_keeps.append(log_keep)
            inners.append(_dot(hi, later) + _dot(lo, later))
        cmax = jnp.float32(-jnp.inf)
        for h, hs in enumerate(heads):
            c = csum[h]
            w = keep(jnp.exp(log_betas[h] + inners[h] + c[:, :1]))
            acc[:, hs] += _dot(w.astype(BF16), v_ref[0, pl.ds(k0, tq), hs])
            c = c + jnp.sum(log_keeps[h], axis=1, keepdims=True)
            csum[h] = c
            cmax = jnp.maximum(cmax, jnp.max(c))
        return cmax

    def body(state):
        kb, _ = state
        return kb - 1, block(kb, False)

    lax.while_loop(cond, body, (qi - 1, block(qi, True)))
    o_ref[0] = acc[...].astype(o_ref.dtype)


SB_Q_BLOCK, SB_K_BLOCK, SB_V_BLOCK = 3, 4, 5


def _stick_breaking(zb, dq, tq):
    bsz, t, _ = zb.shape
    nh = dq // HEAD_DIM
    return pl.pallas_call(
        functools.partial(_sb_kernel, tq=tq, nh=nh),
        out_shape=jax.ShapeDtypeStruct((bsz, t, dq), BF16),
        grid=(bsz, t // tq),
        in_specs=[pl.BlockSpec((1, tq, dq), lambda b, i: (b, i, SB_Q_BLOCK)),
                  pl.BlockSpec((1, t, dq), lambda b, i: (b, 0, SB_K_BLOCK)),
                  pl.BlockSpec((1, t, dq), lambda b, i: (b, 0, SB_V_BLOCK))],
        out_specs=pl.BlockSpec((1, tq, dq), lambda b, i: (b, i, 0)),
        scratch_shapes=[pltpu.VMEM((tq, dq), F32), pltpu.VMEM((nh, tq, LANES), F32)],
        compiler_params=_cparams(("arbitrary", "arbitrary")),
        name="stick_breaking",
    )(zb, zb, zb)


def _head_norm(x, gsum_ref, gain):
    ss = _dot2_exact_rhs(x * x, gsum_ref[...]) * (1.0 / HEAD_DIM)
    return x * lax.rsqrt(ss + RMS_EPS) * gain


def _rope_lanes(x, cos, sin):
    w = x.shape[-1]
    lane = lax.broadcasted_iota(I32, x.shape, 1)
    first = _mod_pow2(lane, HEAD_DIM) < (HEAD_DIM // 2)
    partner = jnp.where(first, -pltpu.roll(x, w - HEAD_DIM // 2, 1), pltpu.roll(x, HEAD_DIM // 2, 1))
    return x * cos + partner * sin


def _nsa_prep_kernel(zq_ref, zkc_ref, zvc_ref, zks_ref, zvs_ref, zkw_ref, zvw_ref, cos_ref, sin_ref, gq_ref,
                     gk_ref, gsq_ref, gsk_ref, q_ref, ck_ref, cv_ref, kaug_ref, vaug_ref, kw_ref, vw_ref,
                     *, tm, ngrp, nblk):
    cos1 = cos_ref[0]
    sin1 = sin_ref[0]
    wq = zq_ref.shape[-1]
    cosq = jnp.concatenate([cos1] * (wq // LANES), axis=1)
    sinq = jnp.concatenate([sin1] * (wq // LANES), axis=1)
    q = _rope_lanes(_head_norm(zq_ref[0], gsq_ref, gq_ref[...]), cosq, sinq)
    q_ref[0] = q * (HEAD_DIM ** -0.5)
    ks = _rope_lanes(_head_norm(zks_ref[0], gsk_ref, gk_ref[...]), cos1, sin1)
    kw = _rope_lanes(_head_norm(zkw_ref[0], gsk_ref, gk_ref[...]), cos1, sin1)
    t0 = pl.program_id(1) * tm
    tok = t0 + lax.broadcasted_iota(I32, (tm, nblk), 0)
    blk = lax.broadcasted_iota(I32, (tm, nblk), 1)
    onehot = jnp.where(_div_pow2(tok, SLC_BLOCK) == blk, 1.0, 0.0).astype(BF16)
    vs = zvs_ref[0]
    vw = zvw_ref[0]
    lane = lax.broadcasted_iota(I32, (tm, LANES - HEAD_DIM), 1)
    ones_col = jnp.where(lane == 0, 1.0, 0.0).astype(BF16)
    for g in range(ngrp):
        sl = slice(g * HEAD_DIM, (g + 1) * HEAD_DIM)
        kaug_ref[0, g] = jnp.concatenate([onehot, ks[:, sl].astype(BF16)], axis=1)
        vaug_ref[0, g] = jnp.concatenate([vs[:, sl].astype(BF16), ones_col], axis=1)
        kw_ref[0, g] = kw[:, sl].astype(BF16)
        vw_ref[0, g] = jnp.concatenate([vw[:, sl].astype(BF16), ones_col], axis=1)
    for j in range(CMP_STRIDE):
        rows = pl.ds(j, tm // CMP_STRIDE, stride=CMP_STRIDE)
        js = slice(j * HEAD_DIM, (j + 1) * HEAD_DIM)
        kj = zkc_ref[0, rows, :]
        vj = zvc_ref[0, rows, :]
        for g in range(ngrp):
            sl = slice(g * HEAD_DIM, (g + 1) * HEAD_DIM)
            ck_ref[0, g, :, js] = kj[:, sl]
            cv_ref[0, g, :, js] = vj[:, sl]


ZF_Q_BLOCK = 0
ZF_KCMP, ZF_VCMP, ZF_KSLC, ZF_VSLC, ZF_KWIN, ZF_VWIN, ZF_GATE = 2, 3, 4, 5, 6, 7, 8


def _rope_kernel(ang_ref, cos_ref, sin_ref):
    ang = ang_ref[...]
    cos_ref[...] = jnp.cos(ang)
    sin_ref[...] = jnp.sin(ang)


def _rope_tables(positions):
    bsz, t = positions.shape
    half = HEAD_DIM // 2
    inv_freq = jnp.power(jnp.float32(ROPE_THETA), -jnp.arange(half, dtype=F32) / half)
    ang = (positions.astype(F32)[..., None] * inv_freq).reshape(bsz, t * half // LANES, LANES)
    rows = ang.shape[1]
    spec = pl.BlockSpec((1, rows, LANES), lambda b: (b, 0, 0))
    cos, sin = pl.pallas_call(
        _rope_kernel,
        out_shape=(jax.ShapeDtypeStruct(ang.shape, F32), jax.ShapeDtypeStruct(ang.shape, F32)),
        grid=(bsz,),
        in_specs=[spec],
        out_specs=(spec, spec),
        compiler_params=_cparams(("arbitrary",)),
        name="rope_tables",
    )(ang)
    widen = lambda x: jnp.tile(x.reshape(bsz, t, half), (1, 1, LANES // half))
    return widen(cos), widen(sin)


def _nsa_prep(zf, rope, gq, gk, wq, tm):
    bsz, t, _ = zf.shape
    cos, sin = rope
    ngrp = NSA_KV_HEADS
    nblk = t // SLC_BLOCK
    wk = ngrp * HEAD_DIM
    assert wk == LANES and wq == 2 * wk
    lane_q = jnp.arange(wq) // HEAD_DIM
    gsq = (lane_q[:, None] == lane_q[None, :]).astype(BF16)
    lane_k = jnp.arange(wk) // HEAD_DIM
    gsk = (lane_k[:, None] == lane_k[None, :]).astype(BF16)
    kvspec = lambda blk: pl.BlockSpec((1, tm, wk), lambda b, i: (b, i, blk))
    hspec = lambda w: pl.BlockSpec((1, ngrp, tm, w), lambda b, i: (b, 0, i, 0))
    cspec = pl.BlockSpec((1, ngrp, tm // CMP_STRIDE, CMP_STRIDE * HEAD_DIM), lambda b, i: (b, 0, i, 0))
    return pl.pallas_call(
        functools.partial(_nsa_prep_kernel, tm=tm, ngrp=ngrp, nblk=nblk),
        out_shape=(jax.ShapeDtypeStruct((bsz, t, wq), F32),
                   jax.ShapeDtypeStruct((bsz, ngrp, t // CMP_STRIDE, CMP_STRIDE * HEAD_DIM), F32),
                   jax.ShapeDtypeStruct((bsz, ngrp, t // CMP_STRIDE, CMP_STRIDE * HEAD_DIM), F32),
                   jax.ShapeDtypeStruct((bsz, ngrp, t, HEAD_DIM + nblk), BF16),
                   jax.ShapeDtypeStruct((bsz, ngrp, t, LANES), BF16),
                   jax.ShapeDtypeStruct((bsz, ngrp, t, HEAD_DIM), BF16),
                   jax.ShapeDtypeStruct((bsz, ngrp, t, LANES), BF16)),
        grid=(bsz, t // tm),
        in_specs=[pl.BlockSpec((1, tm, wq), lambda b, i: (b, i, ZF_Q_BLOCK)),
                  kvspec(ZF_KCMP), kvspec(ZF_VCMP),
                  kvspec(ZF_KSLC), kvspec(ZF_VSLC), kvspec(ZF_KWIN), kvspec(ZF_VWIN),
                  pl.BlockSpec((1, tm, LANES), lambda b, i: (b, i, 0)),
                  pl.BlockSpec((1, tm, LANES), lambda b, i: (b, i, 0)),
                  pl.BlockSpec((1, wq), lambda b, i: (0, 0)),
                  pl.BlockSpec((1, wk), lambda b, i: (0, 0)),
                  pl.BlockSpec((wq, wq), lambda b, i: (0, 0)),
                  pl.BlockSpec((wk, wk), lambda b, i: (0, 0))],
        out_specs=(pl.BlockSpec((1, tm, wq), lambda b, i: (b, i, 0)),
                   cspec, cspec,
                   hspec(HEAD_DIM + nblk), hspec(LANES), hspec(HEAD_DIM), hspec(LANES)),
        compiler_params=_cparams(("arbitrary", "arbitrary")),
        name="nsa_prep",
    )(zf, zf, zf, zf, zf, zf, zf, cos, sin, jnp.tile(gq, wq // HEAD_DIM)[None, :], jnp.tile(gk, ngrp)[None, :],
      gsq, gsk)


def _compress_kernel(ck_ref, cv_ref, w1k_ref, w2k_ref, w1v_ref, w2v_ref, pek_ref, pev_ref,
                     gk_ref, angc_ref, kc_ref, vc_ref, *, nch, half):
    def mlp(c_ref, w1_ref, w2_ref, pe_ref):
        x = c_ref[0, 0]
        w1 = w1_ref[...]
        first = _dot3(x, w1[:half])
        second = _dot3(x, w1[half:])
        bias = _dot3(pe_ref[...], w1)[0:1]
        hid = _silu(first + pltpu.roll(second, nch - 1, 0) + bias)
        return _dot3(hid, w2_ref[...])

    kc = mlp(ck_ref, w1k_ref, w2k_ref, pek_ref)
    kc = kc * _rms_scale(kc) * gk_ref[...]
    ang = angc_ref[0]
    half_d = HEAD_DIM // 2
    k1 = kc[:, :half_d]
    k2 = kc[:, half_d:]
    cos = jnp.cos(ang[:, :half_d])
    sin = jnp.sin(ang[:, :half_d])
    kc_ref[0, 0] = jnp.concatenate([k1 * cos - k2 * sin, k2 * cos + k1 * sin], axis=1)
    vc_ref[0, 0] = mlp(cv_ref, w1v_ref, w2v_ref, pev_ref)


def _compress(ck, cv, w1k, w2k, w1v, w2v, pos_k, pos_v, gk, angc):
    bsz, ngrp, nch, half = ck.shape
    hid = w1k.shape[1]

    def pe_rows(pe):
        return jnp.zeros((SUBLANES, 2 * half), F32).at[0].set(pe.reshape(-1))

    blk4 = lambda b, g: (b, g, 0, 0)
    full2 = lambda b, g: (0, 0)
    return pl.pallas_call(
        functools.partial(_compress_kernel, nch=nch, half=half),
        out_shape=(jax.ShapeDtypeStruct((bsz, ngrp, nch, HEAD_DIM), F32),
                   jax.ShapeDtypeStruct((bsz, ngrp, nch, HEAD_DIM), F32)),
        grid=(bsz, ngrp),
        in_specs=[pl.BlockSpec((1, 1, nch, half), blk4),
                  pl.BlockSpec((1, 1, nch, half), blk4),
                  pl.BlockSpec((2 * half, hid), full2),
                  pl.BlockSpec((hid, HEAD_DIM), full2),
                  pl.BlockSpec((2 * half, hid), full2),
                  pl.BlockSpec((hid, HEAD_DIM), full2),
                  pl.BlockSpec((SUBLANES, 2 * half), full2),
                  pl.BlockSpec((SUBLANES, 2 * half), full2),
                  pl.BlockSpec((1, HEAD_DIM), full2),
                  pl.BlockSpec((1, nch, HEAD_DIM), lambda b, g: (b, 0, 0))],
        out_specs=(pl.BlockSpec((1, 1, nch, HEAD_DIM), blk4),
                   pl.BlockSpec((1, 1, nch, HEAD_DIM), blk4)),
        compiler_params=_cparams(("arbitrary", "arbitrary")),
        name="nsa_compress",
    )(ck, cv, w1k, w2k, w1v, w2v, pe_rows(pos_k), pe_rows(pos_v), gk[None, :], angc)


def _gate_col(gates, head, branch):
    c = head * 3 + branch
    if isinstance(c, int):
        return gates[:, c:c + 1]
    lane = lax.broadcasted_iota(I32, gates.shape, 1)
    return jnp.sum(jnp.where(lane == c, gates, 0.0), axis=1, keepdims=True)


def _cmp_attn_kernel(q_ref, kc_ref, vct_ref, gl_ref, ovt_ref, oc_ref, sel_ref, *, tq, ngrp, hpg, nch, nblk):
    t = pl.program_id(1) * tq + lax.broadcasted_iota(I32, (1, tq), 1)
    cmp_end = lax.broadcasted_iota(I32, (nch, 1), 0) * CMP_STRIDE + (CMP_LEN - 1)
    visible = cmp_end <= t
    q = q_ref[0]
    gates = jax.nn.sigmoid(gl_ref[0])
    blk = lax.broadcasted_iota(I32, (nblk, tq), 0)
    cur = _div_pow2(t, SLC_BLOCK)
    forced = (blk == 0) | (blk == cur) | (blk == cur - 1)
    causal = blk <= cur
    lane = lax.broadcasted_iota(I32, (tq, hpg * HEAD_DIM), 1)
    outs = []
    for g in range(ngrp):
        kc = kc_ref[0, g]
        vct = vct_ref[0, g]
        psum = jnp.zeros((nch, tq), F32)
        o_t = []
        for hh in range(hpg):
            head = g * hpg + hh
            qh = q[:, head * HEAD_DIM:(head + 1) * HEAD_DIM]
            s = jnp.where(visible, _dot3_nt(kc, qh), NEG_INF)
            p = jnp.exp(s - jnp.maximum(jnp.max(s, axis=0, keepdims=True), 0.1 * NEG_INF))
            denom = jnp.sum(p, axis=0, keepdims=True)
            p = p * (1.0 / jnp.where(denom == 0.0, 1.0, denom))
            psum = psum + p
            o_t.append(_dot(vct, p.astype(BF16)))
        gate = _gate_col(gates, g * hpg, 0)
        for hh in range(1, hpg):
            gate = jnp.where(lane < hh * HEAD_DIM, gate, _gate_col(gates, g * hpg + hh, 0))
        outs.append(jnp.concatenate(o_t, axis=0).T * gate)
        hi, lo = _split(psum)
        imp = _dot(ovt_ref[...], hi) + _dot(ovt_ref[...], lo)
        val = jnp.where(causal, imp + jnp.where(forced, FORCE_BONUS, 0.0), NEG_INF)
        for _ in range(min(SLC_TOPK, nblk)):
            m = jnp.max(val, axis=0, keepdims=True)
            first = jnp.min(jnp.where(val == m, blk, nblk), axis=0, keepdims=True)
            val = jnp.where(blk == first, -jnp.inf, val)
        sel_ref[0, g] = jnp.where(val == -jnp.inf, 0.0, SEL_MASK_BIAS).T.astype(BF16)
    oc_ref[0] = jnp.concatenate(outs, axis=1).astype(oc_ref.dtype)


def _cmp_attn(q, kc, vc, glog, tq):
    bsz, t, wq = q.shape
    ngrp, nch = kc.shape[1], kc.shape[2]
    hpg = wq // HEAD_DIM // ngrp
    nblk = t // SLC_BLOCK
    cs = jnp.arange(nch) * CMP_STRIDE
    ss = jnp.arange(nblk) * SLC_BLOCK
    ovt = ((cs[None, :] < ss[:, None] + SLC_BLOCK) & (cs[None, :] + CMP_LEN > ss[:, None])).astype(BF16)
    return pl.pallas_call(
        functools.partial(_cmp_attn_kernel, tq=tq, ngrp=ngrp, hpg=hpg, nch=nch, nblk=nblk),
        out_shape=(jax.ShapeDtypeStruct((bsz, t, wq), BF16),
                   jax.ShapeDtypeStruct((bsz, ngrp, t, nblk), BF16)),
        grid=(bsz, t // tq),
        in_specs=[pl.BlockSpec((1, tq, wq), lambda b, i: (b, i, 0)),
                  pl.BlockSpec((1, ngrp, nch, HEAD_DIM), lambda b, i: (b, 0, 0, 0)),
                  pl.BlockSpec((1, ngrp, HEAD_DIM, nch), lambda b, i: (b, 0, 0, 0)),
                  pl.BlockSpec((1, tq, GATE_PAD), lambda b, i: (b, i, ZF_GATE)),
                  pl.BlockSpec((nblk, nch), lambda b, i: (0, 0))],
        out_specs=(pl.BlockSpec((1, tq, wq), lambda b, i: (b, i, 0)),
                   pl.BlockSpec((1, ngrp, tq, nblk), lambda b, i: (b, 0, i, 0))),
        compiler_params=_cparams(("arbitrary", "arbitrary")),
        name="nsa_cmp_select",
    )(q, kc, vc.transpose(0, 1, 3, 2).astype(BF16), glog, ovt)


SEL_ROW_CHUNK = 128


def _sel_attn_kernel(q_ref, sel_ref, kaug_ref, vaug_ref, gl_ref, o_ref, qa_s, m_s, acc, s_a, s_b, *, tq, tk, hpg):
    g = pl.program_id(1)
    qi = pl.program_id(2)
    q = q_ref[0]
    sel = sel_ref[0, 0]
    for hh in range(hpg):
        qa_s[hh * tq:(hh + 1) * tq, :] = jnp.concatenate(
            [sel, q[:, hh * HEAD_DIM:(hh + 1) * HEAD_DIM].astype(BF16)], axis=1)
    m_s[...] = jnp.full_like(m_s, NEG_INF)
    acc[...] = jnp.zeros_like(acc)
    rows = hpg * tq
    rc = SEL_ROW_CHUNK

    def scores(kt, buf):
        k0 = pl.multiple_of(kt * tk, tk)
        kblk = kaug_ref[0, 0, pl.ds(k0, tk), :]
        for c in range(rows // rc):
            rs = slice(c * rc, (c + 1) * rc)
            buf[rs, :] = _dot_nt(qa_s[rs, :], kblk)

    def consume(kt, buf, diag):
        k0 = pl.multiple_of(kt * tk, tk)
        vblk = vaug_ref[0, 0, pl.ds(k0, tk), :]
        for c in range(rows // rc):
            rs = slice(c * rc, (c + 1) * rc)
            s = buf[rs, :]
            if diag:
                t = qi * tq + _mod_pow2(c * rc + lax.broadcasted_iota(I32, (rc, tk), 0), tq)
                s = jnp.where(k0 + lax.broadcasted_iota(I32, (rc, tk), 1) <= t, s, NEG_INF)
            m_old = m_s[rs, :]
            m_new = jnp.maximum(m_old, jnp.max(s, axis=1, keepdims=True))
            alpha = jnp.exp(m_old - m_new)
            p = jnp.exp(s - m_new[:, :1])
            acc[rs, :] = alpha * acc[rs, :] + _dot(p.astype(BF16), vblk)
            m_s[rs, :] = m_new

    last = (qi * tq) // tk
    scores(0, s_a)

    def two_tiles(k):
        scores(k + 1, s_b)
        consume(k, s_a, False)
        scores(k + 2, s_a)
        consume(k + 1, s_b, False)

    def quad(j, carry):
        two_tiles(4 * j)
        two_tiles(4 * j + 2)
        return carry

    def pair(j, carry):
        two_tiles(4 * (last // 4) + 2 * j)
        return carry

    lax.fori_loop(0, last // 4, quad, 0)
    lax.fori_loop(0, (last % 4) // 2, pair, 0)
    t0 = (last // 2) * 2

    @pl.when(t0 < last)
    def _():
        scores(last, s_b)
        consume(t0, s_a, False)
        consume(last, s_b, True)

    @pl.when(t0 == last)
    def _():
        consume(last, s_a, True)

    gates = jax.nn.sigmoid(gl_ref[0])
    a = acc[...]
    out = a[:, :HEAD_DIM] / a[:, HEAD_DIM:HEAD_DIM + 1]
    o_ref[0] = jnp.concatenate(
        [out[hh * tq:(hh + 1) * tq] * _gate_col(gates, g * hpg + hh, 1) for hh in range(hpg)],
        axis=1).astype(o_ref.dtype)


def _sel_attn(q, sel, kaug, vaug, glog, tq, tk):
    bsz, t, wq = q.shape
    ngrp = kaug.shape[1]
    hpg = wq // HEAD_DIM // ngrp
    gw = hpg * HEAD_DIM
    nblk = sel.shape[-1]
    ka = kaug.shape[-1]
    rows = hpg * tq
    assert tk % tq == 0 and rows % SEL_ROW_CHUNK == 0
    return pl.pallas_call(
        functools.partial(_sel_attn_kernel, tq=tq, tk=tk, hpg=hpg),
        out_shape=jax.ShapeDtypeStruct((bsz, t, wq), BF16),
        grid=(bsz, ngrp, t // tq),
        in_specs=[pl.BlockSpec((1, tq, gw), lambda b, g, i: (b, i, g)),
                  pl.BlockSpec((1, 1, tq, nblk), lambda b, g, i: (b, g, i, 0)),
                  pl.BlockSpec((1, 1, t, ka), lambda b, g, i: (b, g, 0, 0)),
                  pl.BlockSpec((1, 1, t, LANES), lambda b, g, i: (b, g, 0, 0)),
                  pl.BlockSpec((1, tq, GATE_PAD), lambda b, g, i: (b, i, ZF_GATE))],
        out_specs=pl.BlockSpec((1, tq, gw), lambda b, g, i: (b, i, g)),
        scratch_shapes=[pltpu.VMEM((rows, ka), BF16), pltpu.VMEM((rows, LANES), F32),
                        pltpu.VMEM((rows, LANES), F32),
                        pltpu.VMEM((rows, tk), F32), pltpu.VMEM((rows, tk), F32)],
        compiler_params=_cparams(("arbitrary", "arbitrary", "arbitrary")),
        name="nsa_selected",
    )(q, sel, kaug, vaug, glog)


def _win_attn_kernel(q_ref, k_ref, v_ref, gl_ref, bias_ref, o_ref, *, tq, hpg, span):
    g = pl.program_id(1)
    qi = pl.program_id(2)
    q = q_ref[0]
    q2 = jnp.concatenate([q[:, hh * HEAD_DIM:(hh + 1) * HEAD_DIM] for hh in range(hpg)], axis=0).astype(BF16)
    base = pl.multiple_of(jnp.maximum(qi * tq - WINDOW, 0), tq)
    kblk = k_ref[0, 0, pl.ds(base, span), :]
    vblk = v_ref[0, 0, pl.ds(base, span), :]
    gates = jax.nn.sigmoid(gl_ref[0])

    def attend(bias):
        rc = SEL_ROW_CHUNK
        chunks = [slice(c * rc, (c + 1) * rc) for c in range(hpg * tq // rc)]
        scores = [_dot_nt(q2[rs], kblk) + bias[rs] for rs in chunks]
        outs = []
        for s in scores:
            p = jnp.exp(s - jnp.max(s, axis=1, keepdims=True))
            acc = _dot(p.astype(BF16), vblk)
            outs.append(acc[:, :HEAD_DIM] / acc[:, HEAD_DIM:HEAD_DIM + 1])
        out = jnp.concatenate(outs, axis=0)
        o_ref[0] = jnp.concatenate(
            [out[hh * tq:(hh + 1) * tq] * _gate_col(gates, g * hpg + hh, 2) for hh in range(hpg)],
            axis=1).astype(o_ref.dtype)

    @pl.when(qi * tq >= WINDOW)
    def _():
        attend(jnp.concatenate([bias_ref[...]] * hpg, axis=0))

    @pl.when(qi * tq < WINDOW)
    def _():
        rows = hpg * tq
        t = qi * tq + _mod_pow2(lax.broadcasted_iota(I32, (rows, span), 0), tq)
        wpos = lax.broadcasted_iota(I32, (rows, span), 1)
        attend(jnp.where((wpos <= t) & (wpos > t - WINDOW), 0.0, NEG_INF))


def _win_attn(q, kw, vw, glog, tq):
    bsz, t, wq = q.shape
    ngrp = kw.shape[1]
    hpg = wq // HEAD_DIM // ngrp
    gw = hpg * HEAD_DIM
    span = WINDOW + tq
    assert WINDOW % tq == 0 and t >= span
    rel = jnp.arange(span)[None, :] - WINDOW - jnp.arange(tq)[:, None]
    band = jnp.where((rel <= 0) & (rel > -WINDOW), 0.0, NEG_INF).astype(F32)
    return pl.pallas_call(
        functools.partial(_win_attn_kernel, tq=tq, hpg=hpg, span=span),
        out_shape=jax.ShapeDtypeStruct((bsz, t, wq), BF16),
        grid=(bsz, ngrp, t // tq),
        in_specs=[pl.BlockSpec((1, tq, gw), lambda b, g, i: (b, i, g)),
                  pl.BlockSpec((1, 1, t, HEAD_DIM), lambda b, g, i: (b, g, 0, 0)),
                  pl.BlockSpec((1, 1, t, LANES), lambda b, g, i: (b, g, 0, 0)),
                  pl.BlockSpec((1, tq, GATE_PAD), lambda b, g, i: (b, i, ZF_GATE)),
                  pl.BlockSpec((tq, span), lambda b, g, i: (0, 0))],
        out_specs=pl.BlockSpec((1, tq, gw), lambda b, g, i: (b, i, g)),
        compiler_params=_cparams(("arbitrary", "arbitrary", "arbitrary")),
        name="nsa_window",
    )(q, kw, vw, glog, band)


def _out_kernel(h_ref, yc_ref, ys_ref, yb_ref, oc_ref, os_ref, ow_ref, mod_ref, wglu_ref, w_ref, o_ref):
    g = _gelu_tanh(ys_ref[0])
    y_ssm = g * jax.nn.sigmoid(_dot(g.astype(BF16), wglu_ref[...]))
    nsa = oc_ref[0].astype(F32) + os_ref[0].astype(F32) + ow_ref[0].astype(F32)
    y = jnp.concatenate([yc_ref[0], y_ssm.astype(BF16), yb_ref[0], nsa.astype(BF16)], axis=1)
    o_ref[0] = h_ref[0] + mod_ref[0, 2:3, :] * _dot(y, w_ref[...])


def _out_proj(h, parts, mod, w_glu_bf16, w_bf16, tm):
    bsz, t, d = h.shape
    wp = parts[0].shape[-1]
    tok = lambda b, i: (b, i, 0)
    return pl.pallas_call(
        _out_kernel,
        out_shape=jax.ShapeDtypeStruct((bsz, t, d), F32),
        grid=(bsz, t // tm),
        in_specs=[pl.BlockSpec((1, tm, d), tok)] + [pl.BlockSpec((1, tm, wp), tok)] * 6
        + [pl.BlockSpec((1, 6, d), lambda b, i: (b, 0, 0)),
           pl.BlockSpec((wp, wp), lambda b, i: (0, 0)),
           pl.BlockSpec((d, d), lambda b, i: (0, 0))],
        out_specs=pl.BlockSpec((1, tm, d), tok),
        compiler_params=_cparams(("arbitrary", "arbitrary")),
        name="out_proj",
    )(h, *parts, mod, w_glu_bf16, w_bf16)


def _ffn_kernel(h_ref, g_ref, mod_ref, wg_ref, wu_ref, wd_ref, o_ref, a_s, acc):
    f = pl.program_id(2)

    @pl.when(f == 0)
    def _():
        x = h_ref[0]
        y = x * _rms_scale(x) * g_ref[...]
        a_s[...] = (y * (1.0 + mod_ref[0, 4:5, :]) + mod_ref[0, 3:4, :]).astype(BF16)
        acc[...] = jnp.zeros_like(acc)

    a = a_s[...]
    hid = _silu(_dot(a, wg_ref[...])) * _dot(a, wu_ref[...])
    acc[...] += _dot(hid.astype(BF16), wd_ref[...])

    @pl.when(f == pl.num_programs(2) - 1)
    def _():
        o_ref[0] = h_ref[0] + mod_ref[0, 5:6, :] * acc[...]


def _dense_ffn(h, g, mod, wg, wu, wd, tm, tf):
    bsz, t, d = h.shape
    f = wg.shape[1]
    tok = lambda b, i, j: (b, i, 0)
    return pl.pallas_call(
        _ffn_kernel,
        out_shape=jax.ShapeDtypeStruct((bsz, t, d), F32),
        grid=(bsz, t // tm, f // tf),
        in_specs=[pl.BlockSpec((1, tm, d), tok),
                  pl.BlockSpec((1, d), lambda b, i, j: (0, 0)),
                  pl.BlockSpec((1, 6, d), lambda b, i, j: (b, 0, 0)),
                  pl.BlockSpec((d, tf), lambda b, i, j: (0, j)),
                  pl.BlockSpec((d, tf), lambda b, i, j: (0, j)),
                  pl.BlockSpec((tf, d), lambda b, i, j: (j, 0))],
        out_specs=pl.BlockSpec((1, tm, d), tok),
        scratch_shapes=[pltpu.VMEM((tm, d), BF16), pltpu.VMEM((tm, d), F32)],
        compiler_params=_cparams(("arbitrary", "arbitrary", "arbitrary")),
        name="dense_swiglu",
    )(h, g[None, :], mod, wg, wu, wd)


MOE_CHUNK = 256


def _moe_route_kernel(h_ref, g_ref, mod_ref, wrt_ref, a_ref, gates_ref, rank_ref, cnt_ref, *, tm):
    x = h_ref[0]
    y = x * _rms_scale(x) * g_ref[...]
    a = y * (1.0 + mod_ref[0, 4:5, :]) + mod_ref[0, 3:4, :]
    a_ref[0] = a.astype(BF16)
    logits = _dot3_nt(wrt_ref[...], a)
    row = lax.broadcasted_iota(I32, logits.shape, 0)
    v1 = jnp.max(logits, axis=0, keepdims=True)
    i1 = jnp.min(jnp.where(logits == v1, row, N_EXPERTS), axis=0, keepdims=True)
    rest = jnp.where(row == i1, -jnp.inf, logits)
    v2 = jnp.max(rest, axis=0, keepdims=True)
    i2 = jnp.min(jnp.where(rest == v2, row, N_EXPERTS), axis=0, keepdims=True)
    e2 = jnp.exp(v2 - v1)
    gates_ref[0] = jnp.where(row == i1, 1.0 / (1.0 + e2), 0.0) + jnp.where(row == i2, e2 / (1.0 + e2), 0.0)
    routed = jnp.where(row == i1, 1.0, jnp.where(row == i2, 1.0, 0.0))
    si = lax.broadcasted_iota(I32, (MOE_CHUNK, MOE_CHUNK), 0)
    sj = lax.broadcasted_iota(I32, (MOE_CHUNK, MOE_CHUNK), 1)
    before = jnp.where(si < sj, 1.0, 0.0).astype(BF16)
    carry = jnp.zeros((N_EXPERTS, 1), F32)
    lane = lax.broadcasted_iota(I32, (N_EXPERTS, LANES), 1)
    starts = jnp.zeros((N_EXPERTS, LANES), F32)
    parts = []
    nsub = tm // MOE_CHUNK
    for c in range(nsub):
        rc = routed[:, c * MOE_CHUNK:(c + 1) * MOE_CHUNK]
        parts.append(_dot(rc.astype(BF16), before) + carry)
        starts = jnp.where(lane == c, carry, starts)
        carry = carry + jnp.sum(rc, axis=1, keepdims=True)
    rank = jnp.concatenate(parts, axis=1)
    rank_ref[0] = jnp.where(routed > 0.0, rank, -1.0)
    cnt_ref[0] = jnp.where(lane == nsub, carry, starts).astype(I32)


MOE_WINDOW = 5


def _moe_expert_kernel(cnt_ref, a_ref, rank_t_ref, rank_c_ref, gate_c_ref, wg_ref, wu_ref, wd_ref, o_ref,
                       xc, yacc, rc_s, gc_s, *, tm, ne):
    i = pl.program_id(0)
    e = pl.program_id(1)
    f = pl.program_id(2)
    nsub = tm // MOE_CHUNK
    win = min(MOE_WINDOW, nsub)
    half = MOE_CHUNK // 2
    base = (i * ne + e) * (nsub + 1)
    total = cnt_ref[base + nsub]
    rem = total % MOE_CHUNK
    nbig = total // MOE_CHUNK + (rem > half).astype(I32)
    has_tail = jnp.logical_and(rem > 0, rem <= half)
    tail0 = pl.multiple_of(nbig * MOE_CHUNK, MOE_CHUNK)
    d = o_ref.shape[-1]

    def big(fn):
        def body(c, carry):
            fn(pl.multiple_of(c * MOE_CHUNK, MOE_CHUNK), MOE_CHUNK)
            return carry

        lax.fori_loop(0, nbig, body, 0)

        @pl.when(has_tail)
        def _():
            fn(tail0, half)

    def window(r0, nrows):
        first = jnp.int32(0)
        stop = jnp.int32(0)
        for s in range(nsub):
            first += (cnt_ref[base + s + 1] <= r0).astype(I32)
            stop += (cnt_ref[base + s] < r0 + nrows).astype(I32)
        s0 = jnp.minimum(first, nsub - win)
        return s0, stop <= s0 + win

    def pick_rows(r0, nrows, s0, nsubs):
        tgt = (r0 + lax.broadcasted_iota(I32, (nrows, MOE_CHUNK), 0)).astype(F32)
        return jnp.concatenate(
            [jnp.where(rank_t_ref[0, e, pl.ds(s0 + j, 1), :] == tgt, 1.0, 0.0).astype(BF16)
             for j in range(nsubs)], axis=1)

    @pl.when(jnp.logical_and(e == 0, f == 0))
    def _():
        o_ref[...] = jnp.zeros_like(o_ref)

    @pl.when(f == 0)
    def _():
        def gather(r0, nrows):
            s0, fits = window(r0, nrows)

            @pl.when(fits)
            def _():
                t0 = pl.multiple_of(s0 * MOE_CHUNK, MOE_CHUNK)
                xc[pl.ds(r0, nrows), :] = _dot(pick_rows(r0, nrows, s0, win),
                                               a_ref[pl.ds(t0, win * MOE_CHUNK), :]).astype(BF16)

            @pl.when(jnp.logical_not(fits))
            def _():
                xc[pl.ds(r0, nrows), :] = _dot(pick_rows(r0, nrows, 0, nsub), a_ref[...]).astype(BF16)

            yacc[pl.ds(r0, nrows), :] = jnp.zeros((nrows, d), F32)

        big(gather)

    def ffn(r0, nrows):
        rows = [pl.ds(r0, half), pl.ds(r0 + half, half)] if nrows > half else [pl.ds(r0, nrows)]
        xs = [xc[r, :] for r in rows]
        gate_up = [(_dot(x, wg_ref[0]), _dot(x, wu_ref[0])) for x in xs]
        for r, (gt, up) in zip(rows, gate_up):
            yacc[r, :] += _dot((_silu(gt) * up).astype(BF16), wd_ref[0])

    big(ffn)

    @pl.when(f == pl.num_programs(2) - 1)
    def _():
        lane = lax.broadcasted_iota(I32, (tm, ne), 1)
        rc_s[...] = jnp.sum(jnp.where(lane == e, rank_c_ref[0], 0.0), axis=1, keepdims=True)
        gc_s[...] = jnp.sum(jnp.where(lane == e, gate_c_ref[0], 0.0), axis=1, keepdims=True)

        def put(r0, nrows, t0, ntok):
            rows = pl.ds(t0, ntok)
            tgt = (r0 + lax.broadcasted_iota(I32, (ntok, nrows), 1)).astype(F32)
            place = jnp.where(rc_s[rows, :] == tgt, 1.0, 0.0).astype(BF16)
            o_ref[rows, :] += gc_s[rows, :] * _dot(place, yacc[pl.ds(r0, nrows), :].astype(BF16))

        def scatter(r0, nrows):
            s0, fits = window(r0, nrows)

            @pl.when(fits)
            def _():
                put(r0, nrows, pl.multiple_of(s0 * MOE_CHUNK, MOE_CHUNK), win * MOE_CHUNK)

            @pl.when(jnp.logical_not(fits))
            def _():
                put(r0, nrows, 0, tm)

        big(scatter)


def _residual_kernel(h_ref, y_ref, mod_ref, o_ref):
    o_ref[0] = h_ref[0] + mod_ref[0, 5:6, :] * y_ref[0]


def _moe_ffn(h, g, mod, router, wg, wu, wd, tm, tf):
    bsz, t, d = h.shape
    ne, _, f = wg.shape
    assert ne == N_EXPERTS and tm % MOE_CHUNK == 0
    nt = t // tm
    tok = lambda b, i: (b, i, 0)
    a, gates_t, rank_t, cnt = pl.pallas_call(
        functools.partial(_moe_route_kernel, tm=tm),
        out_shape=(jax.ShapeDtypeStruct((bsz, t, d), BF16),
                   jax.ShapeDtypeStruct((bsz * nt, ne, tm), F32),
                   jax.ShapeDtypeStruct((bsz * nt, ne, tm), F32),
                   jax.ShapeDtypeStruct((bsz * nt, ne, LANES), I32)),
        grid=(bsz, nt),
        in_specs=[pl.BlockSpec((1, tm, d), tok),
                  pl.BlockSpec((1, d), lambda b, i: (0, 0)),
                  pl.BlockSpec((1, 6, d), lambda b, i: (b, 0, 0)),
                  pl.BlockSpec((ne, d), lambda b, i: (0, 0))],
        out_specs=(pl.BlockSpec((1, tm, d), tok),
                   pl.BlockSpec((1, ne, tm), lambda b, i: (b * nt + i, 0, 0)),
                   pl.BlockSpec((1, ne, tm), lambda b, i: (b * nt + i, 0, 0)),
                   pl.BlockSpec((1, ne, LANES), lambda b, i: (b * nt + i, 0, 0))),
        compiler_params=_cparams(("arbitrary", "arbitrary")),
        name="moe_route",
    )(h, g[None, :], mod, router.T)
    nsub = tm // MOE_CHUNK
    counts = cnt[:, :, :nsub + 1].reshape(-1)
    y = pl.pallas_call(
        functools.partial(_moe_expert_kernel, tm=tm, ne=ne),
        out_shape=jax.ShapeDtypeStruct((bsz * t, d), F32),
        grid_spec=pltpu.PrefetchScalarGridSpec(
            num_scalar_prefetch=1,
            grid=(bsz * nt, ne, f // tf),
            in_specs=[pl.BlockSpec((tm, d), lambda i, e, j, cnt: (i, 0)),
                      pl.BlockSpec((1, ne, nsub, MOE_CHUNK), lambda i, e, j, cnt: (i, 0, 0, 0)),
                      pl.BlockSpec((1, tm, ne), lambda i, e, j, cnt: (i, 0, 0)),
                      pl.BlockSpec((1, tm, ne), lambda i, e, j, cnt: (i, 0, 0)),
                      pl.BlockSpec((1, d, tf), lambda i, e, j, cnt: (e, 0, j)),
                      pl.BlockSpec((1, d, tf), lambda i, e, j, cnt: (e, 0, j)),
                      pl.BlockSpec((1, tf, d), lambda i, e, j, cnt: (e, j, 0))],
            out_specs=pl.BlockSpec((tm, d), lambda i, e, j, cnt: (i, 0)),
            scratch_shapes=[pltpu.VMEM((tm, d), BF16), pltpu.VMEM((tm, d), F32),
                            pltpu.VMEM((tm, 1), F32), pltpu.VMEM((tm, 1), F32)]),
        compiler_params=_cparams(("arbitrary", "arbitrary", "arbitrary")),
        name="moe_experts",
    )(counts, a.reshape(bsz * t, d), rank_t.reshape(bsz * nt, ne, nsub, MOE_CHUNK), rank_t.transpose(0, 2, 1),
      gates_t.transpose(0, 2, 1), wg, wu, wd)
    tr = _Tiles().proj
    return pl.pallas_call(
        _residual_kernel,
        out_shape=jax.ShapeDtypeStruct((bsz, t, d), F32),
        grid=(bsz, t // tr),
        in_specs=[pl.BlockSpec((1, tr, d), tok), pl.BlockSpec((1, tr, d), tok),
                  pl.BlockSpec((1, 6, d), lambda b, i: (b, 0, 0))],
        out_specs=pl.BlockSpec((1, tr, d), tok),
        compiler_params=_cparams(("arbitrary", "arbitrary")),
        name="moe_residual",
    )(h, y.reshape(bsz, t, d), mod)


class _Tiles(NamedTuple):
    proj: int = 512
    attn_q: int = 256
    sel_k: int = 512
    s5_rows: int = 256
    ffn_m: int = 1024
    moe_m: int = 2048
    ffn_f: int = 512


def _tiles(t):
    base = _Tiles()
    return base._replace(s5_rows=min(base.s5_rows, t // SSM_CHUNK), ffn_m=min(base.ffn_m, t),
                         moe_m=min(base.moe_m, t))


def _pack_w_in(w_in, dq):
    o1 = 3 * dq
    o2 = o1 + dq
    o3 = o2 + 3 * dq
    sb_q = w_in[:, o2:o2 + dq] * (HEAD_DIM ** -0.5)
    pad = jnp.zeros((w_in.shape[0], GATE_PAD - (w_in.shape[1] - (o3 + dq + 6 * NSA_KV_HEADS * HEAD_DIM))),
                    w_in.dtype)
    cols = [w_in[:, :o1], sb_q, w_in[:, o2 + dq:o3], w_in[:, o1:o2], w_in[:, o3:], pad]
    return jnp.concatenate(cols, axis=1).astype(BF16), o1 + 3 * dq


def _token_mixing(h, mod, positions, rope, g_mix, w_in, w_out, conv_w, ssm, nsa):
    bsz, t, d = h.shape
    dq = d // 4
    tl = _tiles(t)
    w_packed, nb = _pack_w_in(w_in, dq)
    zb, zs, zf = _in_proj(h, g_mix, mod, w_packed, nb, dq, tm=tl.proj)
    y_conv = _short_conv(zb, conv_w, tt=tl.proj)
    y_ssm = _s5(zs, *ssm[:-1], tt=tl.s5_rows)
    y_sb = _stick_breaking(zb, dq, tq=tl.attn_q)
    o_c, o_s, o_w = _nsa_mixer(zf, dq, positions, rope, nsa)
    return _out_proj(h, [y_conv, y_ssm, y_sb, o_c, o_s, o_w], mod, ssm[-1].astype(BF16), w_out.astype(BF16),
                     tm=tl.proj)


def _nsa_mixer(zf, wq, positions, rope, nsa):
    bsz, t, _ = zf.shape
    q_norm_g, k_norm_g, pos_k, pos_v, k_w1, k_w2, v_w1, v_w2 = nsa
    half = HEAD_DIM // 2
    inv_freq = jnp.power(jnp.float32(ROPE_THETA), -jnp.arange(half, dtype=F32) / half)
    tl = _tiles(t)
    q_rot, ck, cv, kaug, vaug, kwin, vwin = _nsa_prep(zf, rope, q_norm_g, k_norm_g, wq, tm=tl.proj)
    nch = t // CMP_STRIDE
    end_idx = jnp.minimum(jnp.arange(nch) * CMP_STRIDE + CMP_LEN - 1, t - 1)
    angc = jnp.tile(positions[:, end_idx].astype(F32)[..., None] * inv_freq, (1, 1, 2))
    kc, vc = _compress(ck, cv, k_w1, k_w2, v_w1, v_w2, pos_k, pos_v, k_norm_g, angc)
    o_c, sel = _cmp_attn(q_rot, kc, vc, zf, tq=tl.attn_q)
    o_s = _sel_attn(q_rot, sel, kaug, vaug, zf, tq=tl.attn_q, tk=tl.sel_k)
    o_w = _win_attn(q_rot, kwin, vwin, zf, tq=tl.attn_q)
    return o_c, o_s, o_w


def kernel(x, c, positions, ada_w, ada_b, norm_mix_g, norm_ffn_g, w_in, w_out, conv_w, ssm_lam_re, ssm_lam_im, ssm_b_re, ssm_b_im, ssm_c_re, ssm_c_im, ssm_d, ssm_log_dt, ssm_w_glu, nsa_q_norm_g, nsa_k_norm_g, cmp_pos_k, cmp_pos_v, cmp_k_w1, cmp_k_w2, cmp_v_w1, cmp_v_w2, ffn_w_gate, ffn_w_up, ffn_w_down, moe_router, moe_w_gate, moe_w_up, moe_w_down):
    depth = ada_w.shape[0]
    mods = _modulation(c, ada_w, ada_b)
    rope = _rope_tables(positions)
    h = x
    for layer in range(depth):
        mod = mods[layer]
        ssm = (ssm_lam_re[layer], ssm_lam_im[layer], ssm_b_re[layer], ssm_b_im[layer], ssm_c_re[layer],
               ssm_c_im[layer], ssm_d[layer], ssm_log_dt[layer], ssm_w_glu[layer])
        nsa = (nsa_q_norm_g[layer], nsa_k_norm_g[layer], cmp_pos_k[layer], cmp_pos_v[layer],
               cmp_k_w1[layer], cmp_k_w2[layer], cmp_v_w1[layer], cmp_v_w2[layer])
        h = _token_mixing(h, mod, positions, rope, norm_mix_g[layer], w_in[layer], w_out[layer],
                          conv_w[layer], ssm, nsa)
        i = layer // 2
        tl = _tiles(h.shape[1])
        if layer % 2 == 0:
            h = _dense_ffn(h, norm_ffn_g[layer], mod, ffn_w_gate[i].astype(BF16), ffn_w_up[i].astype(BF16),
                           ffn_w_down[i].astype(BF16), tm=tl.ffn_m, tf=tl.ffn_f)
        else:
            h = _moe_ffn(h, norm_ffn_g[layer], mod, moe_router[i], moe_w_gate[i].astype(BF16),
                         moe_w_up[i].astype(BF16), moe_w_down[i].astype(BF16), tm=tl.moe_m, tf=tl.ffn_f)
    return h
```

```python
import functools
import math
from typing import NamedTuple

import jax
import jax.numpy as jnp
from jax import lax
from jax.experimental import pallas as pl
from jax.experimental.pallas import tpu as pltpu

F32 = jnp.float32
BF16 = jnp.bfloat16
I32 = jnp.int32

HEAD_DIM = 64
CONV_WIDTH = 3
SSM_GROUP = 16
SSM_STATE = 64
SSM_MAX_RE = -1e-4
NSA_KV_HEADS = 2
CMP_LEN = 32
CMP_STRIDE = 16
SLC_BLOCK = 64
SLC_TOPK = 16
WINDOW = 512
FORCE_BONUS = 1e4
NEG_INF = -1e30
ROPE_THETA = 10000.0
RMS_EPS = 1e-6
N_EXPERTS = 8

LANES = 128
SUBLANES = 8
VMEM_LIMIT = 56 * 1024 * 1024
SEL_MASK_BIAS = -30000.0
SB_SKIP_LOG = -110.0
GATE_PAD = LANES


def _cparams(sem):
    return pltpu.CompilerParams(dimension_semantics=sem, vmem_limit_bytes=VMEM_LIMIT)


def _dot(a, b):
    return jnp.dot(a, b, preferred_element_type=F32)


def _dot_nt(a, b):
    return lax.dot_general(a, b, (((1,), (1,)), ((), ())), preferred_element_type=F32)


def _split(x):
    hi = x.astype(BF16)
    lo = (x - hi.astype(F32)).astype(BF16)
    return hi, lo


def _dot3(a, b):
    ah, al = _split(a)
    bh, bl = _split(b)
    return _dot(ah, bh) + _dot(ah, bl) + _dot(al, bh)


def _dot3_nt(a, b):
    ah, al = _split(a)
    bh, bl = _split(b)
    return _dot_nt(ah, bh) + _dot_nt(ah, bl) + _dot_nt(al, bh)


def _dot2_exact_rhs(a, b_bf16):
    ah, al = _split(a)
    return _dot(ah, b_bf16) + _dot(al, b_bf16)


def _silu(x):
    return x * jax.nn.sigmoid(x)


def _div_pow2(x, n):
    return lax.shift_right_logical(x, jnp.int32(n.bit_length() - 1))


def _mod_pow2(x, n):
    return x & (n - 1)


def _rms_scale(x):
    return lax.rsqrt(jnp.mean(x * x, axis=-1, keepdims=True) + RMS_EPS)


def _mod_kernel(c_ref, w_ref, b_ref, o_ref):
    o_ref[0] = _dot3(_silu(c_ref[...]), w_ref[0]) + b_ref[0]


def _modulation(c, ada_w, ada_b):
    depth, d, n6 = ada_w.shape
    bsz = c.shape[0]
    rows = -(-bsz // SUBLANES) * SUBLANES
    cpad = jnp.zeros((rows, d), F32).at[:bsz].set(c)
    tn = n6 // 4
    out = pl.pallas_call(
        _mod_kernel,
        out_shape=jax.ShapeDtypeStruct((depth, rows, n6), F32),
        grid=(depth, n6 // tn),
        in_specs=[pl.BlockSpec((rows, d), lambda l, j: (0, 0)),
                  pl.BlockSpec((1, d, tn), lambda l, j: (l, 0, j)),
                  pl.BlockSpec((1, 1, tn), lambda l, j: (l, 0, j))],
        out_specs=pl.BlockSpec((1, rows, tn), lambda l, j: (l, 0, j)),
        compiler_params=_cparams(("arbitrary", "arbitrary")),
        name="adaln_mod",
    )(cpad, ada_w, ada_b[:, None, :])
    return out[:, :bsz].reshape(depth, bsz, 6, d)


def _in_kernel(h_ref, g_ref, mod_ref, w_ref, zb_ref, zs_ref, zf_ref):
    x = h_ref[0]
    y = x * _rms_scale(x) * g_ref[...]
    a = y * (1.0 + mod_ref[0, 1:2, :]) + mod_ref[0, 0:1, :]
    z = _dot(a.astype(BF16), w_ref[...])
    nb = zb_ref.shape[-1]
    ns = zs_ref.shape[-1]
    zb_ref[0] = z[:, :nb].astype(BF16)
    zs_ref[0] = z[:, nb:nb + ns]
    zf_ref[0] = z[:, nb + ns:]


def _in_proj(h, g, mod, w_bf16, nb, ns, tm):
    bsz, t, d = h.shape
    nz = w_bf16.shape[1]
    widths = (nb, ns, nz - nb - ns)
    return pl.pallas_call(
        _in_kernel,
        out_shape=tuple(jax.ShapeDtypeStruct((bsz, t, w), dt) for w, dt in zip(widths, (BF16, F32, F32))),
        grid=(bsz, t // tm),
        in_specs=[pl.BlockSpec((1, tm, d), lambda b, i: (b, i, 0)),
                  pl.BlockSpec((1, d), lambda b, i: (0, 0)),
                  pl.BlockSpec((1, 6, d), lambda b, i: (b, 0, 0)),
                  pl.BlockSpec((d, nz), lambda b, i: (0, 0))],
        out_specs=tuple(pl.BlockSpec((1, tm, w), lambda b, i: (b, i, 0)) for w in widths),
        compiler_params=_cparams(("arbitrary", "arbitrary")),
        name="in_proj",
    )(h, g[None, :], mod, w_bf16)


def _conv_kernel(z_ref, w_ref, o_ref, buf, *, tt, dc):
    @pl.when(pl.program_id(1) == 0)
    def _():
        buf[0:SUBLANES, :] = jnp.zeros((SUBLANES, dc), F32)

    z = z_ref[0].astype(F32)
    gate_b = z[:, :dc]
    v = z[:, dc:2 * dc] * z[:, 2 * dc:]
    buf[SUBLANES:, :] = v
    y = (w_ref[2:3, :] * v
         + w_ref[1:2, :] * buf[SUBLANES - 1:SUBLANES - 1 + tt, :]
         + w_ref[0:1, :] * buf[SUBLANES - 2:SUBLANES - 2 + tt, :])
    o_ref[0] = (gate_b * y).astype(o_ref.dtype)
    buf[0:SUBLANES, :] = v[tt - SUBLANES:, :]


def _short_conv(zb, conv_w, tt):
    bsz, t, _ = zb.shape
    dc = conv_w.shape[1]
    c3 = 3 * dc
    return pl.pallas_call(
        functools.partial(_conv_kernel, tt=tt, dc=dc),
        out_shape=jax.ShapeDtypeStruct((bsz, t, dc), BF16),
        grid=(bsz, t // tt),
        in_specs=[pl.BlockSpec((1, tt, c3), lambda b, i: (b, i, 0)),
                  pl.BlockSpec((CONV_WIDTH, dc), lambda b, i: (0, 0))],
        out_specs=pl.BlockSpec((1, tt, dc), lambda b, i: (b, i, 0)),
        scratch_shapes=[pltpu.VMEM((tt + SUBLANES, dc), F32)],
        compiler_params=_cparams(("arbitrary", "arbitrary")),
        name="short_conv",
    )(zb, conv_w)


SSM_LANE_CHUNK = 2 * LANES


def _gelu_tanh(x):
    return 0.5 * x * (1.0 + jnp.tanh(math.sqrt(2.0 / math.pi) * (x + 0.044715 * (x * x * x))))


def _ssm_kernel(u_ref, tz_ref, we_ref, wd_ref, pre_ref, pim_ref, d_ref, o_ref,
                bre, bim, cin, cre, cim, *, tt, ns):
    @pl.when(pl.program_id(1) == 0)
    def _():
        cre[...] = jnp.zeros_like(cre)
        cim[...] = jnp.zeros_like(cim)

    u = u_ref[0]
    ub = u.astype(BF16)
    end = _dot(ub, we_ref[...])
    bre[...] = end[:, :ns]
    bim[...] = end[:, ns:]
    lw = SSM_LANE_CHUNK
    row = lax.broadcasted_iota(I32, (SUBLANES, lw), 0)

    def local_scan(i, carry):
        r0 = pl.multiple_of(i * SUBLANES, SUBLANES)
        for lg in range(ns // lw):
            sl = slice(lg * lw, (lg + 1) * lw)
            xr = bre[pl.ds(r0, SUBLANES), sl]
            xi = bim[pl.ds(r0, SUBLANES), sl]
            for k in (1, 2, 4):
                ar = pre_ref[k - 1:k, sl]
                ai = pim_ref[k - 1:k, sl]
                sr = jnp.where(row >= k, pltpu.roll(xr, k, 0), 0.0)
                si = jnp.where(row >= k, pltpu.roll(xi, k, 0), 0.0)
                xr, xi = xr + ar * sr - ai * si, xi + ar * si + ai * sr
            bre[pl.ds(r0, SUBLANES), sl] = xr
            bim[pl.ds(r0, SUBLANES), sl] = xi
        return carry

    lax.fori_loop(0, tt // SUBLANES, local_scan, 0, unroll=2)

    row_all = lax.broadcasted_iota(I32, (SUBLANES, ns), 0)

    def carry_in(i, carry):
        c_re, c_im = carry
        r0 = pl.multiple_of(i * SUBLANES, SUBLANES)
        pr = pre_ref[...]
        pi_ = pim_ref[...]
        xr = bre[pl.ds(r0, SUBLANES), :] + pr * c_re - pi_ * c_im
        xi = bim[pl.ds(r0, SUBLANES), :] + pr * c_im + pi_ * c_re
        cin[pl.ds(r0, SUBLANES), :ns] = jnp.where(row_all == 0, c_re, pltpu.roll(xr, 1, 0))
        cin[pl.ds(r0, SUBLANES), ns:] = jnp.where(row_all == 0, c_im, pltpu.roll(xi, 1, 0))
        return xr[SUBLANES - 1:], xi[SUBLANES - 1:]

    c_re, c_im = lax.fori_loop(0, tt // SUBLANES, carry_in, (cre[...], cim[...]))
    cre[...] = c_re
    cim[...] = c_im
    o_ref[0] = _dot(ub, tz_ref[...]) + _dot(cin[...].astype(BF16), wd_ref[...]) + d_ref[...] * u


SSM_CHUNK = 8


def _s5(u, lam_re, lam_im, b_re, b_im, c_re, c_im, d_skip, log_dt, tt):
    bsz, t, ds = u.shape
    g, p = lam_re.shape
    hc = SSM_GROUP
    ns = g * p
    nl = SSM_CHUNK
    lam = lax.complex(jnp.minimum(lam_re, SSM_MAX_RE), lam_im)
    dt = jnp.exp(log_dt)[:, None]
    lam_bar = jnp.exp(lam * dt)
    b_bar = ((lam_bar - 1.0) / lam)[..., None] * lax.complex(b_re, b_im)
    cmat = lax.complex(c_re, c_im)
    steps = jnp.arange(nl + 1, dtype=F32)[:, None, None]
    pw = jnp.exp(steps * (lam * dt)[None])
    in_grp = (jnp.arange(nl * ds) // hc) % g
    st_grp = jnp.arange(ns) // p
    rep = lambda n, m: (jnp.arange(m)[None, :] % n == jnp.arange(n)[:, None]).astype(F32)
    kern = jnp.einsum('gcp,tgp,gpd->tgdc', cmat, pw[:nl], b_bar).real.reshape(nl, ds, hc)
    zero = jnp.zeros((ds, hc), F32)
    rows_rc = jnp.concatenate(
        [jnp.concatenate([zero] * s + [kern[r - s] for r in range(s, nl)], axis=1) for s in range(nl)], axis=0)
    col_rc = (jnp.arange(nl * ds) // ds) * hc + jnp.arange(nl * ds) % hc
    spread_rc = (col_rc[None, :] == jnp.arange(nl * hc)[:, None]).astype(F32)
    tz = jnp.dot(rows_rc, spread_rc) * (in_grp[:, None] == in_grp[None, :])
    end = (pw[:nl][::-1][..., None] * b_bar[None]).transpose(0, 1, 3, 2).reshape(nl * ds, p)
    same_in_st = in_grp[:, None] == st_grp[None, :]
    we = jnp.concatenate([jnp.dot(part, rep(p, ns)) * same_in_st for part in (end.real, end.imag)], axis=1)
    dec = (cmat[None] * pw[1:, :, None, :]).transpose(3, 0, 1, 2).reshape(p, nl * ds)
    wd = jnp.concatenate([jnp.dot(rep(p, ns).T, part) * same_in_st.T for part in (dec.real, -dec.imag)], axis=0)
    pows = jnp.exp(jnp.arange(1, SUBLANES + 1, dtype=F32)[:, None, None] * nl * (lam * dt)[None])
    pows = pows.reshape(SUBLANES, ns)
    rows = t // nl
    const = lambda shape: pl.BlockSpec(shape, lambda b, i: (0, 0), pipeline_mode=pl.Buffered(1))
    y = pl.pallas_call(
        functools.partial(_ssm_kernel, tt=tt, ns=ns),
        out_shape=jax.ShapeDtypeStruct((bsz, rows, nl * ds), F32),
        grid=(bsz, rows // tt),
        in_specs=[pl.BlockSpec((1, tt, nl * ds), lambda b, i: (b, i, 0)),
                  const((nl * ds, nl * ds)), const((nl * ds, 2 * ns)), const((2 * ns, nl * ds)),
                  pl.BlockSpec((SUBLANES, ns), lambda b, i: (0, 0)),
                  pl.BlockSpec((SUBLANES, ns), lambda b, i: (0, 0)),
                  pl.BlockSpec((1, nl * ds), lambda b, i: (0, 0))],
        out_specs=pl.BlockSpec((1, tt, nl * ds), lambda b, i: (b, i, 0)),
        scratch_shapes=[pltpu.VMEM((tt, ns), F32), pltpu.VMEM((tt, ns), F32), pltpu.VMEM((tt, 2 * ns), F32),
                        pltpu.VMEM((1, ns), F32), pltpu.VMEM((1, ns), F32)],
        compiler_params=_cparams(("arbitrary", "arbitrary")),
        name="s5_scan",
    )(u.reshape(bsz, rows, nl * ds), tz.astype(BF16), we.astype(BF16), wd.astype(BF16), pows.real, pows.imag,
      jnp.tile(d_skip, nl)[None, :])
    return y.reshape(bsz, t, ds)


def _sb_kernel(q_ref, k_ref, v_ref, o_ref, acc, csum, *, tq, nh):
    qi = pl.program_id(1)
    acc[...] = jnp.zeros_like(acc)
    csum[...] = jnp.zeros_like(csum)
    ti = lax.broadcasted_iota(I32, (tq, tq), 0)
    ji = lax.broadcasted_iota(I32, (tq, tq), 1)
    later = (ti > ji).astype(BF16)

    def cond(state):
        kb, cmax = state
        return jnp.logical_and(kb >= 0, cmax > SB_SKIP_LOG)

    def block(kb, diagonal):
        k0 = pl.multiple_of(kb * tq, tq)
        past = ji < ti
        keep = (lambda x: jnp.where(past, x, 0.0)) if diagonal else (lambda x: x)
        heads = [slice(h * HEAD_DIM, (h + 1) * HEAD_DIM) for h in range(nh)]
        zs = [_dot_nt(q_ref[0, :, hs], k_ref[0, pl.ds(k0, tq), hs]) for hs in heads]
        log_betas, log_keeps, inners = [], [], []
        for z in zs:
            log_beta = jnp.minimum(z, 0.0) - jnp.log(1.0 + jnp.exp(-jnp.abs(z)))
            log_keep = keep(log_beta - z)
            hi, lo = _split(log_keep)
            log_betas.append(log_beta)
            log_keeps.append(log_keep)
            inners.append(_dot(hi, later) + _dot(lo, later))
        cmax = jnp.float32(-jnp.inf)
        for h, hs in enumerate(heads):
            c = csum[h]
            w = keep(jnp.exp(log_betas[h] + inners[h] + c[:, :1]))
            acc[:, hs] += _dot(w.astype(BF16), v_ref[0, pl.ds(k0, tq), hs])
            c = c + jnp.sum(log_keeps[h], axis=1, keepdims=True)
            csum[h] = c
            cmax = jnp.maximum(cmax, jnp.max(c))
        return cmax

    def body(state):
        kb, _ = state
        return kb - 1, block(kb, False)

    lax.while_loop(cond, body, (qi - 1, block(qi, True)))
    o_ref[0] = acc[...].astype(o_ref.dtype)


SB_Q_BLOCK, SB_K_BLOCK, SB_V_BLOCK = 3, 4, 5


def _stick_breaking(zb, dq, tq):
    bsz, t, _ = zb.shape
    nh = dq // HEAD_DIM
    return pl.pallas_call(
        functools.partial(_sb_kernel, tq=tq, nh=nh),
        out_shape=jax.ShapeDtypeStruct((bsz, t, dq), BF16),
        grid=(bsz, t // tq),
        in_specs=[pl.BlockSpec((1, tq, dq), lambda b, i: (b, i, SB_Q_BLOCK)),
                  pl.BlockSpec((1, t, dq), lambda b, i: (b, 0, SB_K_BLOCK)),
                  pl.BlockSpec((1, t, dq), lambda b, i: (b, 0, SB_V_BLOCK))],
        out_specs=pl.BlockSpec((1, tq, dq), lambda b, i: (b, i, 0)),
        scratch_shapes=[pltpu.VMEM((tq, dq), F32), pltpu.VMEM((nh, tq, LANES), F32)],
        compiler_params=_cparams(("arbitrary", "arbitrary")),
        name="stick_breaking",
    )(zb, zb, zb)


def _head_norm(x, gsum_ref, gain):
    ss = _dot2_exact_rhs(x * x, gsum_ref[...]) * (1.0 / HEAD_DIM)
    return x * lax.rsqrt(ss + RMS_EPS) * gain


def _rope_lanes(x, cos, sin):
    w = x.shape[-1]
    lane = lax.broadcasted_iota(I32, x.shape, 1)
    first = _mod_pow2(lane, HEAD_DIM) < (HEAD_DIM // 2)
    partner = jnp.where(first, -pltpu.roll(x, w - HEAD_DIM // 2, 1), pltpu.roll(x, HEAD_DIM // 2, 1))
    return x * cos + partner * sin


def _nsa_prep_kernel(zq_ref, zkc_ref, zvc_ref, zks_ref, zvs_ref, zkw_ref, zvw_ref, cos_ref, sin_ref, gq_ref,
                     gk_ref, gsq_ref, gsk_ref, q_ref, ck_ref, cv_ref, kaug_ref, vaug_ref, kw_ref, vw_ref,
                     *, tm, ngrp, nblk):
    cos1 = cos_ref[0]
    sin1 = sin_ref[0]
    wq = zq_ref.shape[-1]
    cosq = jnp.concatenate([cos1] * (wq // LANES), axis=1)
    sinq = jnp.concatenate([sin1] * (wq // LANES), axis=1)
    q = _rope_lanes(_head_norm(zq_ref[0], gsq_ref, gq_ref[...]), cosq, sinq)
    q_ref[0] = q * (HEAD_DIM ** -0.5)
    ks = _rope_lanes(_head_norm(zks_ref[0], gsk_ref, gk_ref[...]), cos1, sin1)
    kw = _rope_lanes(_head_norm(zkw_ref[0], gsk_ref, gk_ref[...]), cos1, sin1)
    t0 = pl.program_id(1) * tm
    tok = t0 + lax.broadcasted_iota(I32, (tm, nblk), 0)
    blk = lax.broadcasted_iota(I32, (tm, nblk), 1)
    onehot = jnp.where(_div_pow2(tok, SLC_BLOCK) == blk, 1.0, 0.0).astype(BF16)
    vs = zvs_ref[0]
    vw = zvw_ref[0]
    lane = lax.broadcasted_iota(I32, (tm, LANES - HEAD_DIM), 1)
    ones_col = jnp.where(lane == 0, 1.0, 0.0).astype(BF16)
    for g in range(ngrp):
        sl = slice(g * HEAD_DIM, (g + 1) * HEAD_DIM)
        kaug_ref[0, g] = jnp.concatenate([onehot, ks[:, sl].astype(BF16)], axis=1)
        vaug_ref[0, g] = jnp.concatenate([vs[:, sl].astype(BF16), ones_col], axis=1)
        kw_ref[0, g] = kw[:, sl].astype(BF16)
        vw_ref[0, g] = jnp.concatenate([vw[:, sl].astype(BF16), ones_col], axis=1)
    for j in range(CMP_STRIDE):
        rows = pl.ds(j, tm // CMP_STRIDE, stride=CMP_STRIDE)
        js = slice(j * HEAD_DIM, (j + 1) * HEAD_DIM)
        kj = zkc_ref[0, rows, :]
        vj = zvc_ref[0, rows, :]
        for g in range(ngrp):
            sl = slice(g * HEAD_DIM, (g + 1) * HEAD_DIM)
            ck_ref[0, g, :, js] = kj[:, sl]
            cv_ref[0, g, :, js] = vj[:, sl]


ZF_Q_BLOCK = 0
ZF_KCMP, ZF_VCMP, ZF_KSLC, ZF_VSLC, ZF_KWIN, ZF_VWIN, ZF_GATE = 2, 3, 4, 5, 6, 7, 8


def _rope_kernel(ang_ref, cos_ref, sin_ref):
    ang = ang_ref[...]
    cos_ref[...] = jnp.cos(ang)
    sin_ref[...] = jnp.sin(ang)


def _rope_tables(positions):
    bsz, t = positions.shape
    half = HEAD_DIM // 2
    inv_freq = jnp.power(jnp.float32(ROPE_THETA), -jnp.arange(half, dtype=F32) / half)
    ang = (positions.astype(F32)[..., None] * inv_freq).reshape(bsz, t * half // LANES, LANES)
    rows = ang.shape[1]
    spec = pl.BlockSpec((1, rows, LANES), lambda b: (b, 0, 0))
    cos, sin = pl.pallas_call(
        _rope_kernel,
        out_shape=(jax.ShapeDtypeStruct(ang.shape, F32), jax.ShapeDtypeStruct(ang.shape, F32)),
        grid=(bsz,),
        in_specs=[spec],
        out_specs=(spec, spec),
        compiler_params=_cparams(("arbitrary",)),
        name="rope_tables",
    )(ang)
    widen = lambda x: jnp.tile(x.reshape(bsz, t, half), (1, 1, LANES // half))
    return widen(cos), widen(sin)


def _nsa_prep(zf, rope, gq, gk, wq, tm):
    bsz, t, _ = zf.shape
    cos, sin = rope
    ngrp = NSA_KV_HEADS
    nblk = t // SLC_BLOCK
    wk = ngrp * HEAD_DIM
    assert wk == LANES and wq == 2 * wk
    lane_q = jnp.arange(wq) // HEAD_DIM
    gsq = (lane_q[:, None] == lane_q[None, :]).astype(BF16)
    lane_k = jnp.arange(wk) // HEAD_DIM
    gsk = (lane_k[:, None] == lane_k[None, :]).astype(BF16)
    kvspec = lambda blk: pl.BlockSpec((1, tm, wk), lambda b, i: (b, i, blk))
    hspec = lambda w: pl.BlockSpec((1, ngrp, tm, w), lambda b, i: (b, 0, i, 0))
    cspec = pl.BlockSpec((1, ngrp, tm // CMP_STRIDE, CMP_STRIDE * HEAD_DIM), lambda b, i: (b, 0, i, 0))
    return pl.pallas_call(
        functools.partial(_nsa_prep_kernel, tm=tm, ngrp=ngrp, nblk=nblk),
        out_shape=(jax.ShapeDtypeStruct((bsz, t, wq), F32),
                   jax.ShapeDtypeStruct((bsz, ngrp, t // CMP_STRIDE, CMP_STRIDE * HEAD_DIM), F32),
                   jax.ShapeDtypeStruct((bsz, ngrp, t // CMP_STRIDE, CMP_STRIDE * HEAD_DIM), F32),
                   jax.ShapeDtypeStruct((bsz, ngrp, t, HEAD_DIM + nblk), BF16),
                   jax.ShapeDtypeStruct((bsz, ngrp, t, LANES), BF16),
                   jax.ShapeDtypeStruct((bsz, ngrp, t, HEAD_DIM), BF16),
                   jax.ShapeDtypeStruct((bsz, ngrp, t, LANES), BF16)),
        grid=(bsz, t // tm),
        in_specs=[pl.BlockSpec((1, tm, wq), lambda b, i: (b, i, ZF_Q_BLOCK)),
                  kvspec(ZF_KCMP), kvspec(ZF_VCMP),
                  kvspec(ZF_KSLC), kvspec(ZF_VSLC), kvspec(ZF_KWIN), kvspec(ZF_VWIN),
                  pl.BlockSpec((1, tm, LANES), lambda b, i: (b, i, 0)),
                  pl.BlockSpec((1, tm, LANES), lambda b, i: (b, i, 0)),
                  pl.BlockSpec((1, wq), lambda b, i: (0, 0)),
                  pl.BlockSpec((1, wk), lambda b, i: (0, 0)),
                  pl.BlockSpec((wq, wq), lambda b, i: (0, 0)),
                  pl.BlockSpec((wk, wk), lambda b, i: (0, 0))],
        out_specs=(pl.BlockSpec((1, tm, wq), lambda b, i: (b, i, 0)),
                   cspec, cspec,
                   hspec(HEAD_DIM + nblk), hspec(LANES), hspec(HEAD_DIM), hspec(LANES)),
        compiler_params=_cparams(("arbitrary", "arbitrary")),
        name="nsa_prep",
    )(zf, zf, zf, zf, zf, zf, zf, cos, sin, jnp.tile(gq, wq // HEAD_DIM)[None, :], jnp.tile(gk, ngrp)[None, :],
      gsq, gsk)


def _compress_kernel(ck_ref, cv_ref, w1k_ref, w2k_ref, w1v_ref, w2v_ref, pek_ref, pev_ref,
                     gk_ref, angc_ref, kc_ref, vc_ref, *, nch, half):
    def mlp(c_ref, w1_ref, w2_ref, pe_ref):
        x = c_ref[0, 0]
        w1 = w1_ref[...]
        first = _dot3(x, w1[:half])
        second = _dot3(x, w1[half:])
        bias = _dot3(pe_ref[...], w1)[0:1]
        hid = _silu(first + pltpu.roll(second, nch - 1, 0) + bias)
        return _dot3(hid, w2_ref[...])

    kc = mlp(ck_ref, w1k_ref, w2k_ref, pek_ref)
    kc = kc * _rms_scale(kc) * gk_ref[...]
    ang = angc_ref[0]
    half_d = HEAD_DIM // 2
    k1 = kc[:, :half_d]
    k2 = kc[:, half_d:]
    cos = jnp.cos(ang[:, :half_d])
    sin = jnp.sin(ang[:, :half_d])
    kc_ref[0, 0] = jnp.concatenate([k1 * cos - k2 * sin, k2 * cos + k1 * sin], axis=1)
    vc_ref[0, 0] = mlp(cv_ref, w1v_ref, w2v_ref, pev_ref)


def _compress(ck, cv, w1k, w2k, w1v, w2v, pos_k, pos_v, gk, angc):
    bsz, ngrp, nch, half = ck.shape
    hid = w1k.shape[1]

    def pe_rows(pe):
        return jnp.zeros((SUBLANES, 2 * half), F32).at[0].set(pe.reshape(-1))

    blk4 = lambda b, g: (b, g, 0, 0)
    full2 = lambda b, g: (0, 0)
    return pl.pallas_call(
        functools.partial(_compress_kernel, nch=nch, half=half),
        out_shape=(jax.ShapeDtypeStruct((bsz, ngrp, nch, HEAD_DIM), F32),
                   jax.ShapeDtypeStruct((bsz, ngrp, nch, HEAD_DIM), F32)),
        grid=(bsz, ngrp),
        in_specs=[pl.BlockSpec((1, 1, nch, half), blk4),
                  pl.BlockSpec((1, 1, nch, half), blk4),
                  pl.BlockSpec((2 * half, hid), full2),
                  pl.BlockSpec((hid, HEAD_DIM), full2),
                  pl.BlockSpec((2 * half, hid), full2),
                  pl.BlockSpec((hid, HEAD_DIM), full2),
                  pl.BlockSpec((SUBLANES, 2 * half), full2),
                  pl.BlockSpec((SUBLANES, 2 * half), full2),
                  pl.BlockSpec((1, HEAD_DIM), full2),
                  pl.BlockSpec((1, nch, HEAD_DIM), lambda b, g: (b, 0, 0))],
        out_specs=(pl.BlockSpec((1, 1, nch, HEAD_DIM), blk4),
                   pl.BlockSpec((1, 1, nch, HEAD_DIM), blk4)),
        compiler_params=_cparams(("arbitrary", "arbitrary")),
        name="nsa_compress",
    )(ck, cv, w1k, w2k, w1v, w2v, pe_rows(pos_k), pe_rows(pos_v), gk[None, :], angc)


def _gate_col(gates, head, branch):
    c = head * 3 + branch
    if isinstance(c, int):
        return gates[:, c:c + 1]
    lane = lax.broadcasted_iota(I32, gates.shape, 1)
    return jnp.sum(jnp.where(lane == c, gates, 0.0), axis=1, keepdims=True)


def _cmp_attn_kernel(q_ref, kc_ref, vct_ref, gl_ref, ovt_ref, oc_ref, sel_ref, *, tq, ngrp, hpg, nch, nblk):
    t = pl.program_id(1) * tq + lax.broadcasted_iota(I32, (1, tq), 1)
    cmp_end = lax.broadcasted_iota(I32, (nch, 1), 0) * CMP_STRIDE + (CMP_LEN - 1)
    visible = cmp_end <= t
    q = q_ref[0]
    gates = jax.nn.sigmoid(gl_ref[0])
    blk = lax.broadcasted_iota(I32, (nblk, tq), 0)
    cur = _div_pow2(t, SLC_BLOCK)
    forced = (blk == 0) | (blk == cur) | (blk == cur - 1)
    causal = blk <= cur
    lane = lax.broadcasted_iota(I32, (tq, hpg * HEAD_DIM), 1)
    outs = []
    for g in range(ngrp):
        kc = kc_ref[0, g]
        vct = vct_ref[0, g]
        psum = jnp.zeros((nch, tq), F32)
        o_t = []
        for hh in range(hpg):
            head = g * hpg + hh
            qh = q[:, head * HEAD_DIM:(head + 1) * HEAD_DIM]
            s = jnp.where(visible, _dot3_nt(kc, qh), NEG_INF)
            p = jnp.exp(s - jnp.maximum(jnp.max(s, axis=0, keepdims=True), 0.1 * NEG_INF))
            denom = jnp.sum(p, axis=0, keepdims=True)
            p = p * (1.0 / jnp.where(denom == 0.0, 1.0, denom))
            psum = psum + p
            o_t.append(_dot(vct, p.astype(BF16)))
        gate = _gate_col(gates, g * hpg, 0)
        for hh in range(1, hpg):
            gate = jnp.where(lane < hh * HEAD_DIM, gate, _gate_col(gates, g * hpg + hh, 0))
        outs.append(jnp.concatenate(o_t, axis=0).T * gate)
        hi, lo = _split(psum)
        imp = _dot(ovt_ref[...], hi) + _dot(ovt_ref[...], lo)
        val = jnp.where(causal, imp + jnp.where(forced, FORCE_BONUS, 0.0), NEG_INF)
        for _ in range(min(SLC_TOPK, nblk)):
            m = jnp.max(val, axis=0, keepdims=True)
            first = jnp.min(jnp.where(val == m, blk, nblk), axis=0, keepdims=True)
            val = jnp.where(blk == first, -jnp.inf, val)
        sel_ref[0, g] = jnp.where(val == -jnp.inf, 0.0, SEL_MASK_BIAS).T.astype(BF16)
    oc_ref[0] = jnp.concatenate(outs, axis=1).astype(oc_ref.dtype)


def _cmp_attn(q, kc, vc, glog, tq):
    bsz, t, wq = q.shape
    ngrp, nch = kc.shape[1], kc.shape[2]
    hpg = wq // HEAD_DIM // ngrp
    nblk = t // SLC_BLOCK
    cs = jnp.arange(nch) * CMP_STRIDE
    ss = jnp.arange(nblk) * SLC_BLOCK
    ovt = ((cs[None, :] < ss[:, None] + SLC_BLOCK) & (cs[None, :] + CMP_LEN > ss[:, None])).astype(BF16)
    return pl.pallas_call(
        functools.partial(_cmp_attn_kernel, tq=tq, ngrp=ngrp, hpg=hpg, nch=nch, nblk=nblk),
        out_shape=(jax.ShapeDtypeStruct((bsz, t, wq), BF16),
                   jax.ShapeDtypeStruct((bsz, ngrp, t, nblk), BF16)),
        grid=(bsz, t // tq),
        in_specs=[pl.BlockSpec((1, tq, wq), lambda b, i: (b, i, 0)),
                  pl.BlockSpec((1, ngrp, nch, HEAD_DIM), lambda b, i: (b, 0, 0, 0)),
                  pl.BlockSpec((1, ngrp, HEAD_DIM, nch), lambda b, i: (b, 0, 0, 0)),
                  pl.BlockSpec((1, tq, GATE_PAD), lambda b, i: (b, i, ZF_GATE)),
                  pl.BlockSpec((nblk, nch), lambda b, i: (0, 0))],
        out_specs=(pl.BlockSpec((1, tq, wq), lambda b, i: (b, i, 0)),
                   pl.BlockSpec((1, ngrp, tq, nblk), lambda b, i: (b, 0, i, 0))),
        compiler_params=_cparams(("arbitrary", "arbitrary")),
        name="nsa_cmp_select",
    )(q, kc, vc.transpose(0, 1, 3, 2).astype(BF16), glog, ovt)


SEL_ROW_CHUNK = 512
WIN_ROW_CHUNK = 128


def _sel_attn_kernel(q_ref, sel_ref, kaug_ref, vaug_ref, gl_ref, o_ref, qa_s, m_s, acc, s_a, s_b, *, tq, tk, hpg):
    g = pl.program_id(1)
    qi = pl.program_id(2)
    q = q_ref[0]
    sel = sel_ref[0, 0]
    for hh in range(hpg):
        qa_s[hh * tq:(hh + 1) * tq, :] = jnp.concatenate(
            [sel, q[:, hh * HEAD_DIM:(hh + 1) * HEAD_DIM].astype(BF16)], axis=1)
    m_s[...] = jnp.full_like(m_s, NEG_INF)
    acc[...] = jnp.zeros_like(acc)
    rows = hpg * tq
    rc = SEL_ROW_CHUNK

    def scores(kt, buf):
        k0 = pl.multiple_of(kt * tk, tk)
        kblk = kaug_ref[0, 0, pl.ds(k0, tk), :]
        for c in range(rows // rc):
            rs = slice(c * rc, (c + 1) * rc)
            buf[rs, :] = _dot_nt(qa_s[rs, :], kblk)

    def consume(kt, buf, diag):
        k0 = pl.multiple_of(kt * tk, tk)
        vblk = vaug_ref[0, 0, pl.ds(k0, tk), :]
        for c in range(rows // rc):
            rs = slice(c * rc, (c + 1) * rc)
            s = buf[rs, :]
            if diag:
                t = qi * tq + _mod_pow2(c * rc + lax.broadcasted_iota(I32, (rc, tk), 0), tq)
                s = jnp.where(k0 + lax.broadcasted_iota(I32, (rc, tk), 1) <= t, s, NEG_INF)
            m_old = m_s[rs, :]
            m_new = jnp.maximum(m_old, jnp.max(s, axis=1, keepdims=True))
            alpha = jnp.exp(m_old - m_new)
            p = jnp.exp(s - m_new[:, :1])
            acc[rs, :] = alpha * acc[rs, :] + _dot(p.astype(BF16), vblk)
            m_s[rs, :] = m_new

    last = (qi * tq) // tk
    scores(0, s_a)

    def two_tiles(k):
        scores(k + 1, s_b)
        consume(k, s_a, False)
        scores(k + 2, s_a)
        consume(k + 1, s_b, False)

    def quad(j, carry):
        two_tiles(4 * j)
        two_tiles(4 * j + 2)
        return carry

    def pair(j, carry):
        two_tiles(4 * (last // 4) + 2 * j)
        return carry

    lax.fori_loop(0, last // 4, quad, 0)
    lax.fori_loop(0, (last % 4) // 2, pair, 0)
    t0 = (last // 2) * 2

    @pl.when(t0 < last)
    def _():
        scores(last, s_b)
        consume(t0, s_a, False)
        consume(last, s_b, True)

    @pl.when(t0 == last)
    def _():
        consume(last, s_a, True)

    gates = jax.nn.sigmoid(gl_ref[0])
    a = acc[...]
    out = a[:, :HEAD_DIM] / a[:, HEAD_DIM:HEAD_DIM + 1]
    o_ref[0] = jnp.concatenate(
        [out[hh * tq:(hh + 1) * tq] * _gate_col(gates, g * hpg + hh, 1) for hh in range(hpg)],
        axis=1).astype(o_ref.dtype)


def _sel_attn(q, sel, kaug, vaug, glog, tq, tk):
    bsz, t, wq = q.shape
    ngrp = kaug.shape[1]
    hpg = wq // HEAD_DIM // ngrp
    gw = hpg * HEAD_DIM
    nblk = sel.shape[-1]
    ka = kaug.shape[-1]
    rows = hpg * tq
    assert tk % tq == 0 and rows % SEL_ROW_CHUNK == 0
    return pl.pallas_call(
        functools.partial(_sel_attn_kernel, tq=tq, tk=tk, hpg=hpg),
        out_shape=jax.ShapeDtypeStruct((bsz, t, wq), BF16),
        grid=(bsz, ngrp, t // tq),
        in_specs=[pl.BlockSpec((1, tq, gw), lambda b, g, i: (b, i, g)),
                  pl.BlockSpec((1, 1, tq, nblk), lambda b, g, i: (b, g, i, 0)),
                  pl.BlockSpec((1, 1, t, ka), lambda b, g, i: (b, g, 0, 0)),
                  pl.BlockSpec((1, 1, t, LANES), lambda b, g, i: (b, g, 0, 0)),
                  pl.BlockSpec((1, tq, GATE_PAD), lambda b, g, i: (b, i, ZF_GATE))],
        out_specs=pl.BlockSpec((1, tq, gw), lambda b, g, i: (b, i, g)),
        scratch_shapes=[pltpu.VMEM((rows, ka), BF16), pltpu.VMEM((rows, LANES), F32),
                        pltpu.VMEM((rows, LANES), F32),
                        pltpu.VMEM((rows, tk), F32), pltpu.VMEM((rows, tk), F32)],
        compiler_params=_cparams(("arbitrary", "arbitrary", "arbitrary")),
        name="nsa_selected",
    )(q, sel, kaug, vaug, glog)


def _win_attn_kernel(q_ref, k_ref, v_ref, gl_ref, bias_ref, o_ref, *, tq, hpg, span):
    g = pl.program_id(1)
    qi = pl.program_id(2)
    q = q_ref[0]
    q2 = jnp.concatenate([q[:, hh * HEAD_DIM:(hh + 1) * HEAD_DIM] for hh in range(hpg)], axis=0).astype(BF16)
    base = pl.multiple_of(jnp.maximum(qi * tq - WINDOW, 0), tq)
    kblk = k_ref[0, 0, pl.ds(base, span), :]
    vblk = v_ref[0, 0, pl.ds(base, span), :]
    gates = jax.nn.sigmoid(gl_ref[0])

    def attend(bias):
        rc = WIN_ROW_CHUNK
        chunks = [slice(c * rc, (c + 1) * rc) for c in range(hpg * tq // rc)]
        scores = [_dot_nt(q2[rs], kblk) + bias[rs] for rs in chunks]
        outs = []
        for s in scores:
            p = jnp.exp(s - jnp.max(s, axis=1, keepdims=True))
            acc = _dot(p.astype(BF16), vblk)
            outs.append(acc[:, :HEAD_DIM] / acc[:, HEAD_DIM:HEAD_DIM + 1])
        out = jnp.concatenate(outs, axis=0)
        o_ref[0] = jnp.concatenate(
            [out[hh * tq:(hh + 1) * tq] * _gate_col(gates, g * hpg + hh, 2) for hh in range(hpg)],
            axis=1).astype(o_ref.dtype)

    @pl.when(qi * tq >= WINDOW)
    def _():
        attend(jnp.concatenate([bias_ref[...]] * hpg, axis=0))

    @pl.when(qi * tq < WINDOW)
    def _():
        rows = hpg * tq
        t = qi * tq + _mod_pow2(lax.broadcasted_iota(I32, (rows, span), 0), tq)
        wpos = lax.broadcasted_iota(I32, (rows, span), 1)
        attend(jnp.where((wpos <= t) & (wpos > t - WINDOW), 0.0, NEG_INF))


def _win_attn(q, kw, vw, glog, tq):
    bsz, t, wq = q.shape
    ngrp = kw.shape[1]
    hpg = wq // HEAD_DIM // ngrp
    gw = hpg * HEAD_DIM
    span = WINDOW + tq
    assert WINDOW % tq == 0 and t >= span
    rel = jnp.arange(span)[None, :] - WINDOW - jnp.arange(tq)[:, None]
    band = jnp.where((rel <= 0) & (rel > -WINDOW), 0.0, NEG_INF).astype(F32)
    return pl.pallas_call(
        functools.partial(_win_attn_kernel, tq=tq, hpg=hpg, span=span),
        out_shape=jax.ShapeDtypeStruct((bsz, t, wq), BF16),
        grid=(bsz, ngrp, t // tq),
        in_specs=[pl.BlockSpec((1, tq, gw), lambda b, g, i: (b, i, g)),
                  pl.BlockSpec((1, 1, t, HEAD_DIM), lambda b, g, i: (b, g, 0, 0)),
                  pl.BlockSpec((1, 1, t, LANES), lambda b, g, i: (b, g, 0, 0)),
                  pl.BlockSpec((1, tq, GATE_PAD), lambda b, g, i: (b, i, ZF_GATE)),
                  pl.BlockSpec((tq, span), lambda b, g, i: (0, 0))],
        out_specs=pl.BlockSpec((1, tq, gw), lambda b, g, i: (b, i, g)),
        compiler_params=_cparams(("arbitrary", "arbitrary", "arbitrary")),
        name="nsa_window",
    )(q, kw, vw, glog, band)


def _out_kernel(h_ref, yc_ref, ys_ref, yb_ref, oc_ref, os_ref, ow_ref, mod_ref, wglu_ref, w_ref, o_ref):
    g = _gelu_tanh(ys_ref[0])
    y_ssm = g * jax.nn.sigmoid(_dot(g.astype(BF16), wglu_ref[...]))
    nsa = oc_ref[0].astype(F32) + os_ref[0].astype(F32) + ow_ref[0].astype(F32)
    y = jnp.concatenate([yc_ref[0], y_ssm.astype(BF16), yb_ref[0], nsa.astype(BF16)], axis=1)
    o_ref[0] = h_ref[0] + mod_ref[0, 2:3, :] * _dot(y, w_ref[...])


def _out_proj(h, parts, mod, w_glu_bf16, w_bf16, tm):
    bsz, t, d = h.shape
    wp = parts[0].shape[-1]
    tok = lambda b, i: (b, i, 0)
    return pl.pallas_call(
        _out_kernel,
        out_shape=jax.ShapeDtypeStruct((bsz, t, d), F32),
        grid=(bsz, t // tm),
        in_specs=[pl.BlockSpec((1, tm, d), tok)] + [pl.BlockSpec((1, tm, wp), tok)] * 6
        + [pl.BlockSpec((1, 6, d), lambda b, i: (b, 0, 0)),
           pl.BlockSpec((wp, wp), lambda b, i: (0, 0)),
           pl.BlockSpec((d, d), lambda b, i: (0, 0))],
        out_specs=pl.BlockSpec((1, tm, d), tok),
        compiler_params=_cparams(("arbitrary", "arbitrary")),
        name="out_proj",
    )(h, *parts, mod, w_glu_bf16, w_bf16)


def _ffn_kernel(h_ref, g_ref, mod_ref, wg_ref, wu_ref, wd_ref, o_ref, a_s, acc):
    f = pl.program_id(2)

    @pl.when(f == 0)
    def _():
        x = h_ref[0]
        y = x * _rms_scale(x) * g_ref[...]
        a_s[...] = (y * (1.0 + mod_ref[0, 4:5, :]) + mod_ref[0, 3:4, :]).astype(BF16)
        acc[...] = jnp.zeros_like(acc)

    a = a_s[...]
    hid = _silu(_dot(a, wg_ref[...])) * _dot(a, wu_ref[...])
    acc[...] += _dot(hid.astype(BF16), wd_ref[...])

    @pl.when(f == pl.num_programs(2) - 1)
    def _():
        o_ref[0] = h_ref[0] + mod_ref[0, 5:6, :] * acc[...]


def _dense_ffn(h, g, mod, wg, wu, wd, tm, tf):
    bsz, t, d = h.shape
    f = wg.shape[1]
    tok = lambda b, i, j: (b, i, 0)
    return pl.pallas_call(
        _ffn_kernel,
        out_shape=jax.ShapeDtypeStruct((bsz, t, d), F32),
        grid=(bsz, t // tm, f // tf),
        in_specs=[pl.BlockSpec((1, tm, d), tok),
                  pl.BlockSpec((1, d), lambda b, i, j: (0, 0)),
                  pl.BlockSpec((1, 6, d), lambda b, i, j: (b, 0, 0)),
                  pl.BlockSpec((d, tf), lambda b, i, j: (0, j)),
                  pl.BlockSpec((d, tf), lambda b, i, j: (0, j)),
                  pl.BlockSpec((tf, d), lambda b, i, j: (j, 0))],
        out_specs=pl.BlockSpec((1, tm, d), tok),
        scratch_shapes=[pltpu.VMEM((tm, d), BF16), pltpu.VMEM((tm, d), F32)],
        compiler_params=_cparams(("arbitrary", "arbitrary", "arbitrary")),
        name="dense_swiglu",
    )(h, g[None, :], mod, wg, wu, wd)


MOE_CHUNK = 256


def _moe_route_kernel(h_ref, g_ref, mod_ref, wrt_ref, a_ref, gates_ref, rank_ref, cnt_ref, *, tm):
    x = h_ref[0]
    y = x * _rms_scale(x) * g_ref[...]
    a = y * (1.0 + mod_ref[0, 4:5, :]) + mod_ref[0, 3:4, :]
    a_ref[0] = a.astype(BF16)
    logits = _dot3_nt(wrt_ref[...], a)
    row = lax.broadcasted_iota(I32, logits.shape, 0)
    v1 = jnp.max(logits, axis=0, keepdims=True)
    i1 = jnp.min(jnp.where(logits == v1, row, N_EXPERTS), axis=0, keepdims=True)
    rest = jnp.where(row == i1, -jnp.inf, logits)
    v2 = jnp.max(rest, axis=0, keepdims=True)
    i2 = jnp.min(jnp.where(rest == v2, row, N_EXPERTS), axis=0, keepdims=True)
    e2 = jnp.exp(v2 - v1)
    gates_ref[0] = jnp.where(row == i1, 1.0 / (1.0 + e2), 0.0) + jnp.where(row == i2, e2 / (1.0 + e2), 0.0)
    routed = jnp.where(row == i1, 1.0, jnp.where(row == i2, 1.0, 0.0))
    si = lax.broadcasted_iota(I32, (MOE_CHUNK, MOE_CHUNK), 0)
    sj = lax.broadcasted_iota(I32, (MOE_CHUNK, MOE_CHUNK), 1)
    before = jnp.where(si < sj, 1.0, 0.0).astype(BF16)
    carry = jnp.zeros((N_EXPERTS, 1), F32)
    lane = lax.broadcasted_iota(I32, (N_EXPERTS, LANES), 1)
    starts = jnp.zeros((N_EXPERTS, LANES), F32)
    parts = []
    nsub = tm // MOE_CHUNK
    for c in range(nsub):
        rc = routed[:, c * MOE_CHUNK:(c + 1) * MOE_CHUNK]
        parts.append(_dot(rc.astype(BF16), before) + carry)
        starts = jnp.where(lane == c, carry, starts)
        carry = carry + jnp.sum(rc, axis=1, keepdims=True)
    rank = jnp.concatenate(parts, axis=1)
    rank_ref[0] = jnp.where(routed > 0.0, rank, -1.0)
    cnt_ref[0] = jnp.where(lane == nsub, carry, starts).astype(I32)


MOE_WINDOW = 5


def _moe_expert_kernel(cnt_ref, a_ref, rank_t_ref, rank_c_ref, gate_c_ref, wg_ref, wu_ref, wd_ref, o_ref,
                       xc, yacc, rc_s, gc_s, *, tm, ne):
    i = pl.program_id(0)
    e = pl.program_id(1)
    f = pl.program_id(2)
    nsub = tm // MOE_CHUNK
    win = min(MOE_WINDOW, nsub)
    half = MOE_CHUNK // 2
    base = (i * ne + e) * (nsub + 1)
    total = cnt_ref[base + nsub]
    rem = total % MOE_CHUNK
    nbig = total // MOE_CHUNK + (rem > half).astype(I32)
    has_tail = jnp.logical_and(rem > 0, rem <= half)
    tail0 = pl.multiple_of(nbig * MOE_CHUNK, MOE_CHUNK)
    d = o_ref.shape[-1]

    def big(fn):
        def body(c, carry):
            fn(pl.multiple_of(c * MOE_CHUNK, MOE_CHUNK), MOE_CHUNK)
            return carry

        lax.fori_loop(0, nbig, body, 0)

        @pl.when(has_tail)
        def _():
            fn(tail0, half)

    def window(r0, nrows):
        first = jnp.int32(0)
        stop = jnp.int32(0)
        for s in range(nsub):
            first += (cnt_ref[base + s + 1] <= r0).astype(I32)
            stop += (cnt_ref[base + s] < r0 + nrows).astype(I32)
        s0 = jnp.minimum(first, nsub - win)
        return s0, stop <= s0 + win

    def pick_rows(r0, nrows, s0, nsubs):
        tgt = (r0 + lax.broadcasted_iota(I32, (nrows, MOE_CHUNK), 0)).astype(F32)
        return jnp.concatenate(
            [jnp.where(rank_t_ref[0, e, pl.ds(s0 + j, 1), :] == tgt, 1.0, 0.0).astype(BF16)
             for j in range(nsubs)], axis=1)

    @pl.when(jnp.logical_and(e == 0, f == 0))
    def _():
        o_ref[...] = jnp.zeros_like(o_ref)

    @pl.when(f == 0)
    def _():
        def gather(r0, nrows):
            s0, fits = window(r0, nrows)

            @pl.when(fits)
            def _():
                t0 = pl.multiple_of(s0 * MOE_CHUNK, MOE_CHUNK)
                xc[pl.ds(r0, nrows), :] = _dot(pick_rows(r0, nrows, s0, win),
                                               a_ref[pl.ds(t0, win * MOE_CHUNK), :]).astype(BF16)

            @pl.when(jnp.logical_not(fits))
            def _():
                xc[pl.ds(r0, nrows), :] = _dot(pick_rows(r0, nrows, 0, nsub), a_ref[...]).astype(BF16)

            yacc[pl.ds(r0, nrows), :] = jnp.zeros((nrows, d), F32)

        big(gather)

    def ffn(r0, nrows):
        rows = [pl.ds(r0, half), pl.ds(r0 + half, half)] if nrows > half else [pl.ds(r0, nrows)]
        xs = [xc[r, :] for r in rows]
        gate_up = [(_dot(x, wg_ref[0]), _dot(x, wu_ref[0])) for x in xs]
        for r, (gt, up) in zip(rows, gate_up):
            yacc[r, :] += _dot((_silu(gt) * up).astype(BF16), wd_ref[0])

    big(ffn)

    @pl.when(f == pl.num_programs(2) - 1)
    def _():
        lane = lax.broadcasted_iota(I32, (tm, ne), 1)
        rc_s[...] = jnp.sum(jnp.where(lane == e, rank_c_ref[0], 0.0), axis=1, keepdims=True)
        gc_s[...] = jnp.sum(jnp.where(lane == e, gate_c_ref[0], 0.0), axis=1, keepdims=True)

        def put(r0, nrows, t0, ntok):
            rows = pl.ds(t0, ntok)
            tgt = (r0 + lax.broadcasted_iota(I32, (ntok, nrows), 1)).astype(F32)
            place = jnp.where(rc_s[rows, :] == tgt, 1.0, 0.0).astype(BF16)
            o_ref[rows, :] += gc_s[rows, :] * _dot(place, yacc[pl.ds(r0, nrows), :].astype(BF16))

        def scatter(r0, nrows):
            s0, fits = window(r0, nrows)

            @pl.when(fits)
            def _():
                put(r0, nrows, pl.multiple_of(s0 * MOE_CHUNK, MOE_CHUNK), win * MOE_CHUNK)

            @pl.when(jnp.logical_not(fits))
            def _():
                put(r0, nrows, 0, tm)

        big(scatter)


def _residual_kernel(h_ref, y_ref, mod_ref, o_ref):
    o_ref[0] = h_ref[0] + mod_ref[0, 5:6, :] * y_ref[0]


def _moe_ffn(h, g, mod, router, wg, wu, wd, tm, tf):
    bsz, t, d = h.shape
    ne, _, f = wg.shape
    assert ne == N_EXPERTS and tm % MOE_CHUNK == 0
    nt = t // tm
    tok = lambda b, i: (b, i, 0)
    a, gates_t, rank_t, cnt = pl.pallas_call(
        functools.partial(_moe_route_kernel, tm=tm),
        out_shape=(jax.ShapeDtypeStruct((bsz, t, d), BF16),
                   jax.ShapeDtypeStruct((bsz * nt, ne, tm), F32),
                   jax.ShapeDtypeStruct((bsz * nt, ne, tm), F32),
                   jax.ShapeDtypeStruct((bsz * nt, ne, LANES), I32)),
        grid=(bsz, nt),
        in_specs=[pl.BlockSpec((1, tm, d), tok),
                  pl.BlockSpec((1, d), lambda b, i: (0, 0)),
                  pl.BlockSpec((1, 6, d), lambda b, i: (b, 0, 0)),
                  pl.BlockSpec((ne, d), lambda b, i: (0, 0))],
        out_specs=(pl.BlockSpec((1, tm, d), tok),
                   pl.BlockSpec((1, ne, tm), lambda b, i: (b * nt + i, 0, 0)),
                   pl.BlockSpec((1, ne, tm), lambda b, i: (b * nt + i, 0, 0)),
                   pl.BlockSpec((1, ne, LANES), lambda b, i: (b * nt + i, 0, 0))),
        compiler_params=_cparams(("arbitrary", "arbitrary")),
        name="moe_route",
    )(h, g[None, :], mod, router.T)
    nsub = tm // MOE_CHUNK
    counts = cnt[:, :, :nsub + 1].reshape(-1)
    y = pl.pallas_call(
        functools.partial(_moe_expert_kernel, tm=tm, ne=ne),
        out_shape=jax.ShapeDtypeStruct((bsz * t, d), F32),
        grid_spec=pltpu.PrefetchScalarGridSpec(
            num_scalar_prefetch=1,
            grid=(bsz * nt, ne, f // tf),
            in_specs=[pl.BlockSpec((tm, d), lambda i, e, j, cnt: (i, 0)),
                      pl.BlockSpec((1, ne, nsub, MOE_CHUNK), lambda i, e, j, cnt: (i, 0, 0, 0)),
                      pl.BlockSpec((1, tm, ne), lambda i, e, j, cnt: (i, 0, 0)),
                      pl.BlockSpec((1, tm, ne), lambda i, e, j, cnt: (i, 0, 0)),
                      pl.BlockSpec((1, d, tf), lambda i, e, j, cnt: (e, 0, j)),
                      pl.BlockSpec((1, d, tf), lambda i, e, j, cnt: (e, 0, j)),
                      pl.BlockSpec((1, tf, d), lambda i, e, j, cnt: (e, j, 0))],
            out_specs=pl.BlockSpec((tm, d), lambda i, e, j, cnt: (i, 0)),
            scratch_shapes=[pltpu.VMEM((tm, d), BF16), pltpu.VMEM((tm, d), F32),
                            pltpu.VMEM((tm, 1), F32), pltpu.VMEM((tm, 1), F32)]),
        compiler_params=_cparams(("arbitrary", "arbitrary", "arbitrary")),
        name="moe_experts",
    )(counts, a.reshape(bsz * t, d), rank_t.reshape(bsz * nt, ne, nsub, MOE_CHUNK), rank_t.transpose(0, 2, 1),
      gates_t.transpose(0, 2, 1), wg, wu, wd)
    tr = _Tiles().proj
    return pl.pallas_call(
        _residual_kernel,
        out_shape=jax.ShapeDtypeStruct((bsz, t, d), F32),
        grid=(bsz, t // tr),
        in_specs=[pl.BlockSpec((1, tr, d), tok), pl.BlockSpec((1, tr, d), tok),
                  pl.BlockSpec((1, 6, d), lambda b, i: (b, 0, 0))],
        out_specs=pl.BlockSpec((1, tr, d), tok),
        compiler_params=_cparams(("arbitrary", "arbitrary")),
        name="moe_residual",
    )(h, y.reshape(bsz, t, d), mod)


class _Tiles(NamedTuple):
    proj: int = 512
    attn_q: int = 256
    sel_k: int = 512
    s5_rows: int = 256
    ffn_m: int = 1024
    moe_m: int = 2048
    ffn_f: int = 512


def _tiles(t):
    base = _Tiles()
    return base._replace(s5_rows=min(base.s5_rows, t // SSM_CHUNK), ffn_m=min(base.ffn_m, t),
                         moe_m=min(base.moe_m, t))


def _pack_w_in(w_in, dq):
    o1 = 3 * dq
    o2 = o1 + dq
    o3 = o2 + 3 * dq
    sb_q = w_in[:, o2:o2 + dq] * (HEAD_DIM ** -0.5)
    pad = jnp.zeros((w_in.shape[0], GATE_PAD - (w_in.shape[1] - (o3 + dq + 6 * NSA_KV_HEADS * HEAD_DIM))),
                    w_in.dtype)
    cols = [w_in[:, :o1], sb_q, w_in[:, o2 + dq:o3], w_in[:, o1:o2], w_in[:, o3:], pad]
    return jnp.concatenate(cols, axis=1).astype(BF16), o1 + 3 * dq


def _token_mixing(h, mod, positions, rope, g_mix, w_in, w_out, conv_w, ssm, nsa):
    bsz, t, d = h.shape
    dq = d // 4
    tl = _tiles(t)
    w_packed, nb = _pack_w_in(w_in, dq)
    zb, zs, zf = _in_proj(h, g_mix, mod, w_packed, nb, dq, tm=tl.proj)
    y_conv = _short_conv(zb, conv_w, tt=tl.proj)
    y_ssm = _s5(zs, *ssm[:-1], tt=tl.s5_rows)
    y_sb = _stick_breaking(zb, dq, tq=tl.attn_q)
    o_c, o_s, o_w = _nsa_mixer(zf, dq, positions, rope, nsa)
    return _out_proj(h, [y_conv, y_ssm, y_sb, o_c, o_s, o_w], mod, ssm[-1].astype(BF16), w_out.astype(BF16),
                     tm=tl.proj)


def _nsa_mixer(zf, wq, positions, rope, nsa):
    bsz, t, _ = zf.shape
    q_norm_g, k_norm_g, pos_k, pos_v, k_w1, k_w2, v_w1, v_w2 = nsa
    half = HEAD_DIM // 2
    inv_freq = jnp.power(jnp.float32(ROPE_THETA), -jnp.arange(half, dtype=F32) / half)
    tl = _tiles(t)
    q_rot, ck, cv, kaug, vaug, kwin, vwin = _nsa_prep(zf, rope, q_norm_g, k_norm_g, wq, tm=tl.proj)
    nch = t // CMP_STRIDE
    end_idx = jnp.minimum(jnp.arange(nch) * CMP_STRIDE + CMP_LEN - 1, t - 1)
    angc = jnp.tile(positions[:, end_idx].astype(F32)[..., None] * inv_freq, (1, 1, 2))
    kc, vc = _compress(ck, cv, k_w1, k_w2, v_w1, v_w2, pos_k, pos_v, k_norm_g, angc)
    o_c, sel = _cmp_attn(q_rot, kc, vc, zf, tq=tl.attn_q)
    o_s = _sel_attn(q_rot, sel, kaug, vaug, zf, tq=tl.attn_q, tk=tl.sel_k)
    o_w = _win_attn(q_rot, kwin, vwin, zf, tq=tl.attn_q)
    return o_c, o_s, o_w


def kernel(x, c, positions, ada_w, ada_b, norm_mix_g, norm_ffn_g, w_in, w_out, conv_w, ssm_lam_re, ssm_lam_im, ssm_b_re, ssm_b_im, ssm_c_re, ssm_c_im, ssm_d, ssm_log_dt, ssm_w_glu, nsa_q_norm_g, nsa_k_norm_g, cmp_pos_k, cmp_pos_v, cmp_k_w1, cmp_k_w2, cmp_v_w1, cmp_v_w2, ffn_w_gate, ffn_w_up, ffn_w_down, moe_router, moe_w_gate, moe_w_up, moe_w_down):
    depth = ada_w.shape[0]
    mods = _modulation(c, ada_w, ada_b)
    rope = _rope_tables(positions)
    h = x
    for layer in range(depth):
        mod = mods[layer]
        ssm = (ssm_lam_re[layer], ssm_lam_im[layer], ssm_b_re[layer], ssm_b_im[layer], ssm_c_re[layer],
               ssm_c_im[layer], ssm_d[layer], ssm_log_dt[layer], ssm_w_glu[layer])
        nsa = (nsa_q_norm_g[layer], nsa_k_norm_g[layer], cmp_pos_k[layer], cmp_pos_v[layer],
               cmp_k_w1[layer], cmp_k_w2[layer], cmp_v_w1[layer], cmp_v_w2[layer])
        h = _token_mixing(h, mod, positions, rope, norm_mix_g[layer], w_in[layer], w_out[layer],
                          conv_w[layer], ssm, nsa)
        i = layer // 2
        tl = _tiles(h.shape[1])
        if layer % 2 == 0:
            h = _dense_ffn(h, norm_ffn_g[layer], mod, ffn_w_gate[i].astype(BF16), ffn_w_up[i].astype(BF16),
                           ffn_w_down[i].astype(BF16), tm=tl.ffn_m, tf=tl.ffn_f)
        else:
            h = _moe_ffn(h, norm_ffn_g[layer], mod, moe_router[i], moe_w_gate[i].astype(BF16),
                         moe_w_up[i].astype(BF16), moe_w_down[i].astype(BF16), tm=tl.moe_m, tf=tl.ffn_f)
    return h
```

```python
import functools
import math
from typing import NamedTuple

import jax
import jax.numpy as jnp
from jax import lax
from jax.experimental import pallas as pl
from jax.experimental.pallas import tpu as pltpu

F32 = jnp.float32
BF16 = jnp.bfloat16
I32 = jnp.int32

HEAD_DIM = 64
CONV_WIDTH = 3
SSM_GROUP = 16
SSM_STATE = 64
SSM_MAX_RE = -1e-4
NSA_KV_HEADS = 2
CMP_LEN = 32
CMP_STRIDE = 16
SLC_BLOCK = 64
SLC_TOPK = 16
WINDOW = 512
FORCE_BONUS = 1e4
NEG_INF = -1e30
ROPE_THETA = 10000.0
RMS_EPS = 1e-6
N_EXPERTS = 8

LANES = 128
SUBLANES = 8
VMEM_LIMIT = 56 * 1024 * 1024
SEL_MASK_BIAS = -30000.0
SB_SKIP_LOG = -110.0
GATE_PAD = LANES


def _cparams(sem):
    return pltpu.CompilerParams(dimension_semantics=sem, vmem_limit_bytes=VMEM_LIMIT)


def _dot(a, b):
    return jnp.dot(a, b, preferred_element_type=F32)


def _dot_nt(a, b):
    return lax.dot_general(a, b, (((1,), (1,)), ((), ())), preferred_element_type=F32)


def _split(x):
    hi = x.astype(BF16)
    lo = (x - hi.astype(F32)).astype(BF16)
    return hi, lo


def _dot3(a, b):
    ah, al = _split(a)
    bh, bl = _split(b)
    return _dot(ah, bh) + _dot(ah, bl) + _dot(al, bh)


def _dot3_nt(a, b):
    ah, al = _split(a)
    bh, bl = _split(b)
    return _dot_nt(ah, bh) + _dot_nt(ah, bl) + _dot_nt(al, bh)


def _dot2_exact_rhs(a, b_bf16):
    ah, al = _split(a)
    return _dot(ah, b_bf16) + _dot(al, b_bf16)


def _silu(x):
    return x * jax.nn.sigmoid(x)


def _div_pow2(x, n):
    return lax.shift_right_logical(x, jnp.int32(n.bit_length() - 1))


def _mod_pow2(x, n):
    return x & (n - 1)


def _rms_scale(x):
    return lax.rsqrt(jnp.mean(x * x, axis=-1, keepdims=True) + RMS_EPS)


def _mod_kernel(c_ref, w_ref, b_ref, o_ref):
    o_ref[0] = _dot3(_silu(c_ref[...]), w_ref[0]) + b_ref[0]


def _modulation(c, ada_w, ada_b):
    depth, d, n6 = ada_w.shape
    bsz = c.shape[0]
    rows = -(-bsz // SUBLANES) * SUBLANES
    cpad = jnp.zeros((rows, d), F32).at[:bsz].set(c)
    tn = n6 // 4
    out = pl.pallas_call(
        _mod_kernel,
        out_shape=jax.ShapeDtypeStruct((depth, rows, n6), F32),
        grid=(depth, n6 // tn),
        in_specs=[pl.BlockSpec((rows, d), lambda l, j: (0, 0)),
                  pl.BlockSpec((1, d, tn), lambda l, j: (l, 0, j)),
                  pl.BlockSpec((1, 1, tn), lambda l, j: (l, 0, j))],
        out_specs=pl.BlockSpec((1, rows, tn), lambda l, j: (l, 0, j)),
        compiler_params=_cparams(("arbitrary", "arbitrary")),
        name="adaln_mod",
    )(cpad, ada_w, ada_b[:, None, :])
    return out[:, :bsz].reshape(depth, bsz, 6, d)


def _in_kernel(h_ref, g_ref, mod_ref, w_ref, zb_ref, zs_ref, zf_ref):
    x = h_ref[0]
    y = x * _rms_scale(x) * g_ref[...]
    a = y * (1.0 + mod_ref[0, 1:2, :]) + mod_ref[0, 0:1, :]
    z = _dot(a.astype(BF16), w_ref[...])
    nb = zb_ref.shape[-1]
    ns = zs_ref.shape[-1]
    zb_ref[0] = z[:, :nb].astype(BF16)
    zs_ref[0] = z[:, nb:nb + ns]
    zf_ref[0] = z[:, nb + ns:]


def _in_proj(h, g, mod, w_bf16, nb, ns, tm):
    bsz, t, d = h.shape
    nz = w_bf16.shape[1]
    widths = (nb, ns, nz - nb - ns)
    return pl.pallas_call(
        _in_kernel,
        out_shape=tuple(jax.ShapeDtypeStruct((bsz, t, w), dt) for w, dt in zip(widths, (BF16, F32, F32))),
        grid=(bsz, t // tm),
        in_specs=[pl.BlockSpec((1, tm, d), lambda b, i: (b, i, 0)),
                  pl.BlockSpec((1, d), lambda b, i: (0, 0)),
                  pl.BlockSpec((1, 6, d), lambda b, i: (b, 0, 0)),
                  pl.BlockSpec((d, nz), lambda b, i: (0, 0))],
        out_specs=tuple(pl.BlockSpec((1, tm, w), lambda b, i: (b, i, 0)) for w in widths),
        compiler_params=_cparams(("arbitrary", "arbitrary")),
        name="in_proj",
    )(h, g[None, :], mod, w_bf16)


def _conv_kernel(z_ref, w_ref, o_ref, buf, *, tt, dc):
    @pl.when(pl.program_id(1) == 0)
    def _():
        buf[0:SUBLANES, :] = jnp.zeros((SUBLANES, dc), F32)

    z = z_ref[0].astype(F32)
    gate_b = z[:, :dc]
    v = z[:, dc:2 * dc] * z[:, 2 * dc:]
    buf[SUBLANES:, :] = v
    y = (w_ref[2:3, :] * v
         + w_ref[1:2, :] * buf[SUBLANES - 1:SUBLANES - 1 + tt, :]
         + w_ref[0:1, :] * buf[SUBLANES - 2:SUBLANES - 2 + tt, :])
    o_ref[0] = (gate_b * y).astype(o_ref.dtype)
    buf[0:SUBLANES, :] = v[tt - SUBLANES:, :]


def _short_conv(zb, conv_w, tt):
    bsz, t, _ = zb.shape
    dc = conv_w.shape[1]
    c3 = 3 * dc
    return pl.pallas_call(
        functools.partial(_conv_kernel, tt=tt, dc=dc),
        out_shape=jax.ShapeDtypeStruct((bsz, t, dc), BF16),
        grid=(bsz, t // tt),
        in_specs=[pl.BlockSpec((1, tt, c3), lambda b, i: (b, i, 0)),
                  pl.BlockSpec((CONV_WIDTH, dc), lambda b, i: (0, 0))],
        out_specs=pl.BlockSpec((1, tt, dc), lambda b, i: (b, i, 0)),
        scratch_shapes=[pltpu.VMEM((tt + SUBLANES, dc), F32)],
        compiler_params=_cparams(("arbitrary", "arbitrary")),
        name="short_conv",
    )(zb, conv_w)


SSM_LANE_CHUNK = 2 * LANES


def _gelu_tanh(x):
    return 0.5 * x * (1.0 + jnp.tanh(math.sqrt(2.0 / math.pi) * (x + 0.044715 * (x * x * x))))


def _ssm_kernel(u_ref, tz_ref, we_ref, wd_ref, pre_ref, pim_ref, d_ref, o_ref,
                bre, bim, cin, cre, cim, *, tt, ns):
    @pl.when(pl.program_id(1) == 0)
    def _():
        cre[...] = jnp.zeros_like(cre)
        cim[...] = jnp.zeros_like(cim)

    u = u_ref[0]
    ub = u.astype(BF16)
    end = _dot(ub, we_ref[...])
    bre[...] = end[:, :ns]
    bim[...] = end[:, ns:]
    lw = SSM_LANE_CHUNK
    row = lax.broadcasted_iota(I32, (SUBLANES, lw), 0)

    def local_scan(i, carry):
        r0 = pl.multiple_of(i * SUBLANES, SUBLANES)
        for lg in range(ns // lw):
            sl = slice(lg * lw, (lg + 1) * lw)
            xr = bre[pl.ds(r0, SUBLANES), sl]
            xi = bim[pl.ds(r0, SUBLANES), sl]
            for k in (1, 2, 4):
                ar = pre_ref[k - 1:k, sl]
                ai = pim_ref[k - 1:k, sl]
                sr = jnp.where(row >= k, pltpu.roll(xr, k, 0), 0.0)
                si = jnp.where(row >= k, pltpu.roll(xi, k, 0), 0.0)
                xr, xi = xr + ar * sr - ai * si, xi + ar * si + ai * sr
            bre[pl.ds(r0, SUBLANES), sl] = xr
            bim[pl.ds(r0, SUBLANES), sl] = xi
        return carry

    lax.fori_loop(0, tt // SUBLANES, local_scan, 0, unroll=2)

    row_all = lax.broadcasted_iota(I32, (SUBLANES, ns), 0)

    def carry_in(i, carry):
        c_re, c_im = carry
        r0 = pl.multiple_of(i * SUBLANES, SUBLANES)
        pr = pre_ref[...]
        pi_ = pim_ref[...]
        xr = bre[pl.ds(r0, SUBLANES), :] + pr * c_re - pi_ * c_im
        xi = bim[pl.ds(r0, SUBLANES), :] + pr * c_im + pi_ * c_re
        cin[pl.ds(r0, SUBLANES), :ns] = jnp.where(row_all == 0, c_re, pltpu.roll(xr, 1, 0))
        cin[pl.ds(r0, SUBLANES), ns:] = jnp.where(row_all == 0, c_im, pltpu.roll(xi, 1, 0))
        return xr[SUBLANES - 1:], xi[SUBLANES - 1:]

    c_re, c_im = lax.fori_loop(0, tt // SUBLANES, carry_in, (cre[...], cim[...]))
    cre[...] = c_re
    cim[...] = c_im
    o_ref[0] = _dot(ub, tz_ref[...]) + _dot(cin[...].astype(BF16), wd_ref[...]) + d_ref[...] * u


SSM_CHUNK = 8


def _s5(u, lam_re, lam_im, b_re, b_im, c_re, c_im, d_skip, log_dt, tt):
    bsz, t, ds = u.shape
    g, p = lam_re.shape
    hc = SSM_GROUP
    ns = g * p
    nl = SSM_CHUNK
    lam = lax.complex(jnp.minimum(lam_re, SSM_MAX_RE), lam_im)
    dt = jnp.exp(log_dt)[:, None]
    lam_bar = jnp.exp(lam * dt)
    b_bar = ((lam_bar - 1.0) / lam)[..., None] * lax.complex(b_re, b_im)
    cmat = lax.complex(c_re, c_im)
    steps = jnp.arange(nl + 1, dtype=F32)[:, None, None]
    pw = jnp.exp(steps * (lam * dt)[None])
    in_grp = (jnp.arange(nl * ds) // hc) % g
    st_grp = jnp.arange(ns) // p
    rep = lambda n, m: (jnp.arange(m)[None, :] % n == jnp.arange(n)[:, None]).astype(F32)
    kern = jnp.einsum('gcp,tgp,gpd->tgdc', cmat, pw[:nl], b_bar).real.reshape(nl, ds, hc)
    zero = jnp.zeros((ds, hc), F32)
    rows_rc = jnp.concatenate(
        [jnp.concatenate([zero] * s + [kern[r - s] for r in range(s, nl)], axis=1) for s in range(nl)], axis=0)
    col_rc = (jnp.arange(nl * ds) // ds) * hc + jnp.arange(nl * ds) % hc
    spread_rc = (col_rc[None, :] == jnp.arange(nl * hc)[:, None]).astype(F32)
    tz = jnp.dot(rows_rc, spread_rc) * (in_grp[:, None] == in_grp[None, :])
    end = (pw[:nl][::-1][..., None] * b_bar[None]).transpose(0, 1, 3, 2).reshape(nl * ds, p)
    same_in_st = in_grp[:, None] == st_grp[None, :]
    we = jnp.concatenate([jnp.dot(part, rep(p, ns)) * same_in_st for part in (end.real, end.imag)], axis=1)
    dec = (cmat[None] * pw[1:, :, None, :]).transpose(3, 0, 1, 2).reshape(p, nl * ds)
    wd = jnp.concatenate([jnp.dot(rep(p, ns).T, part) * same_in_st.T for part in (dec.real, -dec.imag)], axis=0)
    pows = jnp.exp(jnp.arange(1, SUBLANES + 1, dtype=F32)[:, None, None] * nl * (lam * dt)[None])
    pows = pows.reshape(SUBLANES, ns)
    rows = t // nl
    const = lambda shape: pl.BlockSpec(shape, lambda b, i: (0, 0), pipeline_mode=pl.Buffered(1))
    y = pl.pallas_call(
        functools.partial(_ssm_kernel, tt=tt, ns=ns),
        out_shape=jax.ShapeDtypeStruct((bsz, rows, nl * ds), F32),
        grid=(bsz, rows // tt),
        in_specs=[pl.BlockSpec((1, tt, nl * ds), lambda b, i: (b, i, 0)),
                  const((nl * ds, nl * ds)), const((nl * ds, 2 * ns)), const((2 * ns, nl * ds)),
                  pl.BlockSpec((SUBLANES, ns), lambda b, i: (0, 0)),
                  pl.BlockSpec((SUBLANES, ns), lambda b, i: (0, 0)),
                  pl.BlockSpec((1, nl * ds), lambda b, i: (0, 0))],
        out_specs=pl.BlockSpec((1, tt, nl * ds), lambda b, i: (b, i, 0)),
        scratch_shapes=[pltpu.VMEM((tt, ns), F32), pltpu.VMEM((tt, ns), F32), pltpu.VMEM((tt, 2 * ns), F32),
                        pltpu.VMEM((1, ns), F32), pltpu.VMEM((1, ns), F32)],
        compiler_params=_cparams(("arbitrary", "arbitrary")),
        name="s5_scan",
    )(u.reshape(bsz, rows, nl * ds), tz.astype(BF16), we.astype(BF16), wd.astype(BF16), pows.real, pows.imag,
      jnp.tile(d_skip, nl)[None, :])
    return y.reshape(bsz, t, ds)


def _sb_kernel(q_ref, k_ref, v_ref, o_ref, acc, csum, *, tq, nh):
    qi = pl.program_id(1)
    acc[...] = jnp.zeros_like(acc)
    csum[...] = jnp.zeros_like(csum)
    ti = lax.broadcasted_iota(I32, (tq, tq), 0)
    ji = lax.broadcasted_iota(I32, (tq, tq), 1)
    later = (ti > ji).astype(BF16)

    def cond(state):
        kb, cmax = state
        return jnp.logical_and(kb >= 0, cmax > SB_SKIP_LOG)

    def block(kb, diagonal):
        k0 = pl.multiple_of(kb * tq, tq)
        past = ji < ti
        keep = (lambda x: jnp.where(past, x, 0.0)) if diagonal else (lambda x: x)
        heads = [slice(h * HEAD_DIM, (h + 1) * HEAD_DIM) for h in range(nh)]
        zs = [_dot_nt(q_ref[0, :, hs], k_ref[0, pl.ds(k0, tq), hs]) for hs in heads]
        log_betas, log_keeps, inners = [], [], []
        for z in zs:
            log_beta = jnp.minimum(z, 0.0) - jnp.log(1.0 + jnp.exp(-jnp.abs(z)))
            log_keep = keep(log_beta - z)
            hi, lo = _split(log_keep)
            log_betas.append(log_beta)
            log_keeps.append(log_keep)
            inners.append(_dot(hi, later) + _dot(lo, later))
        cmax = jnp.float32(-jnp.inf)
        for h, hs in enumerate(heads):
            c = csum[h]
            w = keep(jnp.exp(log_betas[h] + inners[h] + c[:, :1]))
            acc[:, hs] += _dot(w.astype(BF16), v_ref[0, pl.ds(k0, tq), hs])
            c = c + jnp.sum(log_keeps[h], axis=1, keepdims=True)
            csum[h] = c
            cmax = jnp.maximum(cmax, jnp.max(c))
        return cmax

    def body(state):
        kb, _ = state
        return kb - 1, block(kb, False)

    lax.while_loop(cond, body, (qi - 1, block(qi, True)))
    o_ref[0] = acc[...].astype(o_ref.dtype)


SB_Q_BLOCK, SB_K_BLOCK, SB_V_BLOCK = 3, 4, 5


def _stick_breaking(zb, dq, tq):
    bsz, t, _ = zb.shape
    nh = dq // HEAD_DIM
    return pl.pallas_call(
        functools.partial(_sb_kernel, tq=tq, nh=nh),
        out_shape=jax.ShapeDtypeStruct((bsz, t, dq), BF16),
        grid=(bsz, t // tq),
        in_specs=[pl.BlockSpec((1, tq, dq), lambda b, i: (b, i, SB_Q_BLOCK)),
                  pl.BlockSpec((1, t, dq), lambda b, i: (b, 0, SB_K_BLOCK)),
                  pl.BlockSpec((1, t, dq), lambda b, i: (b, 0, SB_V_BLOCK))],
        out_specs=pl.BlockSpec((1, tq, dq), lambda b, i: (b, i, 0)),
        scratch_shapes=[pltpu.VMEM((tq, dq), F32), pltpu.VMEM((nh, tq, LANES), F32)],
        compiler_params=_cparams(("arbitrary", "arbitrary")),
        name="stick_breaking",
    )(zb, zb, zb)


def _head_norm(x, gsum_ref, gain):
    ss = _dot2_exact_rhs(x * x, gsum_ref[...]) * (1.0 / HEAD_DIM)
    return x * lax.rsqrt(ss + RMS_EPS) * gain


def _rope_lanes(x, cos, sin):
    w = x.shape[-1]
    lane = lax.broadcasted_iota(I32, x.shape, 1)
    first = _mod_pow2(lane, HEAD_DIM) < (HEAD_DIM // 2)
    partner = jnp.where(first, -pltpu.roll(x, w - HEAD_DIM // 2, 1), pltpu.roll(x, HEAD_DIM // 2, 1))
    return x * cos + partner * sin


def _nsa_prep_kernel(zq_ref, zkc_ref, zvc_ref, zks_ref, zvs_ref, zkw_ref, zvw_ref, cos_ref, sin_ref, gq_ref,
                     gk_ref, gsq_ref, gsk_ref, q_ref, ck_ref, cv_ref, kaug_ref, vaug_ref, kw_ref, vw_ref,
                     *, tm, ngrp, nblk):
    cos1 = cos_ref[0]
    sin1 = sin_ref[0]
    wq = zq_ref.shape[-1]
    cosq = jnp.concatenate([cos1] * (wq // LANES), axis=1)
    sinq = jnp.concatenate([sin1] * (wq // LANES), axis=1)
    q = _rope_lanes(_head_norm(zq_ref[0], gsq_ref, gq_ref[...]), cosq, sinq)
    q_ref[0] = q * (HEAD_DIM ** -0.5)
    ks = _rope_lanes(_head_norm(zks_ref[0], gsk_ref, gk_ref[...]), cos1, sin1)
    kw = _rope_lanes(_head_norm(zkw_ref[0], gsk_ref, gk_ref[...]), cos1, sin1)
    t0 = pl.program_id(1) * tm
    tok = t0 + lax.broadcasted_iota(I32, (tm, nblk), 0)
    blk = lax.broadcasted_iota(I32, (tm, nblk), 1)
    onehot = jnp.where(_div_pow2(tok, SLC_BLOCK) == blk, 1.0, 0.0).astype(BF16)
    vs = zvs_ref[0]
    vw = zvw_ref[0]
    lane = lax.broadcasted_iota(I32, (tm, LANES - HEAD_DIM), 1)
    ones_col = jnp.where(lane == 0, 1.0, 0.0).astype(BF16)
    for g in range(ngrp):
        sl = slice(g * HEAD_DIM, (g + 1) * HEAD_DIM)
        kaug_ref[0, g] = jnp.concatenate([onehot, ks[:, sl].astype(BF16)], axis=1)
        vaug_ref[0, g] = jnp.concatenate([vs[:, sl].astype(BF16), ones_col], axis=1)
        kw_ref[0, g] = kw[:, sl].astype(BF16)
        vw_ref[0, g] = jnp.concatenate([vw[:, sl].astype(BF16), ones_col], axis=1)
    for j in range(CMP_STRIDE):
        rows = pl.ds(j, tm // CMP_STRIDE, stride=CMP_STRIDE)
        js = slice(j * HEAD_DIM, (j + 1) * HEAD_DIM)
        kj = zkc_ref[0, rows, :]
        vj = zvc_ref[0, rows, :]
        for g in range(ngrp):
            sl = slice(g * HEAD_DIM, (g + 1) * HEAD_DIM)
            ck_ref[0, g, :, js] = kj[:, sl]
            cv_ref[0, g, :, js] = vj[:, sl]


ZF_Q_BLOCK = 0
ZF_KCMP, ZF_VCMP, ZF_KSLC, ZF_VSLC, ZF_KWIN, ZF_VWIN, ZF_GATE = 2, 3, 4, 5, 6, 7, 8


def _rope_kernel(ang_ref, cos_ref, sin_ref):
    ang = ang_ref[...]
    cos_ref[...] = jnp.cos(ang)
    sin_ref[...] = jnp.sin(ang)


def _rope_tables(positions):
    bsz, t = positions.shape
    half = HEAD_DIM // 2
    inv_freq = jnp.power(jnp.float32(ROPE_THETA), -jnp.arange(half, dtype=F32) / half)
    ang = (positions.astype(F32)[..., None] * inv_freq).reshape(bsz, t * half // LANES, LANES)
    rows = ang.shape[1]
    spec = pl.BlockSpec((1, rows, LANES), lambda b: (b, 0, 0))
    cos, sin = pl.pallas_call(
        _rope_kernel,
        out_shape=(jax.ShapeDtypeStruct(ang.shape, F32), jax.ShapeDtypeStruct(ang.shape, F32)),
        grid=(bsz,),
        in_specs=[spec],
        out_specs=(spec, spec),
        compiler_params=_cparams(("arbitrary",)),
        name="rope_tables",
    )(ang)
    widen = lambda x: jnp.tile(x.reshape(bsz, t, half), (1, 1, LANES // half))
    return widen(cos), widen(sin)


def _nsa_prep(zf, rope, gq, gk, wq, tm):
    bsz, t, _ = zf.shape
    cos, sin = rope
    ngrp = NSA_KV_HEADS
    nblk = t // SLC_BLOCK
    wk = ngrp * HEAD_DIM
    assert wk == LANES and wq == 2 * wk
    lane_q = jnp.arange(wq) // HEAD_DIM
    gsq = (lane_q[:, None] == lane_q[None, :]).astype(BF16)
    lane_k = jnp.arange(wk) // HEAD_DIM
    gsk = (lane_k[:, None] == lane_k[None, :]).astype(BF16)
    kvspec = lambda blk: pl.BlockSpec((1, tm, wk), lambda b, i: (b, i, blk))
    hspec = lambda w: pl.BlockSpec((1, ngrp, tm, w), lambda b, i: (b, 0, i, 0))
    cspec = pl.BlockSpec((1, ngrp, tm // CMP_STRIDE, CMP_STRIDE * HEAD_DIM), lambda b, i: (b, 0, i, 0))
    return pl.pallas_call(
        functools.partial(_nsa_prep_kernel, tm=tm, ngrp=ngrp, nblk=nblk),
        out_shape=(jax.ShapeDtypeStruct((bsz, t, wq), F32),
                   jax.ShapeDtypeStruct((bsz, ngrp, t // CMP_STRIDE, CMP_STRIDE * HEAD_DIM), F32),
                   jax.ShapeDtypeStruct((bsz, ngrp, t // CMP_STRIDE, CMP_STRIDE * HEAD_DIM), F32),
                   jax.ShapeDtypeStruct((bsz, ngrp, t, HEAD_DIM + nblk), BF16),
                   jax.ShapeDtypeStruct((bsz, ngrp, t, LANES), BF16),
                   jax.ShapeDtypeStruct((bsz, ngrp, t, HEAD_DIM), BF16),
                   jax.ShapeDtypeStruct((bsz, ngrp, t, LANES), BF16)),
        grid=(bsz, t // tm),
        in_specs=[pl.BlockSpec((1, tm, wq), lambda b, i: (b, i, ZF_Q_BLOCK)),
                  kvspec(ZF_KCMP), kvspec(ZF_VCMP),
                  kvspec(ZF_KSLC), kvspec(ZF_VSLC), kvspec(ZF_KWIN), kvspec(ZF_VWIN),
                  pl.BlockSpec((1, tm, LANES), lambda b, i: (b, i, 0)),
                  pl.BlockSpec((1, tm, LANES), lambda b, i: (b, i, 0)),
                  pl.BlockSpec((1, wq), lambda b, i: (0, 0)),
                  pl.BlockSpec((1, wk), lambda b, i: (0, 0)),
                  pl.BlockSpec((wq, wq), lambda b, i: (0, 0)),
                  pl.BlockSpec((wk, wk), lambda b, i: (0, 0))],
        out_specs=(pl.BlockSpec((1, tm, wq), lambda b, i: (b, i, 0)),
                   cspec, cspec,
                   hspec(HEAD_DIM + nblk), hspec(LANES), hspec(HEAD_DIM), hspec(LANES)),
        compiler_params=_cparams(("arbitrary", "arbitrary")),
        name="nsa_prep",
    )(zf, zf, zf, zf, zf, zf, zf, cos, sin, jnp.tile(gq, wq // HEAD_DIM)[None, :], jnp.tile(gk, ngrp)[None, :],
      gsq, gsk)


def _compress_kernel(ck_ref, cv_ref, w1k_ref, w2k_ref, w1v_ref, w2v_ref, pek_ref, pev_ref,
                     gk_ref, angc_ref, kc_ref, vc_ref, *, nch, half):
    def mlp(c_ref, w1_ref, w2_ref, pe_ref):
        x = c_ref[0, 0]
        w1 = w1_ref[...]
        first = _dot3(x, w1[:half])
        second = _dot3(x, w1[half:])
        bias = _dot3(pe_ref[...], w1)[0:1]
        hid = _silu(first + pltpu.roll(second, nch - 1, 0) + bias)
        return _dot3(hid, w2_ref[...])

    kc = mlp(ck_ref, w1k_ref, w2k_ref, pek_ref)
    kc = kc * _rms_scale(kc) * gk_ref[...]
    ang = angc_ref[0]
    half_d = HEAD_DIM // 2
    k1 = kc[:, :half_d]
    k2 = kc[:, half_d:]
    cos = jnp.cos(ang[:, :half_d])
    sin = jnp.sin(ang[:, :half_d])
    kc_ref[0, 0] = jnp.concatenate([k1 * cos - k2 * sin, k2 * cos + k1 * sin], axis=1)
    vc_ref[0, 0] = mlp(cv_ref, w1v_ref, w2v_ref, pev_ref)


def _compress(ck, cv, w1k, w2k, w1v, w2v, pos_k, pos_v, gk, angc):
    bsz, ngrp, nch, half = ck.shape
    hid = w1k.shape[1]

    def pe_rows(pe):
        return jnp.zeros((SUBLANES, 2 * half), F32).at[0].set(pe.reshape(-1))

    blk4 = lambda b, g: (b, g, 0, 0)
    full2 = lambda b, g: (0, 0)
    return pl.pallas_call(
        functools.partial(_compress_kernel, nch=nch, half=half),
        out_shape=(jax.ShapeDtypeStruct((bsz, ngrp, nch, HEAD_DIM), F32),
                   jax.ShapeDtypeStruct((bsz, ngrp, nch, HEAD_DIM), F32)),
        grid=(bsz, ngrp),
        in_specs=[pl.BlockSpec((1, 1, nch, half), blk4),
                  pl.BlockSpec((1, 1, nch, half), blk4),
                  pl.BlockSpec((2 * half, hid), full2),
                  pl.BlockSpec((hid, HEAD_DIM), full2),
                  pl.BlockSpec((2 * half, hid), full2),
                  pl.BlockSpec((hid, HEAD_DIM), full2),
                  pl.BlockSpec((SUBLANES, 2 * half), full2),
                  pl.BlockSpec((SUBLANES, 2 * half), full2),
                  pl.BlockSpec((1, HEAD_DIM), full2),
                  pl.BlockSpec((1, nch, HEAD_DIM), lambda b, g: (b, 0, 0))],
        out_specs=(pl.BlockSpec((1, 1, nch, HEAD_DIM), blk4),
                   pl.BlockSpec((1, 1, nch, HEAD_DIM), blk4)),
        compiler_params=_cparams(("arbitrary", "arbitrary")),
        name="nsa_compress",
    )(ck, cv, w1k, w2k, w1v, w2v, pe_rows(pos_k), pe_rows(pos_v), gk[None, :], angc)


def _gate_col(gates, head, branch):
    c = head * 3 + branch
    if isinstance(c, int):
        return gates[:, c:c + 1]
    lane = lax.broadcasted_iota(I32, gates.shape, 1)
    return jnp.sum(jnp.where(lane == c, gates, 0.0), axis=1, keepdims=True)


def _cmp_attn_kernel(q_ref, kc_ref, vct_ref, gl_ref, ovt_ref, oc_ref, sel_ref, *, tq, ngrp, hpg, nch, nblk):
    t = pl.program_id(1) * tq + lax.broadcasted_iota(I32, (1, tq), 1)
    cmp_end = lax.broadcasted_iota(I32, (nch, 1), 0) * CMP_STRIDE + (CMP_LEN - 1)
    visible = cmp_end <= t
    q = q_ref[0]
    gates = jax.nn.sigmoid(gl_ref[0])
    blk = lax.broadcasted_iota(I32, (nblk, tq), 0)
    cur = _div_pow2(t, SLC_BLOCK)
    forced = (blk == 0) | (blk == cur) | (blk == cur - 1)
    causal = blk <= cur
    lane = lax.broadcasted_iota(I32, (tq, hpg * HEAD_DIM), 1)
    outs = []
    for g in range(ngrp):
        kc = kc_ref[0, g]
        vct = vct_ref[0, g]
        psum = jnp.zeros((nch, tq), F32)
        o_t = []
        for hh in range(hpg):
            head = g * hpg + hh
            qh = q[:, head * HEAD_DIM:(head + 1) * HEAD_DIM]
            s = jnp.where(visible, _dot3_nt(kc, qh), NEG_INF)
            p = jnp.exp(s - jnp.maximum(jnp.max(s, axis=0, keepdims=True), 0.1 * NEG_INF))
            denom = jnp.sum(p, axis=0, keepdims=True)
            p = p * (1.0 / jnp.where(denom == 0.0, 1.0, denom))
            psum = psum + p
            o_t.append(_dot(vct, p.astype(BF16)))
        gate = _gate_col(gates, g * hpg, 0)
        for hh in range(1, hpg):
            gate = jnp.where(lane < hh * HEAD_DIM, gate, _gate_col(gates, g * hpg + hh, 0))
        outs.append(jnp.concatenate(o_t, axis=0).T * gate)
        hi, lo = _split(psum)
        imp = _dot(ovt_ref[...], hi) + _dot(ovt_ref[...], lo)
        val = jnp.where(causal, imp + jnp.where(forced, FORCE_BONUS, 0.0), NEG_INF)
        for _ in range(min(SLC_TOPK, nblk)):
            m = jnp.max(val, axis=0, keepdims=True)
            first = jnp.min(jnp.where(val == m, blk, nblk), axis=0, keepdims=True)
            val = jnp.where(blk == first, -jnp.inf, val)
        sel_ref[0, g] = jnp.where(val == -jnp.inf, 0.0, SEL_MASK_BIAS).T.astype(BF16)
    oc_ref[0] = jnp.concatenate(outs, axis=1).astype(oc_ref.dtype)


def _cmp_attn(q, kc, vc, glog, tq):
    bsz, t, wq = q.shape
    ngrp, nch = kc.shape[1], kc.shape[2]
    hpg = wq // HEAD_DIM // ngrp
    nblk = t // SLC_BLOCK
    cs = jnp.arange(nch) * CMP_STRIDE
    ss = jnp.arange(nblk) * SLC_BLOCK
    ovt = ((cs[None, :] < ss[:, None] + SLC_BLOCK) & (cs[None, :] + CMP_LEN > ss[:, None])).astype(BF16)
    return pl.pallas_call(
        functools.partial(_cmp_attn_kernel, tq=tq, ngrp=ngrp, hpg=hpg, nch=nch, nblk=nblk),
        out_shape=(jax.ShapeDtypeStruct((bsz, t, wq), BF16),
                   jax.ShapeDtypeStruct((bsz, ngrp, t, nblk), BF16)),
        grid=(bsz, t // tq),
        in_specs=[pl.BlockSpec((1, tq, wq), lambda b, i: (b, i, 0)),
                  pl.BlockSpec((1, ngrp, nch, HEAD_DIM), lambda b, i: (b, 0, 0, 0)),
                  pl.BlockSpec((1, ngrp, HEAD_DIM, nch), lambda b, i: (b, 0, 0, 0)),
                  pl.BlockSpec((1, tq, GATE_PAD), lambda b, i: (b, i, ZF_GATE)),
                  pl.BlockSpec((nblk, nch), lambda b, i: (0, 0))],
        out_specs=(pl.BlockSpec((1, tq, wq), lambda b, i: (b, i, 0)),
                   pl.BlockSpec((1, ngrp, tq, nblk), lambda b, i: (b, 0, i, 0))),
        compiler_params=_cparams(("arbitrary", "arbitrary")),
        name="nsa_cmp_select",
    )(q, kc, vc.transpose(0, 1, 3, 2).astype(BF16), glog, ovt)


SEL_ROW_CHUNK = 512
WIN_ROW_CHUNK = 128


def _sel_attn_kernel(q_ref, sel_ref, kaug_ref, vaug_ref, gl_ref, o_ref, qa_s, m_s, acc, s_a, s_b, *, tq, tk, hpg):
    g = pl.program_id(1)
    qi = pl.program_id(2)
    q = q_ref[0]
    sel = sel_ref[0, 0]
    for hh in range(hpg):
        qa_s[hh * tq:(hh + 1) * tq, :] = jnp.concatenate(
            [sel, q[:, hh * HEAD_DIM:(hh + 1) * HEAD_DIM].astype(BF16)], axis=1)
    m_s[...] = jnp.full_like(m_s, NEG_INF)
    acc[...] = jnp.zeros_like(acc)
    rows = hpg * tq
    rc = SEL_ROW_CHUNK

    def scores(kt, buf):
        k0 = pl.multiple_of(kt * tk, tk)
        kblk = kaug_ref[0, 0, pl.ds(k0, tk), :]
        for c in range(rows // rc):
            rs = slice(c * rc, (c + 1) * rc)
            buf[rs, :] = _dot_nt(qa_s[rs, :], kblk)

    def consume(kt, buf, diag):
        k0 = pl.multiple_of(kt * tk, tk)
        vblk = vaug_ref[0, 0, pl.ds(k0, tk), :]
        for c in range(rows // rc):
            rs = slice(c * rc, (c + 1) * rc)
            s = buf[rs, :]
            if diag:
                t = qi * tq + _mod_pow2(c * rc + lax.broadcasted_iota(I32, (rc, tk), 0), tq)
                s = jnp.where(k0 + lax.broadcasted_iota(I32, (rc, tk), 1) <= t, s, NEG_INF)
            m_old = m_s[rs, :]
            m_new = jnp.maximum(m_old, jnp.max(s, axis=1, keepdims=True))
            alpha = jnp.exp(m_old - m_new)
            p = jnp.exp(s - m_new[:, :1])
            acc[rs, :] = alpha * acc[rs, :] + _dot(p.astype(BF16), vblk)
            m_s[rs, :] = m_new

    last = (qi * tq) // tk
    scores(0, s_a)

    def two_tiles(k):
        scores(k + 1, s_b)
        consume(k, s_a, False)
        scores(k + 2, s_a)
        consume(k + 1, s_b, False)

    def quad(j, carry):
        two_tiles(4 * j)
        two_tiles(4 * j + 2)
        return carry

    def pair(j, carry):
        two_tiles(4 * (last // 4) + 2 * j)
        return carry

    lax.fori_loop(0, last // 4, quad, 0)
    lax.fori_loop(0, (last % 4) // 2, pair, 0)
    t0 = (last // 2) * 2

    @pl.when(t0 < last)
    def _():
        scores(last, s_b)
        consume(t0, s_a, False)
        consume(last, s_b, True)

    @pl.when(t0 == last)
    def _():
        consume(last, s_a, True)

    gates = jax.nn.sigmoid(gl_ref[0])
    a = acc[...]
    out = a[:, :HEAD_DIM] / a[:, HEAD_DIM:HEAD_DIM + 1]
    o_ref[0] = jnp.concatenate(
        [out[hh * tq:(hh + 1) * tq] * _gate_col(gates, g * hpg + hh, 1) for hh in range(hpg)],
        axis=1).astype(o_ref.dtype)


def _sel_attn(q, sel, kaug, vaug, glog, tq, tk):
    bsz, t, wq = q.shape
    ngrp = kaug.shape[1]
    hpg = wq // HEAD_DIM // ngrp
    gw = hpg * HEAD_DIM
    nblk = sel.shape[-1]
    ka = kaug.shape[-1]
    rows = hpg * tq
    assert tk % tq == 0 and rows % SEL_ROW_CHUNK == 0
    return pl.pallas_call(
        functools.partial(_sel_attn_kernel, tq=tq, tk=tk, hpg=hpg),
        out_shape=jax.ShapeDtypeStruct((bsz, t, wq), BF16),
        grid=(bsz, ngrp, t // tq),
        in_specs=[pl.BlockSpec((1, tq, gw), lambda b, g, i: (b, i, g)),
                  pl.BlockSpec((1, 1, tq, nblk), lambda b, g, i: (b, g, i, 0)),
                  pl.BlockSpec((1, 1, t, ka), lambda b, g, i: (b, g, 0, 0)),
                  pl.BlockSpec((1, 1, t, LANES), lambda b, g, i: (b, g, 0, 0)),
                  pl.BlockSpec((1, tq, GATE_PAD), lambda b, g, i: (b, i, ZF_GATE))],
        out_specs=pl.BlockSpec((1, tq, gw), lambda b, g, i: (b, i, g)),
        scratch_shapes=[pltpu.VMEM((rows, ka), BF16), pltpu.VMEM((rows, LANES), F32),
                        pltpu.VMEM((rows, LANES), F32),
                        pltpu.VMEM((rows, tk), F32), pltpu.VMEM((rows, tk), F32)],
        compiler_params=_cparams(("arbitrary", "arbitrary", "arbitrary")),
        name="nsa_selected",
    )(q, sel, kaug, vaug, glog)


def _win_attn_kernel(q_ref, k_ref, v_ref, gl_ref, bias_ref, o_ref, *, tq, hpg, span):
    g = pl.program_id(1)
    qi = pl.program_id(2)
    q = q_ref[0]
    q2 = jnp.concatenate([q[:, hh * HEAD_DIM:(hh + 1) * HEAD_DIM] for hh in range(hpg)], axis=0).astype(BF16)
    base = pl.multiple_of(jnp.maximum(qi * tq - WINDOW, 0), tq)
    kblk = k_ref[0, 0, pl.ds(base, span), :]
    vblk = v_ref[0, 0, pl.ds(base, span), :]
    gates = jax.nn.sigmoid(gl_ref[0])

    def attend(bias):
        rc = WIN_ROW_CHUNK
        chunks = [slice(c * rc, (c + 1) * rc) for c in range(hpg * tq // rc)]
        scores = [_dot_nt(q2[rs], kblk) + bias[rs] for rs in chunks]
        outs = []
        for s in scores:
            p = jnp.exp(s - jnp.max(s, axis=1, keepdims=True))
            acc = _dot(p.astype(BF16), vblk)
            outs.append(acc[:, :HEAD_DIM] / acc[:, HEAD_DIM:HEAD_DIM + 1])
        out = jnp.concatenate(outs, axis=0)
        o_ref[0] = jnp.concatenate(
            [out[hh * tq:(hh + 1) * tq] * _gate_col(gates, g * hpg + hh, 2) for hh in range(hpg)],
            axis=1).astype(o_ref.dtype)

    @pl.when(qi * tq >= WINDOW)
    def _():
        attend(jnp.concatenate([bias_ref[...]] * hpg, axis=0))

    @pl.when(qi * tq < WINDOW)
    def _():
        rows = hpg * tq
        t = qi * tq + _mod_pow2(lax.broadcasted_iota(I32, (rows, span), 0), tq)
        wpos = lax.broadcasted_iota(I32, (rows, span), 1)
        attend(jnp.where((wpos <= t) & (wpos > t - WINDOW), 0.0, NEG_INF))


def _win_attn(q, kw, vw, glog, tq):
    bsz, t, wq = q.shape
    ngrp = kw.shape[1]
    hpg = wq // HEAD_DIM // ngrp
    gw = hpg * HEAD_DIM
    span = WINDOW + tq
    assert WINDOW % tq == 0 and t >= span
    rel = jnp.arange(span)[None, :] - WINDOW - jnp.arange(tq)[:, None]
    band = jnp.where((rel <= 0) & (rel > -WINDOW), 0.0, NEG_INF).astype(F32)
    return pl.pallas_call(
        functools.partial(_win_attn_kernel, tq=tq, hpg=hpg, span=span),
        out_shape=jax.ShapeDtypeStruct((bsz, t, wq), BF16),
        grid=(bsz, ngrp, t // tq),
        in_specs=[pl.BlockSpec((1, tq, gw), lambda b, g, i: (b, i, g)),
                  pl.BlockSpec((1, 1, t, HEAD_DIM), lambda b, g, i: (b, g, 0, 0)),
                  pl.BlockSpec((1, 1, t, LANES), lambda b, g, i: (b, g, 0, 0)),
                  pl.BlockSpec((1, tq, GATE_PAD), lambda b, g, i: (b, i, ZF_GATE)),
                  pl.BlockSpec((tq, span), lambda b, g, i: (0, 0))],
        out_specs=pl.BlockSpec((1, tq, gw), lambda b, g, i: (b, i, g)),
        compiler_params=_cparams(("arbitrary", "arbitrary", "arbitrary")),
        name="nsa_window",
    )(q, kw, vw, glog, band)


def _out_kernel(h_ref, yc_ref, ys_ref, yb_ref, oc_ref, os_ref, ow_ref, mod_ref, wglu_ref, w_ref, o_ref):
    g = _gelu_tanh(ys_ref[0])
    y_ssm = g * jax.nn.sigmoid(_dot(g.astype(BF16), wglu_ref[...]))
    nsa = oc_ref[0].astype(F32) + os_ref[0].astype(F32) + ow_ref[0].astype(F32)
    y = jnp.concatenate([yc_ref[0], y_ssm.astype(BF16), yb_ref[0], nsa.astype(BF16)], axis=1)
    o_ref[0] = h_ref[0] + mod_ref[0, 2:3, :] * _dot(y, w_ref[...])


def _out_proj(h, parts, mod, w_glu_bf16, w_bf16, tm):
    bsz, t, d = h.shape
    wp = parts[0].shape[-1]
    tok = lambda b, i: (b, i, 0)
    return pl.pallas_call(
        _out_kernel,
        out_shape=jax.ShapeDtypeStruct((bsz, t, d), F32),
        grid=(bsz, t // tm),
        in_specs=[pl.BlockSpec((1, tm, d), tok)] + [pl.BlockSpec((1, tm, wp), tok)] * 6
        + [pl.BlockSpec((1, 6, d), lambda b, i: (b, 0, 0)),
           pl.BlockSpec((wp, wp), lambda b, i: (0, 0)),
           pl.BlockSpec((d, d), lambda b, i: (0, 0))],
        out_specs=pl.BlockSpec((1, tm, d), tok),
        compiler_params=_cparams(("arbitrary", "arbitrary")),
        name="out_proj",
    )(h, *parts, mod, w_glu_bf16, w_bf16)


def _ffn_kernel(h_ref, g_ref, mod_ref, wg_ref, wu_ref, wd_ref, o_ref, a_s, acc):
    f = pl.program_id(2)

    @pl.when(f == 0)
    def _():
        x = h_ref[0]
        y = x * _rms_scale(x) * g_ref[...]
        a_s[...] = (y * (1.0 + mod_ref[0, 4:5, :]) + mod_ref[0, 3:4, :]).astype(BF16)
        acc[...] = jnp.zeros_like(acc)

    a = a_s[...]
    hid = _silu(_dot(a, wg_ref[...])) * _dot(a, wu_ref[...])
    acc[...] += _dot(hid.astype(BF16), wd_ref[...])

    @pl.when(f == pl.num_programs(2) - 1)
    def _():
        o_ref[0] = h_ref[0] + mod_ref[0, 5:6, :] * acc[...]


def _dense_ffn(h, g, mod, wg, wu, wd, tm, tf):
    bsz, t, d = h.shape
    f = wg.shape[1]
    tok = lambda b, i, j: (b, i, 0)
    return pl.pallas_call(
        _ffn_kernel,
        out_shape=jax.ShapeDtypeStruct((bsz, t, d), F32),
        grid=(bsz, t // tm, f // tf),
        in_specs=[pl.BlockSpec((1, tm, d), tok),
                  pl.BlockSpec((1, d), lambda b, i, j: (0, 0)),
                  pl.BlockSpec((1, 6, d), lambda b, i, j: (b, 0, 0)),
                  pl.BlockSpec((d, tf), lambda b, i, j: (0, j)),
                  pl.BlockSpec((d, tf), lambda b, i, j: (0, j)),
                  pl.BlockSpec((tf, d), lambda b, i, j: (j, 0))],
        out_specs=pl.BlockSpec((1, tm, d), tok),
        scratch_shapes=[pltpu.VMEM((tm, d), BF16), pltpu.VMEM((tm, d), F32)],
        compiler_params=_cparams(("arbitrary", "arbitrary", "arbitrary")),
        name="dense_swiglu",
    )(h, g[None, :], mod, wg, wu, wd)


MOE_CHUNK = 256


def _moe_route_kernel(h_ref, g_ref, mod_ref, wrt_ref, a_ref, gates_ref, rank_ref, cnt_ref, *, tm):
    x = h_ref[0]
    y = x * _rms_scale(x) * g_ref[...]
    a = y * (1.0 + mod_ref[0, 4:5, :]) + mod_ref[0, 3:4, :]
    a_ref[0] = a.astype(BF16)
    logits = _dot3_nt(wrt_ref[...], a)
    row = lax.broadcasted_iota(I32, logits.shape, 0)
    v1 = jnp.max(logits, axis=0, keepdims=True)
    i1 = jnp.min(jnp.where(logits == v1, row, N_EXPERTS), axis=0, keepdims=True)
    rest = jnp.where(row == i1, -jnp.inf, logits)
    v2 = jnp.max(rest, axis=0, keepdims=True)
    i2 = jnp.min(jnp.where(rest == v2, row, N_EXPERTS), axis=0, keepdims=True)
    e2 = jnp.exp(v2 - v1)
    gates_ref[0] = jnp.where(row == i1, 1.0 / (1.0 + e2), 0.0) + jnp.where(row == i2, e2 / (1.0 + e2), 0.0)
    routed = jnp.where(row == i1, 1.0, jnp.where(row == i2, 1.0, 0.0))
    si = lax.broadcasted_iota(I32, (MOE_CHUNK, MOE_CHUNK), 0)
    sj = lax.broadcasted_iota(I32, (MOE_CHUNK, MOE_CHUNK), 1)
    before = jnp.where(si < sj, 1.0, 0.0).astype(BF16)
    carry = jnp.zeros((N_EXPERTS, 1), F32)
    lane = lax.broadcasted_iota(I32, (N_EXPERTS, LANES), 1)
    starts = jnp.zeros((N_EXPERTS, LANES), F32)
    parts = []
    nsub = tm // MOE_CHUNK
    for c in range(nsub):
        rc = routed[:, c * MOE_CHUNK:(c + 1) * MOE_CHUNK]
        parts.append(_dot(rc.astype(BF16), before) + carry)
        starts = jnp.where(lane == c, carry, starts)
        carry = carry + jnp.sum(rc, axis=1, keepdims=True)
    rank = jnp.concatenate(parts, axis=1)
    rank_ref[0] = jnp.where(routed > 0.0, rank, -1.0)
    cnt_ref[0] = jnp.where(lane == nsub, carry, starts).astype(I32)


MOE_WINDOW = 5


def _moe_expert_kernel(cnt_ref, a_ref, rank_t_ref, rank_c_ref, gate_c_ref, wg_ref, wu_ref, wd_ref, o_ref,
                       xc, yacc, rc_s, gc_s, *, tm, ne):
    i = pl.program_id(0)
    e = pl.program_id(1)
    f = pl.program_id(2)
    nsub = tm // MOE_CHUNK
    win = min(MOE_WINDOW, nsub)
    half = MOE_CHUNK // 2
    base = (i * ne + e) * (nsub + 1)
    total = cnt_ref[base + nsub]
    rem = total % MOE_CHUNK
    nbig = total // MOE_CHUNK + (rem > half).astype(I32)
    has_tail = jnp.logical_and(rem > 0, rem <= half)
    tail0 = pl.multiple_of(nbig * MOE_CHUNK, MOE_CHUNK)
    d = o_ref.shape[-1]

    def big(fn):
        def body(c, carry):
            fn(pl.multiple_of(c * MOE_CHUNK, MOE_CHUNK), MOE_CHUNK)
            return carry

        lax.fori_loop(0, nbig, body, 0)

        @pl.when(has_tail)
        def _():
            fn(tail0, half)

    def window(r0, nrows):
        first = jnp.int32(0)
        stop = jnp.int32(0)
        for s in range(nsub):
            first += (cnt_ref[base + s + 1] <= r0).astype(I32)
            stop += (cnt_ref[base + s] < r0 + nrows).astype(I32)
        s0 = jnp.minimum(first, nsub - win)
        return s0, stop <= s0 + win

    def pick_rows(r0, nrows, s0, nsubs):
        tgt = (r0 + lax.broadcasted_iota(I32, (nrows, MOE_CHUNK), 0)).astype(F32)
        return jnp.concatenate(
            [jnp.where(rank_t_ref[0, e, pl.ds(s0 + j, 1), :] == tgt, 1.0, 0.0).astype(BF16)
             for j in range(nsubs)], axis=1)

    @pl.when(jnp.logical_and(e == 0, f == 0))
    def _():
        o_ref[...] = jnp.zeros_like(o_ref)

    @pl.when(f == 0)
    def _():
        def gather(r0, nrows):
            s0, fits = window(r0, nrows)

            @pl.when(fits)
            def _():
                t0 = pl.multiple_of(s0 * MOE_CHUNK, MOE_CHUNK)
                xc[pl.ds(r0, nrows), :] = _dot(pick_rows(r0, nrows, s0, win),
                                               a_ref[pl.ds(t0, win * MOE_CHUNK), :]).astype(BF16)

            @pl.when(jnp.logical_not(fits))
            def _():
                xc[pl.ds(r0, nrows), :] = _dot(pick_rows(r0, nrows, 0, nsub), a_ref[...]).astype(BF16)

            yacc[pl.ds(r0, nrows), :] = jnp.zeros((nrows, d), F32)

        big(gather)

    def ffn(r0, nrows):
        rows = [pl.ds(r0, half), pl.ds(r0 + half, half)] if nrows > half else [pl.ds(r0, nrows)]
        xs = [xc[r, :] for r in rows]
        gate_up = [(_dot(x, wg_ref[0]), _dot(x, wu_ref[0])) for x in xs]
        for r, (gt, up) in zip(rows, gate_up):
            yacc[r, :] += _dot((_silu(gt) * up).astype(BF16), wd_ref[0])

    big(ffn)

    @pl.when(f == pl.num_programs(2) - 1)
    def _():
        lane = lax.broadcasted_iota(I32, (tm, ne), 1)
        rc_s[...] = jnp.sum(jnp.where(lane == e, rank_c_ref[0], 0.0), axis=1, keepdims=True)
        gc_s[...] = jnp.sum(jnp.where(lane == e, gate_c_ref[0], 0.0), axis=1, keepdims=True)

        def put(r0, nrows, t0, ntok):
            rows = pl.ds(t0, ntok)
            tgt = (r0 + lax.broadcasted_iota(I32, (ntok, nrows), 1)).astype(F32)
            place = jnp.where(rc_s[rows, :] == tgt, 1.0, 0.0).astype(BF16)
            o_ref[rows, :] += gc_s[rows, :] * _dot(place, yacc[pl.ds(r0, nrows), :].astype(BF16))

        def scatter(r0, nrows):
            s0, fits = window(r0, nrows)

            @pl.when(fits)
            def _():
                put(r0, nrows, pl.multiple_of(s0 * MOE_CHUNK, MOE_CHUNK), win * MOE_CHUNK)

            @pl.when(jnp.logical_not(fits))
            def _():
                put(r0, nrows, 0, tm)

        big(scatter)


def _residual_kernel(h_ref, y_ref, mod_ref, o_ref):
    o_ref[0] = h_ref[0] + mod_ref[0, 5:6, :] * y_ref[0]


def _moe_ffn(h, g, mod, router, wg, wu, wd, tm, tf):
    bsz, t, d = h.shape
    ne, _, f = wg.shape
    assert ne == N_EXPERTS and tm % MOE_CHUNK == 0
    nt = t // tm
    tok = lambda b, i: (b, i, 0)
    a, gates_t, rank_t, cnt = pl.pallas_call(
        functools.partial(_moe_route_kernel, tm=tm),
        out_shape=(jax.ShapeDtypeStruct((bsz, t, d), BF16),
                   jax.ShapeDtypeStruct((bsz * nt, ne, tm), F32),
                   jax.ShapeDtypeStruct((bsz * nt, ne, tm), F32),
                   jax.ShapeDtypeStruct((bsz * nt, ne, LANES), I32)),
        grid=(bsz, nt),
        in_specs=[pl.BlockSpec((1, tm, d), tok),
                  pl.BlockSpec((1, d), lambda b, i: (0, 0)),
                  pl.BlockSpec((1, 6, d), lambda b, i: (b, 0, 0)),
                  pl.BlockSpec((ne, d), lambda b, i: (0, 0))],
        out_specs=(pl.BlockSpec((1, tm, d), tok),
                   pl.BlockSpec((1, ne, tm), lambda b, i: (b * nt + i, 0, 0)),
                   pl.BlockSpec((1, ne, tm), lambda b, i: (b * nt + i, 0, 0)),
                   pl.BlockSpec((1, ne, LANES), lambda b, i: (b * nt + i, 0, 0))),
        compiler_params=_cparams(("arbitrary", "arbitrary")),
        name="moe_route",
    )(h, g[None, :], mod, router.T)
    nsub = tm // MOE_CHUNK
    counts = cnt[:, :, :nsub + 1].reshape(-1)
    y = pl.pallas_call(
        functools.partial(_moe_expert_kernel, tm=tm, ne=ne),
        out_shape=jax.ShapeDtypeStruct((bsz * t, d), F32),
        grid_spec=pltpu.PrefetchScalarGridSpec(
            num_scalar_prefetch=1,
            grid=(bsz * nt, ne, f // tf),
            in_specs=[pl.BlockSpec((tm, d), lambda i, e, j, cnt: (i, 0)),
                      pl.BlockSpec((1, ne, nsub, MOE_CHUNK), lambda i, e, j, cnt: (i, 0, 0, 0)),
                      pl.BlockSpec((1, tm, ne), lambda i, e, j, cnt: (i, 0, 0)),
                      pl.BlockSpec((1, tm, ne), lambda i, e, j, cnt: (i, 0, 0)),
                      pl.BlockSpec((1, d, tf), lambda i, e, j, cnt: (e, 0, j)),
                      pl.BlockSpec((1, d, tf), lambda i, e, j, cnt: (e, 0, j)),
                      pl.BlockSpec((1, tf, d), lambda i, e, j, cnt: (e, j, 0))],
            out_specs=pl.BlockSpec((tm, d), lambda i, e, j, cnt: (i, 0)),
            scratch_shapes=[pltpu.VMEM((tm, d), BF16), pltpu.VMEM((tm, d), F32),
                            pltpu.VMEM((tm, 1), F32), pltpu.VMEM((tm, 1), F32)]),
        compiler_params=_cparams(("arbitrary", "arbitrary", "arbitrary")),
        name="moe_experts",
    )(counts, a.reshape(bsz * t, d), rank_t.reshape(bsz * nt, ne, nsub, MOE_CHUNK), rank_t.transpose(0, 2, 1),
      gates_t.transpose(0, 2, 1), wg, wu, wd)
    tr = _Tiles().proj
    return pl.pallas_call(
        _residual_kernel,
        out_shape=jax.ShapeDtypeStruct((bsz, t, d), F32),
        grid=(bsz, t // tr),
        in_specs=[pl.BlockSpec((1, tr, d), tok), pl.BlockSpec((1, tr, d), tok),
                  pl.BlockSpec((1, 6, d), lambda b, i: (b, 0, 0))],
        out_specs=pl.BlockSpec((1, tr, d), tok),
        compiler_params=_cparams(("arbitrary", "arbitrary")),
        name="moe_residual",
    )(h, y.reshape(bsz, t, d), mod)


class _Tiles(NamedTuple):
    proj: int = 512
    attn_q: int = 256
    sel_k: int = 1024
    s5_rows: int = 256
    ffn_m: int = 1024
    moe_m: int = 2048
    ffn_f: int = 512


def _tiles(t):
    base = _Tiles()
    return base._replace(s5_rows=min(base.s5_rows, t // SSM_CHUNK), ffn_m=min(base.ffn_m, t),
                         moe_m=min(base.moe_m, t))


def _pack_w_in(w_in, dq):
    o1 = 3 * dq
    o2 = o1 + dq
    o3 = o2 + 3 * dq
    sb_q = w_in[:, o2:o2 + dq] * (HEAD_DIM ** -0.5)
    pad = jnp.zeros((w_in.shape[0], GATE_PAD - (w_in.shape[1] - (o3 + dq + 6 * NSA_KV_HEADS * HEAD_DIM))),
                    w_in.dtype)
    cols = [w_in[:, :o1], sb_q, w_in[:, o2 + dq:o3], w_in[:, o1:o2], w_in[:, o3:], pad]
    return jnp.concatenate(cols, axis=1).astype(BF16), o1 + 3 * dq


def _token_mixing(h, mod, positions, rope, g_mix, w_in, w_out, conv_w, ssm, nsa):
    bsz, t, d = h.shape
    dq = d // 4
    tl = _tiles(t)
    w_packed, nb = _pack_w_in(w_in, dq)
    zb, zs, zf = _in_proj(h, g_mix, mod, w_packed, nb, dq, tm=tl.proj)
    y_conv = _short_conv(zb, conv_w, tt=tl.proj)
    y_ssm = _s5(zs, *ssm[:-1], tt=tl.s5_rows)
    y_sb = _stick_breaking(zb, dq, tq=tl.attn_q)
    o_c, o_s, o_w = _nsa_mixer(zf, dq, positions, rope, nsa)
    return _out_proj(h, [y_conv, y_ssm, y_sb, o_c, o_s, o_w], mod, ssm[-1].astype(BF16), w_out.astype(BF16),
                     tm=tl.proj)


def _nsa_mixer(zf, wq, positions, rope, nsa):
    bsz, t, _ = zf.shape
    q_norm_g, k_norm_g, pos_k, pos_v, k_w1, k_w2, v_w1, v_w2 = nsa
    half = HEAD_DIM // 2
    inv_freq = jnp.power(jnp.float32(ROPE_THETA), -jnp.arange(half, dtype=F32) / half)
    tl = _tiles(t)
    q_rot, ck, cv, kaug, vaug, kwin, vwin = _nsa_prep(zf, rope, q_norm_g, k_norm_g, wq, tm=tl.proj)
    nch = t // CMP_STRIDE
    end_idx = jnp.minimum(jnp.arange(nch) * CMP_STRIDE + CMP_LEN - 1, t - 1)
    angc = jnp.tile(positions[:, end_idx].astype(F32)[..., None] * inv_freq, (1, 1, 2))
    kc, vc = _compress(ck, cv, k_w1, k_w2, v_w1, v_w2, pos_k, pos_v, k_norm_g, angc)
    o_c, sel = _cmp_attn(q_rot, kc, vc, zf, tq=tl.attn_q)
    o_s = _sel_attn(q_rot, sel, kaug, vaug, zf, tq=tl.attn_q, tk=tl.sel_k)
    o_w = _win_attn(q_rot, kwin, vwin, zf, tq=tl.attn_q)
    return o_c, o_s, o_w


def kernel(x, c, positions, ada_w, ada_b, norm_mix_g, norm_ffn_g, w_in, w_out, conv_w, ssm_lam_re, ssm_lam_im, ssm_b_re, ssm_b_im, ssm_c_re, ssm_c_im, ssm_d, ssm_log_dt, ssm_w_glu, nsa_q_norm_g, nsa_k_norm_g, cmp_pos_k, cmp_pos_v, cmp_k_w1, cmp_k_w2, cmp_v_w1, cmp_v_w2, ffn_w_gate, ffn_w_up, ffn_w_down, moe_router, moe_w_gate, moe_w_up, moe_w_down):
    depth = ada_w.shape[0]
    mods = _modulation(c, ada_w, ada_b)
    rope = _rope_tables(positions)
    h = x
    for layer in range(depth):
        mod = mods[layer]
        ssm = (ssm_lam_re[layer], ssm_lam_im[layer], ssm_b_re[layer], ssm_b_im[layer], ssm_c_re[layer],
               ssm_c_im[layer], ssm_d[layer], ssm_log_dt[layer], ssm_w_glu[layer])
        nsa = (nsa_q_norm_g[layer], nsa_k_norm_g[layer], cmp_pos_k[layer], cmp_pos_v[layer],
               cmp_k_w1[layer], cmp_k_w2[layer], cmp_v_w1[layer], cmp_v_w2[layer])
        h = _token_mixing(h, mod, positions, rope, norm_mix_g[layer], w_in[layer], w_out[layer],
                          conv_w[layer], ssm, nsa)
        i = layer // 2
        tl = _tiles(h.shape[1])
        if layer % 2 == 0:
            h = _dense_ffn(h, norm_ffn_g[layer], mod, ffn_w_gate[i].astype(BF16), ffn_w_up[i].astype(BF16),
                           ffn_w_down[i].astype(BF16), tm=tl.ffn_m, tf=tl.ffn_f)
        else:
            h = _moe_ffn(h, norm_ffn_g[layer], mod, moe_router[i], moe_w_gate[i].astype(BF16),
                         moe_w_up[i].astype(BF16), moe_w_down[i].astype(BF16), tm=tl.moe_m, tf=tl.ffn_f)
    return h
```
